```python
import jax
import jax.numpy as jnp
from jax import lax
import numpy as np

D_MODEL = 1024
BATCH = 8
SEQ = 4096
DEPTH = 4

N_ATTN_HEADS = 8
Q_RANK = 256
KV_RANK = 128
ATTN_V_DIM = 64
ATTN_WIDTH = N_ATTN_HEADS * ATTN_V_DIM
ATTN_SCALE = KV_RANK ** -0.5
IDX_HEADS = 8
IDX_DIM = 64
IDX_W_SCALE = (IDX_HEADS * IDX_DIM) ** -0.5
TOPK_MAX = 256
Q_BLOCK = 128
N_REC_HEADS = 8
REC_K_DIM = 64
REC_V_DIM = 64
REC_KEY_WIDTH = N_REC_HEADS * REC_K_DIM
REC_WIDTH = N_REC_HEADS * REC_V_DIM
CHUNK = 64
N_GROUPS = 4
EXPERTS_PER_GROUP = 8
N_EXPERTS = N_GROUPS * EXPERTS_PER_GROUP
TOP_K_IN_GROUP = 2
D_EXPERT = 512
MOE_BLOCK = 128
EPS = 1e-6
NEG = -1e30
TINY = 1e-30
SPLIT_SIZES = (Q_RANK, KV_RANK, IDX_DIM, IDX_HEADS, REC_KEY_WIDTH, REC_KEY_WIDTH, REC_WIDTH, REC_WIDTH, D_MODEL, D_MODEL)
D_IN = sum(SPLIT_SIZES)

kernel_name = 'hybrid_dsa_hgrn2_hmoe_adaln'


def rmsnorm(x, g):
    xf = x.astype(jnp.float32)
    y = xf * lax.rsqrt(jnp.mean(xf * xf, axis=-1, keepdims=True) + EPS)
    return y.astype(x.dtype) * g


def dsa_attention(c_q, c_kv, k_idx_pre, w_idx_pre, g_cq, g_ckv, g_kidx, w_q_up, w_idx_q, w_v_up):
    B, S, _ = c_q.shape
    cq = rmsnorm(c_q, g_cq)
    q = (cq @ w_q_up).reshape(B, S, N_ATTN_HEADS, KV_RANK)
    kv = rmsnorm(c_kv, g_ckv)
    q_idx = (cq @ w_idx_q).reshape(B, S, IDX_HEADS, IDX_DIM)
    k_idx = rmsnorm(k_idx_pre, g_kidx)
    w_idx = w_idx_pre * IDX_W_SCALE
    topk = min(TOPK_MAX, S // 4)
    qb = min(Q_BLOCK, S)
    nb = S // qb
    key_pos = jnp.arange(S)

    def to_blocks(a):
        return jnp.moveaxis(a.reshape(B, nb, qb, *a.shape[2:]), 1, 0)

    def block(args):
        q_b, qi_b, wi_b, b_id = args
        q_pos = b_id * qb + jnp.arange(qb)
        causal = key_pos[None, :] <= q_pos[:, None]
        logits = jnp.einsum('bqhd,bsd->bqhs', qi_b, k_idx)
        score = jnp.einsum('bqh,bqhs->bqs', wi_b, jax.nn.relu(logits)).astype(jnp.float32)
        score = jnp.where(causal[None], score, NEG)
        _, sel = lax.top_k(score, topk)
        kv_sel = jax.vmap(lambda kv_b, sel_b: kv_b[sel_b])(kv, sel)
        valid = sel <= q_pos[None, :, None]
        s = jnp.einsum('bqhr,bqkr->bqhk', q_b, kv_sel).astype(jnp.float32) * ATTN_SCALE
        s = jnp.where(valid[:, :, None, :], s, NEG)
        p = jax.nn.softmax(s, axis=-1).astype(kv.dtype)
        return jnp.einsum('bqhk,bqkr->bqhr', p, kv_sel)

    o = lax.map(block, (to_blocks(q), to_blocks(q_idx), to_blocks(w_idx), jnp.arange(nb)))
    o = jnp.moveaxis(o, 0, 1).reshape(B, S, N_ATTN_HEADS, KV_RANK)
    o = jnp.einsum('bshr,hrv->bshv', o, w_v_up)
    return o.reshape(B, S, ATTN_WIDTH)


def hgrn2(q_pre, f_pre, i_pre, o_gate_pre, lower_bound, g_out):
    B, S, _ = q_pre.shape
    nc = S // CHUNK
    z = f_pre.astype(jnp.float32)
    lb = lower_bound.astype(jnp.float32)
    sig = jax.nn.sigmoid(z)
    f = lb + (1.0 - lb) * sig
    log_f = jnp.log(jnp.maximum(f, TINY))
    k = (1.0 - lb) * (1.0 - sig)
    q = jax.nn.silu(q_pre.astype(jnp.float32))
    v = i_pre.astype(jnp.float32)

    def to_chunks(a, width):
        return a.reshape(B, nc, CHUNK, N_REC_HEADS, width).transpose(1, 0, 3, 2, 4)

    tri = jnp.tril(jnp.ones((CHUNK, CHUNK), dtype=bool))[:, :, None]

    def step(state, inp):
        q_c, k_c, v_c, a_c = inp
        A = jnp.cumsum(a_c, axis=2)
        diff = A[:, :, :, None, :] - A[:, :, None, :, :]
        decay = jnp.where(tri, jnp.exp(jnp.where(tri, diff, NEG)), 0.0)
        scores = jnp.einsum('bhtk,bhsk,bhtsk->bhts', q_c, k_c, decay)
        o = (jnp.einsum('bhts,bhsv->bhtv', scores, v_c)
             + jnp.einsum('bhtk,bhkv->bhtv', q_c * jnp.exp(A), state))
        A_end = A[:, :, -1:, :]
        state = (jnp.exp(A_end[:, :, 0, :])[..., None] * state
                 + jnp.einsum('bhsk,bhsv->bhkv', k_c * jnp.exp(A_end - A), v_c))
        return state, o

    state0 = jnp.zeros((B, N_REC_HEADS, REC_K_DIM, REC_V_DIM), jnp.float32)
    _, o = lax.scan(step, state0, (to_chunks(q, REC_K_DIM), to_chunks(k, REC_K_DIM),
                                   to_chunks(v, REC_V_DIM), to_chunks(log_f, REC_K_DIM)))
    o = o.transpose(1, 0, 3, 2, 4).reshape(B, S, N_REC_HEADS, REC_V_DIM).astype(q_pre.dtype)
    gate = jax.nn.silu(o_gate_pre).reshape(B, S, N_REC_HEADS, REC_V_DIM)
    return (rmsnorm(o, g_out) * gate).reshape(B, S, REC_WIDTH)


def hier_moe(h, w_grp, b_grp, w_exp_router, b_exp_router, w_gate, w_up, w_down):
    B, S, D = h.shape
    T = B * S
    x = h.reshape(T, D)
    grp_logits = (x @ w_grp + b_grp).astype(jnp.float32)
    grp_prob = jax.nn.softmax(grp_logits, axis=-1)
    g_sel = jnp.argmax(grp_logits, axis=-1).astype(jnp.int32)
    grp_gate = jnp.take_along_axis(grp_prob, g_sel[:, None], axis=1)[:, 0]
    exp_logits = (x @ w_exp_router + b_exp_router).astype(jnp.float32).reshape(T, N_GROUPS, EXPERTS_PER_GROUP)
    exp_logits = jnp.take_along_axis(exp_logits, g_sel[:, None, None], axis=1)[:, 0]
    top_val, top_idx = lax.top_k(exp_logits, TOP_K_IN_GROUP)
    gates = jax.nn.softmax(top_val, axis=-1) * grp_gate[:, None]
    expert_id = (g_sel[:, None] * EXPERTS_PER_GROUP + top_idx.astype(jnp.int32)).reshape(-1)
    token_id = jnp.repeat(jnp.arange(T, dtype=jnp.int32), TOP_K_IN_GROUP)
    gate = gates.reshape(-1)
    N = T * TOP_K_IN_GROUP
    order = jnp.argsort(expert_id)
    e_sorted = expert_id[order]
    counts = jnp.zeros((N_EXPERTS,), jnp.int32).at[expert_id].add(1)
    padded = (counts + MOE_BLOCK - 1) // MOE_BLOCK * MOE_BLOCK
    pad_end = jnp.cumsum(padded)
    pad_start = pad_end - padded
    start = jnp.cumsum(counts) - counts
    dest = pad_start[e_sorted] + (jnp.arange(N, dtype=jnp.int32) - start[e_sorted])
    P = N + N_EXPERTS * MOE_BLOCK
    n_blk = P // MOE_BLOCK
    buf_tok = jnp.zeros((P,), jnp.int32).at[dest].set(token_id[order])
    buf_gate = jnp.zeros((P,), h.dtype).at[dest].set(gate[order].astype(h.dtype))
    blk_expert = jnp.minimum(jnp.searchsorted(pad_end, jnp.arange(n_blk, dtype=jnp.int32) * MOE_BLOCK, side='right'),
                             N_EXPERTS - 1).astype(jnp.int32)
    xs = x[buf_tok].reshape(n_blk, MOE_BLOCK, D)

    def expert_block(args):
        xb, e = args
        hid = jax.nn.silu(xb @ w_gate[e]) * (xb @ w_up[e])
        return hid @ w_down[e]

    ys = lax.map(expert_block, (xs, blk_expert)).reshape(P, D)
    out = jnp.zeros((T, D), h.dtype).at[buf_tok].add(ys * buf_gate[:, None])
    return out.reshape(B, S, D)


def setup_inputs(seed: int = 0) -> dict:
    key = jax.random.key(seed)
    ks = jax.random.split(key, 32)
    L, D = DEPTH, D_MODEL

    def nrm(k, shape, scale):
        return jax.random.normal(k, shape, jnp.float32) * scale

    def gain(k, shape):
        return 1.0 + 0.05 * jax.random.normal(k, shape, jnp.float32)

    return {
        'x': nrm(ks[0], (BATCH, SEQ, D), 1.0),
        'c': nrm(ks[1], (BATCH, D), 1.0),
        'w_mod': nrm(ks[2], (L, D, 6 * D), 0.5 * D ** -0.5),
        'b_mod': nrm(ks[3], (L, 6 * D), 0.02),
        'g_norm1': gain(ks[4], (L, D)),
        'g_norm2': gain(ks[5], (L, D)),
        'w_in': nrm(ks[6], (L, D, D_IN), D ** -0.5),
        'g_cq': gain(ks[7], (L, Q_RANK)),
        'g_ckv': gain(ks[8], (L, KV_RANK)),
        'g_kidx': gain(ks[9], (L, IDX_DIM)),
        'w_q_up': nrm(ks[10], (L, Q_RANK, N_ATTN_HEADS * KV_RANK), Q_RANK ** -0.5),
        'w_idx_q': nrm(ks[11], (L, Q_RANK, IDX_HEADS * IDX_DIM), Q_RANK ** -0.5),
        'w_v_up': nrm(ks[12], (L, N_ATTN_HEADS, KV_RANK, ATTN_V_DIM), KV_RANK ** -0.5),
        'lb_logits': nrm(ks[13], (L, REC_KEY_WIDTH), 1.0),
        'g_rec': gain(ks[14], (L, REC_V_DIM)),
        'w_branch_a': nrm(ks[15], (L, ATTN_WIDTH, D), ATTN_WIDTH ** -0.5),
        'w_branch_r': nrm(ks[16], (L, REC_WIDTH, D), REC_WIDTH ** -0.5),
        'w_out': nrm(ks[17], (L, D, D), D ** -0.5),
        'w_grp': nrm(ks[18], (L, D, N_GROUPS), D ** -0.5),
        'b_grp': nrm(ks[19], (L, N_GROUPS), 0.01),
        'w_exp_router': nrm(ks[20], (L, D, N_EXPERTS), D ** -0.5),
        'b_exp_router': nrm(ks[21], (L, N_EXPERTS), 0.01),
        'w_gate': nrm(ks[22], (L, N_EXPERTS, D, D_EXPERT), D ** -0.5),
        'w_up': nrm(ks[23], (L, N_EXPERTS, D, D_EXPERT), D ** -0.5),
        'w_down': nrm(ks[24], (L, N_EXPERTS, D_EXPERT, D), D_EXPERT ** -0.5),
        'g_final': gain(ks[25], (D,)),
    }


def reference(x, c, w_mod, b_mod, g_norm1, g_norm2, w_in, g_cq, g_ckv, g_kidx, w_q_up, w_idx_q, w_v_up,
              lb_logits, g_rec, w_branch_a, w_branch_r, w_out, w_grp, b_grp, w_exp_router, b_exp_router,
              w_gate, w_up, w_down, g_final):
    split_at = np.cumsum(SPLIT_SIZES)[:-1].tolist()
    lb_p = jax.nn.softmax(lb_logits.astype(jnp.float32), axis=0)
    lower_bounds = jnp.clip(jnp.cumsum(lb_p, axis=0) - lb_p[0:1], 0.0, 1.0)
    c_act = jax.nn.silu(c)
    for l in range(DEPTH):
        mod = c_act @ w_mod[l] + b_mod[l]
        sh1, sc1, gt1, sh2, sc2, gt2 = jnp.split(mod[:, None, :], 6, axis=-1)
        h = rmsnorm(x, g_norm1[l]) * (1.0 + sc1) + sh1
        u = h @ w_in[l]
        c_q, c_kv, k_idx_pre, w_idx_pre, q_rec, f_rec, i_rec, og_rec, ga, gr = jnp.split(u, split_at, axis=-1)
        y_a = dsa_attention(c_q, c_kv, k_idx_pre, w_idx_pre, g_cq[l], g_ckv[l], g_kidx[l],
                            w_q_up[l], w_idx_q[l], w_v_up[l])
        y_r = hgrn2(q_rec, f_rec, i_rec, og_rec, lower_bounds[l], g_rec[l])
        merged = jax.nn.sigmoid(ga) * (y_a @ w_branch_a[l]) + jax.nn.sigmoid(gr) * (y_r @ w_branch_r[l])
        x = x + gt1 * (merged @ w_out[l])
        h2 = rmsnorm(x, g_norm2[l]) * (1.0 + sc2) + sh2
        x = x + gt2 * hier_moe(h2, w_grp[l], b_grp[l], w_exp_router[l], b_exp_router[l],
                               w_gate[l], w_up[l], w_down[l])
    return rmsnorm(x, g_final)
```

```python
import functools

import jax
import jax.numpy as jnp
import numpy as np
from jax import lax
from jax.experimental import pallas as pl
from jax.experimental.pallas import tpu as pltpu

N_ATTN_HEADS = 8
Q_RANK = 256
KV_RANK = 128
ATTN_V_DIM = 64
ATTN_WIDTH = N_ATTN_HEADS * ATTN_V_DIM
ATTN_SCALE = KV_RANK ** -0.5
IDX_HEADS = 8
IDX_DIM = 64
IDX_W_SCALE = (IDX_HEADS * IDX_DIM) ** -0.5
TOPK_MAX = 256
N_REC_HEADS = 8
REC_K_DIM = 64
REC_V_DIM = 64
REC_WIDTH = N_REC_HEADS * REC_V_DIM
N_GROUPS = 4
EXPERTS_PER_GROUP = 8
N_EXPERTS = N_GROUPS * EXPERTS_PER_GROUP
TOP_K_IN_GROUP = 2
D_EXPERT = 512
EPS = 1e-6
NEG = -1e30
TINY = 1e-30

LANES = 128
Q_BLOCK = 128
SUB = 16
VMEM_LIMIT = 56 * 1024 * 1024

F32 = jnp.float32
BF16 = jnp.bfloat16
I32 = jnp.int32


def _cparams(sem):
    return pltpu.CompilerParams(dimension_semantics=sem, vmem_limit_bytes=VMEM_LIMIT)


def _nt_dot(a, b):
    return lax.dot_general(a, b, (((1,), (1,)), ((), ())), preferred_element_type=F32)


def _tn_dot(a, b):
    return lax.dot_general(a, b, (((0,), (0,)), ((), ())), preferred_element_type=F32)


def _dot(a, b):
    return jnp.dot(a, b, preferred_element_type=F32)


def _split_dot(a_f32, b_hi, b_lo):
    a_hi = a_f32.astype(BF16)
    a_lo = (a_f32 - a_hi.astype(F32)).astype(BF16)
    return _dot(a_hi, b_hi) + (_dot(a_hi, b_lo) + _dot(a_lo, b_hi))


def _hi_lo(w):
    hi = w.astype(BF16)
    lo = (w - hi.astype(F32)).astype(BF16)
    return hi, lo


def _mod_kernel(c_ref, w_ref, b_ref, o_ref):
    c = c_ref[...]
    ca = c * jax.nn.sigmoid(c)
    w = w_ref[0]
    w_hi = w.astype(BF16)
    w_lo = (w - w_hi.astype(F32)).astype(BF16)
    o_ref[0] = _split_dot(ca, w_hi, w_lo) + b_ref[0]


def _modulation(c, w_mod, b_mod):
    L, D, D6 = w_mod.shape
    B = c.shape[0]
    tn = 1024
    return pl.pallas_call(
        _mod_kernel,
        out_shape=jax.ShapeDtypeStruct((L, B, D6), F32),
        grid=(L, D6 // tn),
        in_specs=[
            pl.BlockSpec((B, D), lambda l, n: (0, 0)),
            pl.BlockSpec((1, D, tn), lambda l, n: (l, 0, n)),
            pl.BlockSpec((1, 1, tn), lambda l, n: (l, 0, n)),
        ],
        out_specs=pl.BlockSpec((1, B, tn), lambda l, n: (l, 0, n)),
        compiler_params=_cparams(("arbitrary", "arbitrary")),
        name="modulation",
    )(c, w_mod, b_mod.reshape(L, 1, D6))


def _rms(x, eps=EPS):
    return x * lax.rsqrt(jnp.mean(x * x, axis=-1, keepdims=True) + eps)


def _proj_kernel(layer, x_ref, sh_ref, sc_ref, g1_ref, wa_ref, wrec_ref, wg_ref, gcq_ref, gckv_ref,
                 gk_ref, wq_ref, wiq_ref, lb_ref,
                 q_out, qi_out, kv_out, kidx_out, wt_out, qs_out, k_out, v_out, lf_out, og_out,
                 sga_out, sgr_out):
    x = x_ref[0]
    tm = x.shape[0]
    h = (_rms(x) * g1_ref[...]) * (1.0 + sc_ref[0]) + sh_ref[0]
    hb = h.astype(BF16)

    ua = _dot(hb, wa_ref[...])
    cq = (_rms(ua[:, :Q_RANK]) * gcq_ref[...]).astype(BF16)
    q = (_dot(cq, wq_ref[...]) * ATTN_SCALE).astype(BF16)
    qi = _dot(cq, wiq_ref[...]).astype(BF16)
    for qb in range(tm // Q_BLOCK):
        rows = slice(qb * Q_BLOCK, (qb + 1) * Q_BLOCK)
        for hd in range(N_ATTN_HEADS):
            q_out[0, qb, hd * Q_BLOCK:(hd + 1) * Q_BLOCK, :] = q[rows, hd * LANES:(hd + 1) * LANES]
        for t in range(IDX_HEADS * IDX_DIM // LANES):
            qi_out[0, qb, t * Q_BLOCK:(t + 1) * Q_BLOCK, :] = qi[rows, t * LANES:(t + 1) * LANES]
    kv_out[0] = (_rms(ua[:, Q_RANK:Q_RANK + KV_RANK]) * gckv_ref[...]).astype(BF16)

    t3 = ua[:, Q_RANK + KV_RANK:]
    lane = lax.broadcasted_iota(I32, t3.shape, 1)
    ms = jnp.sum(jnp.where(lane < IDX_DIM, t3 * t3, 0.0), axis=-1, keepdims=True) * (1.0 / IDX_DIM)
    ka = t3 * lax.rsqrt(ms + EPS) * gk_ref[...]
    kb = pltpu.roll(ka, IDX_DIM, 1)
    kidx_out[0] = jnp.concatenate([ka, kb], axis=1).astype(BF16)
    wt_out[0] = t3 * IDX_W_SCALE

    ur = _dot(hb, wrec_ref[...])
    kw = N_REC_HEADS * REC_K_DIM
    lbl = lb_ref[...]
    e = jnp.exp(lbl - jnp.max(lbl, axis=0, keepdims=True))
    p = e / jnp.sum(e, axis=0, keepdims=True)
    lb = jnp.zeros((1, kw), F32)
    for j in range(1, layer + 1):
        lb = lb + p[j:j + 1, :]
    lb = jnp.clip(lb, 0.0, 1.0)
    sig = jax.nn.sigmoid(ur[:, kw:2 * kw])
    f = lb + (1.0 - lb) * sig
    lf_out[0] = jnp.log(jnp.maximum(f, TINY))
    k_out[0] = ((1.0 - lb) * (1.0 - sig)).astype(BF16)
    qr = ur[:, :kw]
    qs_out[0] = (qr * jax.nn.sigmoid(qr)).astype(BF16)
    v_out[0] = ur[:, 2 * kw:2 * kw + REC_WIDTH].astype(BF16)
    og = ur[:, 2 * kw + REC_WIDTH:]
    og_out[0] = (og * jax.nn.sigmoid(og)).astype(BF16)

    ug = _dot(hb, wg_ref[...])
    D = x.shape[1]
    sga_out[0] = jax.nn.sigmoid(ug[:, :D]).astype(BF16)
    sgr_out[0] = jax.nn.sigmoid(ug[:, D:]).astype(BF16)


def _proj(layer, x, sh1, sc1, g1, wa, wrec, wg, gcq, gckv, gk, wq, wiq, lb_logits, tm):
    B, S, D = x.shape
    nqb = tm // Q_BLOCK
    nq = S // Q_BLOCK
    kw = N_REC_HEADS * REC_K_DIM
    c2 = lambda b, i: (0, 0)
    tok = lambda w: pl.BlockSpec((1, tm, w), lambda b, i: (b, i, 0))
    full = lambda a: pl.BlockSpec(a.shape, c2)
    out_shapes = (
        jax.ShapeDtypeStruct((B, nq, N_ATTN_HEADS * Q_BLOCK, KV_RANK), BF16),
        jax.ShapeDtypeStruct((B, nq, IDX_HEADS * IDX_DIM // LANES * Q_BLOCK, LANES), BF16),
        jax.ShapeDtypeStruct((B, S, KV_RANK), BF16),
        jax.ShapeDtypeStruct((B, S, 2 * LANES), BF16),
        jax.ShapeDtypeStruct((B, S, LANES), F32),
        jax.ShapeDtypeStruct((B, S, kw), BF16),
        jax.ShapeDtypeStruct((B, S, kw), BF16),
        jax.ShapeDtypeStruct((B, S, REC_WIDTH), BF16),
        jax.ShapeDtypeStruct((B, S, kw), F32),
        jax.ShapeDtypeStruct((B, S, REC_WIDTH), BF16),
        jax.ShapeDtypeStruct((B, S, D), BF16),
        jax.ShapeDtypeStruct((B, S, D), BF16),
    )
    out_specs = (
        pl.BlockSpec((1, nqb, N_ATTN_HEADS * Q_BLOCK, KV_RANK), lambda b, i: (b, i, 0, 0)),
        pl.BlockSpec((1, nqb, IDX_HEADS * IDX_DIM // LANES * Q_BLOCK, LANES), lambda b, i: (b, i, 0, 0)),
        tok(KV_RANK), tok(2 * LANES), tok(LANES), tok(kw), tok(kw), tok(REC_WIDTH), tok(kw), tok(REC_WIDTH),
        tok(D), tok(D),
    )
    vec = lambda: pl.BlockSpec((1, 1, D), lambda b, i: (b, 0, 0))
    return pl.pallas_call(
        functools.partial(_proj_kernel, layer),
        out_shape=out_shapes,
        grid=(B, S // tm),
        in_specs=[tok(D), vec(), vec(), full(g1), full(wa), full(wrec), full(wg), full(gcq), full(gckv),
                  full(gk), full(wq), full(wiq), full(lb_logits)],
        out_specs=out_specs,
        compiler_params=_cparams(("arbitrary", "arbitrary")),
        name="proj",
    )(x, sh1, sc1, g1, wa, wrec, wg, gcq, gckv, gk, wq, wiq, lb_logits)


def _sort_key(score):
    bits = lax.bitcast_convert_type(score, I32)
    return bits ^ (lax.shift_right_arithmetic(bits, 31) & jnp.int32(0x7FFFFFFF))


def _dsa_kernel(S, KC, TOPK, q_ref, qi_ref, wt_ref, kv_ref, kidx_ref, wbd_ref, ya_ref,
                keys_ref, acc_ref, m_ref, l_ref):
    i = pl.program_id(1)
    n_ch = ((i + 1) * Q_BLOCK + KC - 1) // KC
    qpos = i * Q_BLOCK + lax.broadcasted_iota(I32, (Q_BLOCK, 1), 0)
    wt = wt_ref[0]
    wcol = [wt[:, IDX_DIM + hd:IDX_DIM + hd + 1] for hd in range(IDX_HEADS)]
    qi = qi_ref[0, 0]
    n_t = IDX_HEADS * IDX_DIM // LANES

    def kpos_of(c):
        return c * KC + lax.broadcasted_iota(I32, (1, KC), 1)

    def score_body(c, carry):
        koff = pl.multiple_of(c * KC, KC)
        ka = kidx_ref[0, pl.ds(koff, KC), 0:LANES]
        kb = kidx_ref[0, pl.ds(koff, KC), LANES:2 * LANES]
        la = _nt_dot(qi, ka)
        lb = _nt_dot(qi, kb)
        score = jnp.zeros((Q_BLOCK, KC), F32)
        for t in range(n_t):
            rows = slice(t * Q_BLOCK, (t + 1) * Q_BLOCK)
            score = score + wcol[2 * t] * jnp.maximum(la[rows], 0.0)
            score = score + wcol[2 * t + 1] * jnp.maximum(lb[rows], 0.0)
        score = jnp.where(kpos_of(c) <= qpos, score, NEG)
        keys_ref[:, pl.ds(koff, KC)] = _sort_key(score)
        return carry

    lax.fori_loop(0, n_ch, score_body, 0)

    def count(pred):
        def body(c, acc):
            kk = keys_ref[:, pl.ds(pl.multiple_of(c * KC, KC), KC)]
            ones = jnp.where(pred(kk), 1, 0).astype(I32)
            for j in range(KC // LANES):
                acc = acc + ones[:, j * LANES:(j + 1) * LANES]
            return acc
        acc = lax.fori_loop(0, n_ch, body, jnp.zeros((Q_BLOCK, LANES), I32))
        return jnp.sum(acc, axis=1, keepdims=True)

    def bis_body(it, v):
        cand = v + lax.shift_left(jnp.int32(1), 31 - it)
        cnt = count(lambda kk: kk >= cand)
        return jnp.where(cnt >= TOPK, cand, v)

    v = lax.fori_loop(0, 32, bis_body, jnp.full((Q_BLOCK, 1), -2 ** 31, I32))
    n_gt = count(lambda kk: kk > v)
    n_ge = count(lambda kk: kk >= v)
    need = (TOPK - n_gt).astype(F32)
    any_split = jnp.max(jnp.where(n_ge > TOPK, 1, 0)) > 0

    @pl.when(any_split)
    def _():
        tri = jnp.where(lax.broadcasted_iota(I32, (KC, KC), 0) <= lax.broadcasted_iota(I32, (KC, KC), 1),
                        1.0, 0.0).astype(BF16)

        def tie_body(c, seen):
            koff = pl.multiple_of(c * KC, KC)
            kk = keys_ref[:, pl.ds(koff, KC)]
            tie = kk == v
            tie_f = jnp.where(tie, 1.0, 0.0)
            rank = _dot(tie_f.astype(BF16), tri) + seen
            keys_ref[:, pl.ds(koff, KC)] = jnp.where(tie & (rank > need), v - 1, kk)
            return seen + jnp.sum(tie_f, axis=1, keepdims=True)

        lax.fori_loop(0, n_ch, tie_body, jnp.zeros((Q_BLOCK, 1), F32))

    m_ref[...] = jnp.full(m_ref.shape, NEG, F32)
    l_ref[...] = jnp.zeros(l_ref.shape, F32)
    acc_ref[...] = jnp.zeros(acc_ref.shape, F32)

    def att_body(c, carry):
        koff = pl.multiple_of(c * KC, KC)
        kvc = kv_ref[0, pl.ds(koff, KC), :]
        kk = keys_ref[:, pl.ds(koff, KC)]
        msk = (kk >= v) & (kpos_of(c) <= qpos)
        for hd in range(N_ATTN_HEADS):
            qh = q_ref[0, 0, hd * Q_BLOCK:(hd + 1) * Q_BLOCK, :]
            s = jnp.where(msk, _nt_dot(qh, kvc), -jnp.inf)
            m_prev = m_ref[hd]
            m_next = jnp.maximum(m_prev, jnp.max(s, axis=1, keepdims=True))
            alpha = jnp.exp(m_prev - m_next)
            p = jnp.exp(s - m_next[:, 0:1])
            l_ref[hd] = alpha * l_ref[hd] + jnp.sum(p, axis=1, keepdims=True)
            acc_ref[hd] = alpha * acc_ref[hd] + _dot(p.astype(BF16), kvc)
            m_ref[hd] = m_next
        return carry

    lax.fori_loop(0, n_ch, att_body, 0)
    o = jnp.concatenate([(acc_ref[hd] / l_ref[hd]).astype(BF16) for hd in range(N_ATTN_HEADS)], axis=1)
    ya_ref[0] = _dot(o, wbd_ref[...]).astype(BF16)


def _dsa(q4, qi4, wt, kv, kidx2, wbd):
    B, nq = q4.shape[0], q4.shape[1]
    S = kv.shape[1]
    KC = min(512, S)
    topk = min(TOPK_MAX, S // 4)
    assert S % KC == 0 and KC % Q_BLOCK == 0
    return pl.pallas_call(
        functools.partial(_dsa_kernel, S, KC, topk),
        out_shape=jax.ShapeDtypeStruct((B, S, ATTN_WIDTH), BF16),
        grid=(B, nq),
        in_specs=[
            pl.BlockSpec((1, 1) + q4.shape[2:], lambda b, i: (b, i, 0, 0)),
            pl.BlockSpec((1, 1) + qi4.shape[2:], lambda b, i: (b, i, 0, 0)),
            pl.BlockSpec((1, Q_BLOCK, LANES), lambda b, i: (b, i, 0)),
            pl.BlockSpec((1, S, KV_RANK), lambda b, i: (b, 0, 0)),
            pl.BlockSpec((1, S, 2 * LANES), lambda b, i: (b, 0, 0)),
            pl.BlockSpec(wbd.shape, lambda b, i: (0, 0)),
        ],
        out_specs=pl.BlockSpec((1, Q_BLOCK, ATTN_WIDTH), lambda b, i: (b, i, 0)),
        scratch_shapes=[
            pltpu.VMEM((Q_BLOCK, S), I32),
            pltpu.VMEM((N_ATTN_HEADS, Q_BLOCK, KV_RANK), F32),
            pltpu.VMEM((N_ATTN_HEADS, Q_BLOCK, LANES), F32),
            pltpu.VMEM((N_ATTN_HEADS, Q_BLOCK, LANES), F32),
        ],
        compiler_params=_cparams(("arbitrary", "arbitrary")),
        name="dsa",
    )(q4, qi4, wt, kv, kidx2, wbd)


def _hgrn_kernel(TS, qs_ref, k_ref, v_ref, lf_ref, og_ref, grec_ref, y_ref,
                 st_ref, aloc_ref, e_ref, qh_ref, kh_ref, o_ref):
    n_tiles = REC_WIDTH // LANES

    @pl.when(pl.program_id(1) == 0)
    def _():
        st_ref[...] = jnp.zeros(st_ref.shape, F32)

    a = lf_ref[0]
    r = lax.broadcasted_iota(I32, a.shape, 0) % SUB
    for sh in (1, 2, 4, 8):
        a = a + jnp.where(r >= sh, pltpu.roll(a, sh, 0), 0.0)
    aend = jnp.where(r == SUB - 1, a, 0.0)
    for sh in (1, 2, 4, 8):
        aend = aend + jnp.where(r + sh <= SUB - 1, pltpu.roll(aend, TS - sh, 0), 0.0)
    e = jnp.exp(a)
    aloc_ref[...] = a
    e_ref[...] = e
    qh_ref[...] = (qs_ref[0].astype(F32) * e).astype(BF16)
    kh_ref[...] = (k_ref[0].astype(F32) * jnp.exp(aend - a)).astype(BF16)

    half = REC_K_DIM
    li = lax.broadcasted_iota(I32, (LANES, LANES), 0) // half
    lj = lax.broadcasted_iota(I32, (LANES, LANES), 1) // half
    bd_f = jnp.where(li == lj, 1.0, 0.0)
    bd_b = bd_f.astype(BF16)
    gsel = jnp.where(lax.broadcasted_iota(I32, (SUB, SUB * SUB), 0)
                     == lax.broadcasted_iota(I32, (SUB, SUB * SUB), 1) // SUB, 1.0, 0.0).astype(BF16)
    srow = lax.broadcasted_iota(I32, (SUB, LANES), 0)

    def sub_body(j, carry):
        r0 = pl.multiple_of(j * SUB, SUB)
        rows = pl.ds(r0, SUB)
        for p in range(n_tiles):
            lanes = slice(p * LANES, (p + 1) * LANES)
            st = st_ref[p]
            o_inter = _nt_dot(qh_ref[rows, lanes], st.astype(BF16))
            a_blk = aloc_ref[rows, lanes]
            q_blk = qs_ref[0, rows, lanes].astype(F32)
            k_blk = k_ref[0, rows, lanes].astype(F32)
            v_blk = v_ref[0, rows, lanes]
            ws = []
            for t in range(SUB):
                d = a_blk[t:t + 1, :] - a_blk
                w = jnp.exp(jnp.where(srow <= t, d, -jnp.inf)) * (k_blk * q_blk[t:t + 1, :])
                ws.append(w.astype(BF16))
            wst = jnp.concatenate(ws, axis=0)
            rexp = _dot(wst, bd_b)
            v_f = v_blk.astype(F32)
            m2 = (rexp * jnp.concatenate([v_f] * SUB, axis=0)).astype(BF16)
            o_ref[rows, lanes] = o_inter + _dot(gsel, m2)
            ktv = _tn_dot(v_blk, kh_ref[rows, lanes])
            dec = e_ref[rows, lanes][SUB - 1:SUB, :]
            st_ref[p] = st * dec + ktv * bd_f
        return carry

    lax.fori_loop(0, TS // SUB, sub_body, 0)

    o = o_ref[...]
    o2 = o * o
    hi = o2.astype(BF16)
    lo = (o2 - hi.astype(F32)).astype(BF16)
    ms = jnp.concatenate(
        [_dot(hi[:, p * LANES:(p + 1) * LANES], bd_b) + _dot(lo[:, p * LANES:(p + 1) * LANES], bd_b)
         for p in range(n_tiles)], axis=1) * (1.0 / REC_V_DIM)
    y = o * lax.rsqrt(ms + EPS) * grec_ref[...]
    y_ref[0] = (y * og_ref[0].astype(F32)).astype(BF16)


def _hgrn(qs, k, v, lf, og, grec_t, ts):
    B, S, W = qs.shape
    tok = lambda: pl.BlockSpec((1, ts, W), lambda b, i: (b, i, 0))
    return pl.pallas_call(
        functools.partial(_hgrn_kernel, ts),
        out_shape=jax.ShapeDtypeStruct((B, S, W), BF16),
        grid=(B, S // ts),
        in_specs=[tok(), tok(), tok(), tok(), tok(), pl.BlockSpec((1, W), lambda b, i: (0, 0))],
        out_specs=tok(),
        scratch_shapes=[
            pltpu.VMEM((W // LANES, LANES, LANES), F32),
            pltpu.VMEM((ts, W), F32),
            pltpu.VMEM((ts, W), F32),
            pltpu.VMEM((ts, W), BF16),
            pltpu.VMEM((ts, W), BF16),
            pltpu.VMEM((ts, W), F32),
        ],
        compiler_params=_cparams(("arbitrary", "arbitrary")),
        name="hgrn",
    )(qs, k, v, lf, og, grec_t)


def _merge_kernel(x_ref, ya_ref, yr_ref, sga_ref, sgr_ref, gt1_ref, sh2_ref, sc2_ref, g2_ref,
                  wba_ref, wbr_ref, wo_ref, wrh_ref, wrl_ref, br_ref, x1_out, h2_out, rt_out):
    merged = (sga_ref[0].astype(F32) * _dot(ya_ref[0], wba_ref[...])
              + sgr_ref[0].astype(F32) * _dot(yr_ref[0], wbr_ref[...]))
    x1 = x_ref[0] + gt1_ref[0] * _dot(merged.astype(BF16), wo_ref[...])
    x1_out[0] = x1
    h2 = (_rms(x1) * g2_ref[...]) * (1.0 + sc2_ref[0]) + sh2_ref[0]
    h2_out[0] = h2

    lg = _split_dot(h2, wrh_ref[...], wrl_ref[...]) + br_ref[...]
    lane = lax.broadcasted_iota(I32, lg.shape, 1)
    big = jnp.int32(1 << 20)
    gl = jnp.where(lane < N_GROUPS, lg, -jnp.inf)
    gmax = jnp.max(gl, axis=1, keepdims=True)
    gsel = jnp.min(jnp.where(gl == gmax, lane, big), axis=1, keepdims=True)
    ggate = 1.0 / jnp.sum(jnp.exp(gl - gmax), axis=1, keepdims=True)
    lo = N_GROUPS + EXPERTS_PER_GROUP * gsel
    el = jnp.where((lane >= lo) & (lane < lo + EXPERTS_PER_GROUP), lg, -jnp.inf)
    v1 = jnp.max(el, axis=1, keepdims=True)
    i1 = jnp.min(jnp.where(el == v1, lane, big), axis=1, keepdims=True)
    el2 = jnp.where(lane == i1, -jnp.inf, el)
    v2 = jnp.max(el2, axis=1, keepdims=True)
    i2 = jnp.min(jnp.where(el2 == v2, lane, big), axis=1, keepdims=True)
    e2 = jnp.exp(v2 - v1)
    den = 1.0 + e2
    g1 = (1.0 / den) * ggate
    g2 = (e2 / den) * ggate
    rt = jnp.where(lane == 0, (i1 - N_GROUPS).astype(F32),
                   jnp.where(lane == 1, (i2 - N_GROUPS).astype(F32),
                             jnp.where(lane == 2, g1, jnp.where(lane == 3, g2, 0.0))))
    rt_out[0] = rt


def _merge(x, ya, yr, sga, sgr, gt1, sh2, sc2, g2, wba, wbr, wo, wrh, wrl, br, tm):
    B, S, D = x.shape
    tok = lambda w: pl.BlockSpec((1, tm, w), lambda b, i: (b, i, 0))
    vec = lambda: pl.BlockSpec((1, 1, D), lambda b, i: (b, 0, 0))
    full = lambda a: pl.BlockSpec(a.shape, lambda b, i: (0,) * a.ndim)
    return pl.pallas_call(
        _merge_kernel,
        out_shape=(jax.ShapeDtypeStruct((B, S, D), F32), jax.ShapeDtypeStruct((B, S, D), F32),
                   jax.ShapeDtypeStruct((B, S, LANES), F32)),
        grid=(B, S // tm),
        in_specs=[tok(D), tok(ATTN_WIDTH), tok(REC_WIDTH), tok(D), tok(D), vec(), vec(), vec(), full(g2),
                  full(wba), full(wbr), full(wo), full(wrh), full(wrl), full(br)],
        out_specs=(tok(D), tok(D), tok(LANES)),
        compiler_params=_cparams(("arbitrary", "arbitrary")),
        name="merge",
    )(x, ya, yr, sga, sgr, gt1, sh2, sc2, g2, wba, wbr, wo, wrh, wrl, br)


def _row_gather(src_hbm, idx_ref, dst, sem, n):
    def body(r, carry):
        pltpu.make_async_copy(src_hbm.at[pl.ds(idx_ref[0, 0, r], 1)], dst.at[pl.ds(r, 1)], sem).start()
        return carry
    lax.fori_loop(0, n, body, 0)


def _row_gather_wait(src_hbm, dst, sem, n):
    pltpu.make_async_copy(src_hbm.at[pl.ds(0, n)], dst, sem).wait()


def _moe_kernel(BLK, be_ref, tok_ref, tokn_ref, gate_ref, h2_hbm, wg_ref, wu_ref, wd_ref, y_ref,
                xbuf, sem):
    i = pl.program_id(0)
    n = pl.num_programs(0)
    slot = lax.rem(i, 2)

    @pl.when(i == 0)
    def _():
        _row_gather(h2_hbm, tok_ref, xbuf.at[0], sem.at[0], BLK)

    @pl.when(i + 1 < n)
    def _():
        _row_gather(h2_hbm, tokn_ref, xbuf.at[1 - slot], sem.at[1 - slot], BLK)

    _row_gather_wait(h2_hbm, xbuf.at[slot], sem.at[slot], BLK)
    xb = xbuf[slot].astype(BF16)
    g = _dot(xb, wg_ref[0])
    hid = (g * jax.nn.sigmoid(g)) * _dot(xb, wu_ref[0])
    y_ref[...] = _dot(hid.astype(BF16), wd_ref[0]) * gate_ref[...]


def _moe(blk_expert, buf_tok3, buf_gate, h2, wgate, wup, wdown, blk):
    P = buf_gate.shape[0]
    n_blk = P // blk
    D = h2.shape[1]
    grid_spec = pltpu.PrefetchScalarGridSpec(
        num_scalar_prefetch=1,
        grid=(n_blk,),
        in_specs=[
            pl.BlockSpec((1, 1, blk), lambda i, be: (i, 0, 0), memory_space=pltpu.SMEM),
            pl.BlockSpec((1, 1, blk), lambda i, be: (jnp.minimum(i + 1, n_blk - 1), 0, 0),
                         memory_space=pltpu.SMEM),
            pl.BlockSpec((blk, 1), lambda i, be: (i, 0)),
            pl.BlockSpec(memory_space=pl.ANY),
            pl.BlockSpec((1, D, D_EXPERT), lambda i, be: (be[i], 0, 0)),
            pl.BlockSpec((1, D, D_EXPERT), lambda i, be: (be[i], 0, 0)),
            pl.BlockSpec((1, D_EXPERT, D), lambda i, be: (be[i], 0, 0)),
        ],
        out_specs=pl.BlockSpec((blk, D), lambda i, be: (i, 0)),
        scratch_shapes=[pltpu.VMEM((2, blk, D), F32), pltpu.SemaphoreType.DMA((2,))],
    )
    return pl.pallas_call(
        functools.partial(_moe_kernel, blk),
        out_shape=jax.ShapeDtypeStruct((P, D), F32),
        grid_spec=grid_spec,
        compiler_params=_cparams(("arbitrary",)),
        name="moe",
    )(blk_expert, buf_tok3, buf_tok3, buf_gate.reshape(P, 1), h2, wgate, wup, wdown)


def _comb_kernel(TM, final, d_ref, dn_ref, x1_ref, gt2_ref, gf_ref, ys_hbm, o_ref, buf, sem):
    i = pl.program_id(0)
    n = pl.num_programs(0)
    slot = lax.rem(i, 2)

    @pl.when(i == 0)
    def _():
        _row_gather(ys_hbm, d_ref, buf.at[0], sem.at[0], 2 * TM)

    @pl.when(i + 1 < n)
    def _():
        _row_gather(ys_hbm, dn_ref, buf.at[1 - slot], sem.at[1 - slot], 2 * TM)

    _row_gather_wait(ys_hbm, buf.at[slot], sem.at[slot], 2 * TM)
    x2 = x1_ref[...] + gt2_ref[0] * (buf[slot, 0:TM, :] + buf[slot, TM:2 * TM, :])
    if final:
        x2 = _rms(x2) * gf_ref[...]
    o_ref[...] = x2


def _combine(dest3, x1, gt2, gf, ys, S, tm, final):
    T, D = x1.shape
    n = T // tm
    per_b = S // tm
    return pl.pallas_call(
        functools.partial(_comb_kernel, tm, final),
        out_shape=jax.ShapeDtypeStruct((T, D), F32),
        grid=(n,),
        in_specs=[
            pl.BlockSpec((1, 1, 2 * tm), lambda i: (i, 0, 0), memory_space=pltpu.SMEM),
            pl.BlockSpec((1, 1, 2 * tm), lambda i: (jnp.minimum(i + 1, n - 1), 0, 0), memory_space=pltpu.SMEM),
            pl.BlockSpec((tm, D), lambda i: (i, 0)),
            pl.BlockSpec((1, 1, D), lambda i: (i // per_b, 0, 0)),
            pl.BlockSpec((1, D), lambda i: (0, 0)),
            pl.BlockSpec(memory_space=pl.ANY),
        ],
        out_specs=pl.BlockSpec((tm, D), lambda i: (i, 0)),
        scratch_shapes=[pltpu.VMEM((2, 2 * tm, D), F32), pltpu.SemaphoreType.DMA((2,))],
        compiler_params=_cparams(("arbitrary",)),
        name="combine",
    )(dest3, dest3, x1, gt2, gf, ys)


def _dispatch(route, blk):
    T = route.shape[0]
    N = T * TOP_K_IN_GROUP
    eid = route[:, 0:TOP_K_IN_GROUP].astype(I32).reshape(N)
    gate = route[:, TOP_K_IN_GROUP:2 * TOP_K_IN_GROUP].reshape(N)
    oh = (eid[:, None] == jnp.arange(N_EXPERTS, dtype=I32)[None, :]).astype(I32)
    counts = jnp.sum(oh, axis=0)
    rank = jnp.take_along_axis(jnp.cumsum(oh, axis=0) - oh, eid[:, None], axis=1)[:, 0]
    padded = (counts + blk - 1) // blk * blk
    pad_end = jnp.cumsum(padded)
    pad_start = pad_end - padded
    dest = pad_start[eid] + rank
    P = N + N_EXPERTS * blk
    n_blk = P // blk
    token_id = jnp.repeat(jnp.arange(T, dtype=I32), TOP_K_IN_GROUP)
    buf_tok = jnp.zeros((P,), I32).at[dest].set(token_id)
    buf_gate = jnp.zeros((P,), F32).at[dest].set(gate)
    blk_expert = jnp.minimum(jnp.searchsorted(pad_end, jnp.arange(n_blk, dtype=I32) * blk, side="right"),
                             N_EXPERTS - 1).astype(I32)
    return blk_expert, buf_tok.reshape(n_blk, 1, blk), buf_gate, dest.reshape(T, TOP_K_IN_GROUP)


def kernel(x, c, w_mod, b_mod, g_norm1, g_norm2, w_in, g_cq, g_ckv, g_kidx, w_q_up, w_idx_q, w_v_up,
           lb_logits, g_rec, w_branch_a, w_branch_r, w_out, w_grp, b_grp, w_exp_router, b_exp_router,
           w_gate, w_up, w_down, g_final):
    B, S, D = x.shape
    L = w_mod.shape[0]
    T = B * S
    tm = min(256, S)
    ts = min(256, S)
    tmc = min(128, S)
    blk = 256
    kw = N_REC_HEADS * REC_K_DIM
    na = Q_RANK + KV_RANK + IDX_DIM + IDX_HEADS
    n_rec = 2 * kw + 2 * REC_WIDTH

    mod = _modulation(c, w_mod, b_mod)
    eye = jnp.eye(N_ATTN_HEADS, dtype=F32)
    for l in range(L):
        m6 = mod[l].reshape(B, 6, 1, D)
        sh1, sc1, gt1, sh2, sc2, gt2 = (m6[:, j] for j in range(6))
        wa = jnp.pad(w_in[l, :, :na], ((0, 0), (0, 512 - na))).astype(BF16)
        wrec = w_in[l, :, na:na + n_rec].astype(BF16)
        wg = w_in[l, :, na + n_rec:].astype(BF16)
        gk = jnp.pad(g_kidx[l], (0, LANES - IDX_DIM)).reshape(1, LANES)
        (q4, qi4, kv, kidx2, wt, qs, kk, vv, lf, og, sga, sgr) = _proj(
            l, x, sh1, sc1, g_norm1[l].reshape(1, D), wa, wrec, wg, g_cq[l].reshape(1, -1),
            g_ckv[l].reshape(1, -1), gk, w_q_up[l].astype(BF16), w_idx_q[l].astype(BF16), lb_logits, tm)
        wbd = (eye[:, None, :, None] * w_v_up[l][:, :, None, :]).reshape(
            N_ATTN_HEADS * KV_RANK, ATTN_WIDTH).astype(BF16)
        ya = _dsa(q4, qi4, wt, kv, kidx2, wbd)
        yr = _hgrn(qs, kk, vv, lf, og, jnp.tile(g_rec[l], N_REC_HEADS).reshape(1, REC_WIDTH), ts)
        wr = jnp.pad(jnp.concatenate([w_grp[l], w_exp_router[l]], axis=1),
                     ((0, 0), (0, LANES - N_GROUPS - N_EXPERTS)))
        wrh, wrl = _hi_lo(wr)
        br = jnp.pad(jnp.concatenate([b_grp[l], b_exp_router[l]]), (0, LANES - N_GROUPS - N_EXPERTS)).reshape(1, LANES)
        x1, h2, route = _merge(x, ya, yr, sga, sgr, gt1, sh2, sc2, g_norm2[l].reshape(1, D),
                               w_branch_a[l].astype(BF16), w_branch_r[l].astype(BF16), w_out[l].astype(BF16),
                               wrh, wrl, br, tm)
        blk_expert, buf_tok3, buf_gate, dest = _dispatch(route.reshape(T, LANES), blk)
        ys = _moe(blk_expert, buf_tok3, buf_gate, h2.reshape(T, D), w_gate[l].astype(BF16),
                  w_up[l].astype(BF16), w_down[l].astype(BF16), blk)
        dest3 = dest.reshape(T // tmc, tmc, TOP_K_IN_GROUP).transpose(0, 2, 1).reshape(T // tmc, 1, 2 * tmc)
        x = _combine(dest3, x1.reshape(T, D), gt2, g_final.reshape(1, D), ys, S, tmc, l == L - 1).reshape(B, S, D)
    return x
```

```python
import functools

import jax
import jax.numpy as jnp
import numpy as np
from jax import lax
from jax.experimental import pallas as pl
from jax.experimental.pallas import tpu as pltpu

N_ATTN_HEADS = 8
Q_RANK = 256
KV_RANK = 128
ATTN_V_DIM = 64
ATTN_WIDTH = N_ATTN_HEADS * ATTN_V_DIM
ATTN_SCALE = KV_RANK ** -0.5
IDX_HEADS = 8
IDX_DIM = 64
IDX_W_SCALE = (IDX_HEADS * IDX_DIM) ** -0.5
TOPK_MAX = 256
N_REC_HEADS = 8
REC_K_DIM = 64
REC_V_DIM = 64
REC_WIDTH = N_REC_HEADS * REC_V_DIM
N_GROUPS = 4
EXPERTS_PER_GROUP = 8
N_EXPERTS = N_GROUPS * EXPERTS_PER_GROUP
TOP_K_IN_GROUP = 2
D_EXPERT = 512
EPS = 1e-6
NEG = -1e30
TINY = 1e-30

LANES = 128
SUBLANES = 8
Q_BLOCK = 128
SUB = 16
VMEM_LIMIT = 56 * 1024 * 1024

F32 = jnp.float32
BF16 = jnp.bfloat16
I32 = jnp.int32


def _cparams(sem):
    return pltpu.CompilerParams(dimension_semantics=sem, vmem_limit_bytes=VMEM_LIMIT)


def _nt_dot(a, b):
    return lax.dot_general(a, b, (((1,), (1,)), ((), ())), preferred_element_type=F32)


def _tn_dot(a, b):
    return lax.dot_general(a, b, (((0,), (0,)), ((), ())), preferred_element_type=F32)


def _dot(a, b):
    return jnp.dot(a, b, preferred_element_type=F32)


def _split_dot(a_f32, b_hi, b_lo):
    a_hi = a_f32.astype(BF16)
    a_lo = (a_f32 - a_hi.astype(F32)).astype(BF16)
    return _dot(a_hi, b_hi) + (_dot(a_hi, b_lo) + _dot(a_lo, b_hi))


def _hi_lo(w):
    hi = w.astype(BF16)
    lo = (w - hi.astype(F32)).astype(BF16)
    return hi, lo


def _mod_kernel(c_ref, w_ref, b_ref, o_ref):
    c = c_ref[...]
    ca = c * jax.nn.sigmoid(c)
    w = w_ref[0]
    w_hi = w.astype(BF16)
    w_lo = (w - w_hi.astype(F32)).astype(BF16)
    o_ref[0] = _split_dot(ca, w_hi, w_lo) + b_ref[0]


def _modulation(c, w_mod, b_mod):
    L, D, D6 = w_mod.shape
    B = c.shape[0]
    tn = 1024
    return pl.pallas_call(
        _mod_kernel,
        out_shape=jax.ShapeDtypeStruct((L, B, D6), F32),
        grid=(L, D6 // tn),
        in_specs=[
            pl.BlockSpec((B, D), lambda l, n: (0, 0)),
            pl.BlockSpec((1, D, tn), lambda l, n: (l, 0, n)),
            pl.BlockSpec((1, 1, tn), lambda l, n: (l, 0, n)),
        ],
        out_specs=pl.BlockSpec((1, B, tn), lambda l, n: (l, 0, n)),
        compiler_params=_cparams(("arbitrary", "arbitrary")),
        name="modulation",
    )(c, w_mod, b_mod.reshape(L, 1, D6))


def _rms(x, eps=EPS):
    return x * lax.rsqrt(jnp.mean(x * x, axis=-1, keepdims=True) + eps)


def _proj_kernel(layer, x_ref, sh_ref, sc_ref, g1_ref, wa_ref, wrec_ref, wg_ref, gcq_ref, gckv_ref,
                 gk_ref, wq_ref, wiq_ref, lb_ref,
                 q_out, qi_out, kv_out, kvt_out, kidx_out, wt_out, qs_out, k_out, v_out, lf_out, og_out,
                 sga_out, sgr_out):
    x = x_ref[0]
    tm = x.shape[0]
    h = (_rms(x) * g1_ref[...]) * (1.0 + sc_ref[0]) + sh_ref[0]
    hb = h.astype(BF16)

    ua = _dot(hb, wa_ref[...])
    cq = (_rms(ua[:, :Q_RANK]) * gcq_ref[...]).astype(BF16)
    q = (_dot(cq, wq_ref[...]) * ATTN_SCALE).astype(BF16)
    qi = _dot(cq, wiq_ref[...]).astype(BF16)
    for qb in range(tm // Q_BLOCK):
        rows = slice(qb * Q_BLOCK, (qb + 1) * Q_BLOCK)
        for hd in range(N_ATTN_HEADS):
            q_out[0, qb, hd * Q_BLOCK:(hd + 1) * Q_BLOCK, :] = q[rows, hd * LANES:(hd + 1) * LANES]
        for t in range(IDX_HEADS * IDX_DIM // LANES):
            qi_out[0, qb, t * Q_BLOCK:(t + 1) * Q_BLOCK, :] = qi[rows, t * LANES:(t + 1) * LANES]
    kvn = _rms(ua[:, Q_RANK:Q_RANK + KV_RANK]) * gckv_ref[...]
    kv_out[0] = kvn.astype(BF16)
    kvt_out[0] = kvn.T.astype(BF16)

    t3 = ua[:, Q_RANK + KV_RANK:]
    lane = lax.broadcasted_iota(I32, t3.shape, 1)
    ms = jnp.sum(jnp.where(lane < IDX_DIM, t3 * t3, 0.0), axis=-1, keepdims=True) * (1.0 / IDX_DIM)
    ka = t3 * lax.rsqrt(ms + EPS) * gk_ref[...]
    kb = pltpu.roll(ka, IDX_DIM, 1)
    kidx_out[0] = jnp.concatenate([ka, kb], axis=1).astype(BF16)
    wt_out[0] = (t3 * IDX_W_SCALE).T[IDX_DIM:IDX_DIM + IDX_HEADS, :]

    ur = _dot(hb, wrec_ref[...])
    kw = N_REC_HEADS * REC_K_DIM
    lbl = lb_ref[...]
    e = jnp.exp(lbl - jnp.max(lbl, axis=0, keepdims=True))
    p = e / jnp.sum(e, axis=0, keepdims=True)
    lb = jnp.zeros((1, kw), F32)
    for j in range(1, layer + 1):
        lb = lb + p[j:j + 1, :]
    lb = jnp.clip(lb, 0.0, 1.0)
    sig = jax.nn.sigmoid(ur[:, kw:2 * kw])
    f = lb + (1.0 - lb) * sig
    lf_out[0] = jnp.log(jnp.maximum(f, TINY))
    k_out[0] = ((1.0 - lb) * (1.0 - sig)).astype(BF16)
    qr = ur[:, :kw]
    qs_out[0] = (qr * jax.nn.sigmoid(qr)).astype(BF16)
    v_out[0] = ur[:, 2 * kw:2 * kw + REC_WIDTH].astype(BF16)
    og = ur[:, 2 * kw + REC_WIDTH:]
    og_out[0] = (og * jax.nn.sigmoid(og)).astype(BF16)

    ug = _dot(hb, wg_ref[...])
    D = x.shape[1]
    sga_out[0] = jax.nn.sigmoid(ug[:, :D]).astype(BF16)
    sgr_out[0] = jax.nn.sigmoid(ug[:, D:]).astype(BF16)


def _proj(layer, x, sh1, sc1, g1, wa, wrec, wg, gcq, gckv, gk, wq, wiq, lb_logits, tm):
    B, S, D = x.shape
    nqb = tm // Q_BLOCK
    nq = S // Q_BLOCK
    kw = N_REC_HEADS * REC_K_DIM
    c2 = lambda b, i: (0, 0)
    tok = lambda w: pl.BlockSpec((1, tm, w), lambda b, i: (b, i, 0))
    full = lambda a: pl.BlockSpec(a.shape, c2)
    out_shapes = (
        jax.ShapeDtypeStruct((B, nq, N_ATTN_HEADS * Q_BLOCK, KV_RANK), BF16),
        jax.ShapeDtypeStruct((B, nq, IDX_HEADS * IDX_DIM // LANES * Q_BLOCK, LANES), BF16),
        jax.ShapeDtypeStruct((B, S, KV_RANK), BF16),
        jax.ShapeDtypeStruct((B, KV_RANK, S), BF16),
        jax.ShapeDtypeStruct((B, S, 2 * LANES), BF16),
        jax.ShapeDtypeStruct((B, IDX_HEADS, S), F32),
        jax.ShapeDtypeStruct((B, S, kw), BF16),
        jax.ShapeDtypeStruct((B, S, kw), BF16),
        jax.ShapeDtypeStruct((B, S, REC_WIDTH), BF16),
        jax.ShapeDtypeStruct((B, S, kw), F32),
        jax.ShapeDtypeStruct((B, S, REC_WIDTH), BF16),
        jax.ShapeDtypeStruct((B, S, D), BF16),
        jax.ShapeDtypeStruct((B, S, D), BF16),
    )
    out_specs = (
        pl.BlockSpec((1, nqb, N_ATTN_HEADS * Q_BLOCK, KV_RANK), lambda b, i: (b, i, 0, 0)),
        pl.BlockSpec((1, nqb, IDX_HEADS * IDX_DIM // LANES * Q_BLOCK, LANES), lambda b, i: (b, i, 0, 0)),
        tok(KV_RANK),
        pl.BlockSpec((1, KV_RANK, tm), lambda b, i: (b, 0, i)),
        tok(2 * LANES),
        pl.BlockSpec((1, IDX_HEADS, tm), lambda b, i: (b, 0, i)),
        tok(kw), tok(kw), tok(REC_WIDTH), tok(kw), tok(REC_WIDTH),
        tok(D), tok(D),
    )
    vec = lambda: pl.BlockSpec((1, 1, D), lambda b, i: (b, 0, 0))
    return pl.pallas_call(
        functools.partial(_proj_kernel, layer),
        out_shape=out_shapes,
        grid=(B, S // tm),
        in_specs=[tok(D), vec(), vec(), full(g1), full(wa), full(wrec), full(wg), full(gcq), full(gckv),
                  full(gk), full(wq), full(wiq), full(lb_logits)],
        out_specs=out_specs,
        compiler_params=_cparams(("arbitrary", "arbitrary")),
        name="proj",
    )(x, sh1, sc1, g1, wa, wrec, wg, gcq, gckv, gk, wq, wiq, lb_logits)


def _sort_key(score):
    bits = lax.bitcast_convert_type(score, I32)
    return bits ^ (lax.shift_right_arithmetic(bits, 31) & jnp.int32(0x7FFFFFFF))


def _dsa_kernel(S, KC, TOPK, q_ref, qi_ref, wt_ref, kv_ref, kvt_ref, kidx_ref, wvt_ref, ya_ref,
                keys_ref, acc_ref, m_ref, l_ref, a_ref, p_ref):
    i = pl.program_id(1)
    n_ch = ((i + 1) * Q_BLOCK + KC - 1) // KC
    qpos = i * Q_BLOCK + lax.broadcasted_iota(I32, (1, Q_BLOCK), 1)
    wt = wt_ref[0]
    wrow = [wt[hd:hd + 1, :] for hd in range(IDX_HEADS)]
    qi = qi_ref[0, 0]
    n_t = IDX_HEADS * IDX_DIM // LANES

    def kpos_of(c):
        return c * KC + lax.broadcasted_iota(I32, (KC, 1), 0)

    def score_body(c, carry):
        koff = pl.multiple_of(c * KC, KC)
        la = _nt_dot(kidx_ref[0, pl.ds(koff, KC), 0:LANES], qi)
        lb = _nt_dot(kidx_ref[0, pl.ds(koff, KC), LANES:2 * LANES], qi)
        score = jnp.zeros((KC, Q_BLOCK), F32)
        for t in range(n_t):
            lanes = slice(t * Q_BLOCK, (t + 1) * Q_BLOCK)
            score = score + wrow[2 * t] * jnp.maximum(la[:, lanes], 0.0)
            score = score + wrow[2 * t + 1] * jnp.maximum(lb[:, lanes], 0.0)
        score = jnp.where(kpos_of(c) <= qpos, score, NEG)
        keys_ref[pl.ds(koff, KC), :] = _sort_key(score)
        return carry

    lax.fori_loop(0, n_ch, score_body, 0)

    def count(pred):
        def body(c, acc):
            kk = keys_ref[pl.ds(pl.multiple_of(c * KC, KC), KC), :]
            ones = jnp.where(pred(kk), 1, 0).astype(I32)
            return acc + jnp.sum(ones.reshape(KC // SUBLANES, SUBLANES, Q_BLOCK), axis=0)
        acc = lax.fori_loop(0, n_ch, body, jnp.zeros((SUBLANES, Q_BLOCK), I32))
        return jnp.sum(acc, axis=0, keepdims=True)

    def bis_body(it, v):
        cand = v + lax.shift_left(jnp.int32(1), 31 - it)
        cnt = count(lambda kk: kk >= cand)
        return jnp.where(cnt >= TOPK, cand, v)

    v = lax.fori_loop(0, 32, bis_body, jnp.full((1, Q_BLOCK), -2 ** 31, I32))
    n_gt = count(lambda kk: kk > v)
    n_ge = count(lambda kk: kk >= v)
    need = (TOPK - n_gt).astype(F32)
    any_split = jnp.max(jnp.where(n_ge > TOPK, 1, 0)) > 0

    @pl.when(any_split)
    def _():
        tri = jnp.where(lax.broadcasted_iota(I32, (KC, KC), 1) <= lax.broadcasted_iota(I32, (KC, KC), 0),
                        1.0, 0.0).astype(BF16)

        def tie_body(c, seen):
            koff = pl.multiple_of(c * KC, KC)
            kk = keys_ref[pl.ds(koff, KC), :]
            tie = kk == v
            tie_f = jnp.where(tie, 1.0, 0.0)
            rank = _dot(tri, tie_f.astype(BF16)) + seen
            keys_ref[pl.ds(koff, KC), :] = jnp.where(tie & (rank > need), v - 1, kk)
            return seen + jnp.sum(tie_f, axis=0, keepdims=True)

        lax.fori_loop(0, n_ch, tie_body, jnp.zeros((1, Q_BLOCK), F32))

    m_ref[...] = jnp.full(m_ref.shape, NEG, F32)
    l_ref[...] = jnp.zeros(l_ref.shape, F32)
    acc_ref[...] = jnp.zeros(acc_ref.shape, F32)

    def att_body(c, carry):
        koff = pl.multiple_of(c * KC, KC)
        kvc = kv_ref[0, pl.ds(koff, KC), :]
        msk = (keys_ref[pl.ds(koff, KC), :] >= v) & (kpos_of(c) <= qpos)
        for hd in range(N_ATTN_HEADS):
            lanes = slice(hd * Q_BLOCK, (hd + 1) * Q_BLOCK)
            s = jnp.where(msk, _nt_dot(kvc, q_ref[0, 0, hd * Q_BLOCK:(hd + 1) * Q_BLOCK, :]), -jnp.inf)
            m_prev = m_ref[:, lanes]
            m_next = jnp.maximum(m_prev, jnp.max(s, axis=0, keepdims=True))
            alpha = jnp.exp(m_prev - m_next)
            p = jnp.exp(s - m_next)
            l_ref[:, lanes] = alpha * l_ref[:, lanes] + jnp.sum(p, axis=0, keepdims=True)
            m_ref[:, lanes] = m_next
            a_ref[:, lanes] = alpha
            p_ref[:, lanes] = p.astype(BF16)
        acc_ref[...] = acc_ref[...] * a_ref[...] + _dot(kvt_ref[0, :, pl.ds(koff, KC)], p_ref[...])
        return carry

    lax.fori_loop(0, n_ch, att_body, 0)
    o = (acc_ref[...] / l_ref[...]).astype(BF16)
    ya_t = jnp.concatenate([_dot(wvt_ref[hd], o[:, hd * Q_BLOCK:(hd + 1) * Q_BLOCK])
                            for hd in range(N_ATTN_HEADS)], axis=0)
    ya_ref[0] = ya_t.T.astype(BF16)


def _dsa(q4, qi4, wt, kv, kvt, kidx2, wvt):
    B, nq = q4.shape[0], q4.shape[1]
    S = kv.shape[1]
    KC = min(512, S)
    topk = min(TOPK_MAX, S // 4)
    assert S % KC == 0 and KC % Q_BLOCK == 0
    hq = N_ATTN_HEADS * Q_BLOCK
    return pl.pallas_call(
        functools.partial(_dsa_kernel, S, KC, topk),
        out_shape=jax.ShapeDtypeStruct((B, S, ATTN_WIDTH), BF16),
        grid=(B, nq),
        in_specs=[
            pl.BlockSpec((1, 1) + q4.shape[2:], lambda b, i: (b, i, 0, 0)),
            pl.BlockSpec((1, 1) + qi4.shape[2:], lambda b, i: (b, i, 0, 0)),
            pl.BlockSpec((1, IDX_HEADS, Q_BLOCK), lambda b, i: (b, 0, i)),
            pl.BlockSpec((1, S, KV_RANK), lambda b, i: (b, 0, 0)),
            pl.BlockSpec((1, KV_RANK, S), lambda b, i: (b, 0, 0)),
            pl.BlockSpec((1, S, 2 * LANES), lambda b, i: (b, 0, 0)),
            pl.BlockSpec(wvt.shape, lambda b, i: (0, 0, 0)),
        ],
        out_specs=pl.BlockSpec((1, Q_BLOCK, ATTN_WIDTH), lambda b, i: (b, i, 0)),
        scratch_shapes=[
            pltpu.VMEM((S, Q_BLOCK), I32),
            pltpu.VMEM((KV_RANK, hq), F32),
            pltpu.VMEM((1, hq), F32),
            pltpu.VMEM((1, hq), F32),
            pltpu.VMEM((1, hq), F32),
            pltpu.VMEM((KC, hq), BF16),
        ],
        compiler_params=_cparams(("arbitrary", "arbitrary")),
        name="dsa",
    )(q4, qi4, wt, kv, kvt, kidx2, wvt)


def _hgrn_kernel(TS, qs_ref, k_ref, v_ref, lf_ref, og_ref, grec_ref, y_ref,
                 st_ref, aloc_ref, e_ref, qh_ref, kh_ref, o_ref):
    n_tiles = REC_WIDTH // LANES

    @pl.when(pl.program_id(1) == 0)
    def _():
        st_ref[...] = jnp.zeros(st_ref.shape, F32)

    a = lf_ref[0]
    r = lax.broadcasted_iota(I32, a.shape, 0) % SUB
    for sh in (1, 2, 4, 8):
        a = a + jnp.where(r >= sh, pltpu.roll(a, sh, 0), 0.0)
    aend = jnp.where(r == SUB - 1, a, 0.0)
    for sh in (1, 2, 4, 8):
        aend = aend + jnp.where(r + sh <= SUB - 1, pltpu.roll(aend, TS - sh, 0), 0.0)
    e = jnp.exp(a)
    aloc_ref[...] = a
    e_ref[...] = e
    qh_ref[...] = (qs_ref[0].astype(F32) * e).astype(BF16)
    kh_ref[...] = (k_ref[0].astype(F32) * jnp.exp(aend - a)).astype(BF16)

    half = REC_K_DIM
    li = lax.broadcasted_iota(I32, (LANES, LANES), 0) // half
    lj = lax.broadcasted_iota(I32, (LANES, LANES), 1) // half
    bd_f = jnp.where(li == lj, 1.0, 0.0)
    bd_b = bd_f.astype(BF16)
    gsel = jnp.where(lax.broadcasted_iota(I32, (SUB, SUB * SUB), 0)
                     == lax.broadcasted_iota(I32, (SUB, SUB * SUB), 1) // SUB, 1.0, 0.0).astype(BF16)
    srow = lax.broadcasted_iota(I32, (SUB, LANES), 0)

    def sub_body(j, carry):
        r0 = pl.multiple_of(j * SUB, SUB)
        rows = pl.ds(r0, SUB)
        for p in range(n_tiles):
            lanes = slice(p * LANES, (p + 1) * LANES)
            st = st_ref[p]
            o_inter = _nt_dot(qh_ref[rows, lanes], st.astype(BF16))
            a_blk = aloc_ref[rows, lanes]
            q_blk = qs_ref[0, rows, lanes].astype(F32)
            k_blk = k_ref[0, rows, lanes].astype(F32)
            v_blk = v_ref[0, rows, lanes]
            ws = []
            for t in range(SUB):
                d = a_blk[t:t + 1, :] - a_blk
                w = jnp.exp(jnp.where(srow <= t, d, -jnp.inf)) * (k_blk * q_blk[t:t + 1, :])
                ws.append(w.astype(BF16))
            wst = jnp.concatenate(ws, axis=0)
            rexp = _dot(wst, bd_b)
            v_f = v_blk.astype(F32)
            m2 = (rexp * jnp.concatenate([v_f] * SUB, axis=0)).astype(BF16)
            o_ref[rows, lanes] = o_inter + _dot(gsel, m2)
            ktv = _tn_dot(v_blk, kh_ref[rows, lanes])
            dec = e_ref[rows, lanes][SUB - 1:SUB, :]
            st_ref[p] = st * dec + ktv * bd_f
        return carry

    lax.fori_loop(0, TS // SUB, sub_body, 0)

    o = o_ref[...]
    o2 = o * o
    hi = o2.astype(BF16)
    lo = (o2 - hi.astype(F32)).astype(BF16)
    ms = jnp.concatenate(
        [_dot(hi[:, p * LANES:(p + 1) * LANES], bd_b) + _dot(lo[:, p * LANES:(p + 1) * LANES], bd_b)
         for p in range(n_tiles)], axis=1) * (1.0 / REC_V_DIM)
    y = o * lax.rsqrt(ms + EPS) * grec_ref[...]
    y_ref[0] = (y * og_ref[0].astype(F32)).astype(BF16)


def _hgrn(qs, k, v, lf, og, grec_t, ts):
    B, S, W = qs.shape
    tok = lambda: pl.BlockSpec((1, ts, W), lambda b, i: (b, i, 0))
    return pl.pallas_call(
        functools.partial(_hgrn_kernel, ts),
        out_shape=jax.ShapeDtypeStruct((B, S, W), BF16),
        grid=(B, S // ts),
        in_specs=[tok(), tok(), tok(), tok(), tok(), pl.BlockSpec((1, W), lambda b, i: (0, 0))],
        out_specs=tok(),
        scratch_shapes=[
            pltpu.VMEM((W // LANES, LANES, LANES), F32),
            pltpu.VMEM((ts, W), F32),
            pltpu.VMEM((ts, W), F32),
            pltpu.VMEM((ts, W), BF16),
            pltpu.VMEM((ts, W), BF16),
            pltpu.VMEM((ts, W), F32),
        ],
        compiler_params=_cparams(("arbitrary", "arbitrary")),
        name="hgrn",
    )(qs, k, v, lf, og, grec_t)


def _merge_kernel(x_ref, ya_ref, yr_ref, sga_ref, sgr_ref, gt1_ref, sh2_ref, sc2_ref, g2_ref,
                  wba_ref, wbr_ref, wo_ref, wrh_ref, wrl_ref, br_ref, x1_out, h2_out, rt_out):
    merged = (sga_ref[0].astype(F32) * _dot(ya_ref[0], wba_ref[...])
              + sgr_ref[0].astype(F32) * _dot(yr_ref[0], wbr_ref[...]))
    x1 = x_ref[0] + gt1_ref[0] * _dot(merged.astype(BF16), wo_ref[...])
    x1_out[0] = x1
    h2 = (_rms(x1) * g2_ref[...]) * (1.0 + sc2_ref[0]) + sh2_ref[0]
    h2_out[0] = h2

    lg = _split_dot(h2, wrh_ref[...], wrl_ref[...]) + br_ref[...]
    lane = lax.broadcasted_iota(I32, lg.shape, 1)
    big = jnp.int32(1 << 20)
    gl = jnp.where(lane < N_GROUPS, lg, -jnp.inf)
    gmax = jnp.max(gl, axis=1, keepdims=True)
    gsel = jnp.min(jnp.where(gl == gmax, lane, big), axis=1, keepdims=True)
    ggate = 1.0 / jnp.sum(jnp.exp(gl - gmax), axis=1, keepdims=True)
    lo = N_GROUPS + EXPERTS_PER_GROUP * gsel
    el = jnp.where((lane >= lo) & (lane < lo + EXPERTS_PER_GROUP), lg, -jnp.inf)
    v1 = jnp.max(el, axis=1, keepdims=True)
    i1 = jnp.min(jnp.where(el == v1, lane, big), axis=1, keepdims=True)
    el2 = jnp.where(lane == i1, -jnp.inf, el)
    v2 = jnp.max(el2, axis=1, keepdims=True)
    i2 = jnp.min(jnp.where(el2 == v2, lane, big), axis=1, keepdims=True)
    e2 = jnp.exp(v2 - v1)
    den = 1.0 + e2
    g1 = (1.0 / den) * ggate
    g2 = (e2 / den) * ggate
    rt = jnp.where(lane == 0, (i1 - N_GROUPS).astype(F32),
                   jnp.where(lane == 1, (i2 - N_GROUPS).astype(F32),
                             jnp.where(lane == 2, g1, jnp.where(lane == 3, g2, 0.0))))
    rt_out[0] = rt


def _merge(x, ya, yr, sga, sgr, gt1, sh2, sc2, g2, wba, wbr, wo, wrh, wrl, br, tm):
    B, S, D = x.shape
    tok = lambda w: pl.BlockSpec((1, tm, w), lambda b, i: (b, i, 0))
    vec = lambda: pl.BlockSpec((1, 1, D), lambda b, i: (b, 0, 0))
    full = lambda a: pl.BlockSpec(a.shape, lambda b, i: (0,) * a.ndim)
    return pl.pallas_call(
        _merge_kernel,
        out_shape=(jax.ShapeDtypeStruct((B, S, D), F32), jax.ShapeDtypeStruct((B, S, D), F32),
                   jax.ShapeDtypeStruct((B, S, LANES), F32)),
        grid=(B, S // tm),
        in_specs=[tok(D), tok(ATTN_WIDTH), tok(REC_WIDTH), tok(D), tok(D), vec(), vec(), vec(), full(g2),
                  full(wba), full(wbr), full(wo), full(wrh), full(wrl), full(br)],
        out_specs=(tok(D), tok(D), tok(LANES)),
        compiler_params=_cparams(("arbitrary", "arbitrary")),
        name="merge",
    )(x, ya, yr, sga, sgr, gt1, sh2, sc2, g2, wba, wbr, wo, wrh, wrl, br)


def _row_gather(src_hbm, idx_ref, dst, sem, n):
    def body(r, carry):
        pltpu.make_async_copy(src_hbm.at[pl.ds(idx_ref[0, 0, r], 1)], dst.at[pl.ds(r, 1)], sem).start()
        return carry
    lax.fori_loop(0, n, body, 0)


def _row_gather_wait(src_hbm, dst, sem, n):
    pltpu.make_async_copy(src_hbm.at[pl.ds(0, n)], dst, sem).wait()


def _moe_kernel(BLK, be_ref, tok_ref, tokn_ref, gate_ref, h2_hbm, wg_ref, wu_ref, wd_ref, y_ref,
                xbuf, sem):
    i = pl.program_id(0)
    n = pl.num_programs(0)
    slot = lax.rem(i, 2)

    @pl.when(i == 0)
    def _():
        _row_gather(h2_hbm, tok_ref, xbuf.at[0], sem.at[0], BLK)

    @pl.when(i + 1 < n)
    def _():
        _row_gather(h2_hbm, tokn_ref, xbuf.at[1 - slot], sem.at[1 - slot], BLK)

    _row_gather_wait(h2_hbm, xbuf.at[slot], sem.at[slot], BLK)
    xb = xbuf[slot].astype(BF16)
    g = _dot(xb, wg_ref[0])
    hid = (g * jax.nn.sigmoid(g)) * _dot(xb, wu_ref[0])
    y_ref[...] = _dot(hid.astype(BF16), wd_ref[0]) * gate_ref[...]


def _moe(blk_expert, buf_tok3, buf_gate, h2, wgate, wup, wdown, blk):
    P = buf_gate.shape[0]
    n_blk = P // blk
    D = h2.shape[1]
    grid_spec = pltpu.PrefetchScalarGridSpec(
        num_scalar_prefetch=1,
        grid=(n_blk,),
        in_specs=[
            pl.BlockSpec((1, 1, blk), lambda i, be: (i, 0, 0), memory_space=pltpu.SMEM),
            pl.BlockSpec((1, 1, blk), lambda i, be: (jnp.minimum(i + 1, n_blk - 1), 0, 0),
                         memory_space=pltpu.SMEM),
            pl.BlockSpec((blk, 1), lambda i, be: (i, 0)),
            pl.BlockSpec(memory_space=pl.ANY),
            pl.BlockSpec((1, D, D_EXPERT), lambda i, be: (be[i], 0, 0)),
            pl.BlockSpec((1, D, D_EXPERT), lambda i, be: (be[i], 0, 0)),
            pl.BlockSpec((1, D_EXPERT, D), lambda i, be: (be[i], 0, 0)),
        ],
        out_specs=pl.BlockSpec((blk, D), lambda i, be: (i, 0)),
        scratch_shapes=[pltpu.VMEM((2, blk, D), F32), pltpu.SemaphoreType.DMA((2,))],
    )
    return pl.pallas_call(
        functools.partial(_moe_kernel, blk),
        out_shape=jax.ShapeDtypeStruct((P, D), F32),
        grid_spec=grid_spec,
        compiler_params=_cparams(("arbitrary",)),
        name="moe",
    )(blk_expert, buf_tok3, buf_tok3, buf_gate.reshape(P, 1), h2, wgate, wup, wdown)


def _comb_kernel(TM, final, d_ref, dn_ref, x1_ref, gt2_ref, gf_ref, ys_hbm, o_ref, buf, sem):
    i = pl.program_id(0)
    n = pl.num_programs(0)
    slot = lax.rem(i, 2)

    @pl.when(i == 0)
    def _():
        _row_gather(ys_hbm, d_ref, buf.at[0], sem.at[0], 2 * TM)

    @pl.when(i + 1 < n)
    def _():
        _row_gather(ys_hbm, dn_ref, buf.at[1 - slot], sem.at[1 - slot], 2 * TM)

    _row_gather_wait(ys_hbm, buf.at[slot], sem.at[slot], 2 * TM)
    x2 = x1_ref[...] + gt2_ref[0] * (buf[slot, 0:TM, :] + buf[slot, TM:2 * TM, :])
    if final:
        x2 = _rms(x2) * gf_ref[...]
    o_ref[...] = x2


def _combine(dest3, x1, gt2, gf, ys, S, tm, final):
    T, D = x1.shape
    n = T // tm
    per_b = S // tm
    return pl.pallas_call(
        functools.partial(_comb_kernel, tm, final),
        out_shape=jax.ShapeDtypeStruct((T, D), F32),
        grid=(n,),
        in_specs=[
            pl.BlockSpec((1, 1, 2 * tm), lambda i: (i, 0, 0), memory_space=pltpu.SMEM),
            pl.BlockSpec((1, 1, 2 * tm), lambda i: (jnp.minimum(i + 1, n - 1), 0, 0), memory_space=pltpu.SMEM),
            pl.BlockSpec((tm, D), lambda i: (i, 0)),
            pl.BlockSpec((1, 1, D), lambda i: (i // per_b, 0, 0)),
            pl.BlockSpec((1, D), lambda i: (0, 0)),
            pl.BlockSpec(memory_space=pl.ANY),
        ],
        out_specs=pl.BlockSpec((tm, D), lambda i: (i, 0)),
        scratch_shapes=[pltpu.VMEM((2, 2 * tm, D), F32), pltpu.SemaphoreType.DMA((2,))],
        compiler_params=_cparams(("arbitrary",)),
        name="combine",
    )(dest3, dest3, x1, gt2, gf, ys)


def _dispatch(route, blk):
    T = route.shape[0]
    N = T * TOP_K_IN_GROUP
    eid = route[:, 0:TOP_K_IN_GROUP].astype(I32).reshape(N)
    gate = route[:, TOP_K_IN_GROUP:2 * TOP_K_IN_GROUP].reshape(N)
    oh = (eid[:, None] == jnp.arange(N_EXPERTS, dtype=I32)[None, :]).astype(I32)
    counts = jnp.sum(oh, axis=0)
    rank = jnp.take_along_axis(jnp.cumsum(oh, axis=0) - oh, eid[:, None], axis=1)[:, 0]
    padded = (counts + blk - 1) // blk * blk
    pad_end = jnp.cumsum(padded)
    pad_start = pad_end - padded
    dest = pad_start[eid] + rank
    P = N + N_EXPERTS * blk
    n_blk = P // blk
    token_id = jnp.repeat(jnp.arange(T, dtype=I32), TOP_K_IN_GROUP)
    buf_tok = jnp.zeros((P,), I32).at[dest].set(token_id)
    buf_gate = jnp.zeros((P,), F32).at[dest].set(gate)
    blk_expert = jnp.minimum(jnp.searchsorted(pad_end, jnp.arange(n_blk, dtype=I32) * blk, side="right"),
                             N_EXPERTS - 1).astype(I32)
    return blk_expert, buf_tok.reshape(n_blk, 1, blk), buf_gate, dest.reshape(T, TOP_K_IN_GROUP)


def kernel(x, c, w_mod, b_mod, g_norm1, g_norm2, w_in, g_cq, g_ckv, g_kidx, w_q_up, w_idx_q, w_v_up,
           lb_logits, g_rec, w_branch_a, w_branch_r, w_out, w_grp, b_grp, w_exp_router, b_exp_router,
           w_gate, w_up, w_down, g_final):
    B, S, D = x.shape
    L = w_mod.shape[0]
    T = B * S
    tm = min(256, S)
    ts = min(256, S)
    tmc = min(128, S)
    blk = 256
    kw = N_REC_HEADS * REC_K_DIM
    na = Q_RANK + KV_RANK + IDX_DIM + IDX_HEADS
    n_rec = 2 * kw + 2 * REC_WIDTH

    mod = _modulation(c, w_mod, b_mod)
    for l in range(L):
        m6 = mod[l].reshape(B, 6, 1, D)
        sh1, sc1, gt1, sh2, sc2, gt2 = (m6[:, j] for j in range(6))
        wa = jnp.pad(w_in[l, :, :na], ((0, 0), (0, 512 - na))).astype(BF16)
        wrec = w_in[l, :, na:na + n_rec].astype(BF16)
        wg = w_in[l, :, na + n_rec:].astype(BF16)
        gk = jnp.pad(g_kidx[l], (0, LANES - IDX_DIM)).reshape(1, LANES)
        (q4, qi4, kv, kvt, kidx2, wt, qs, kk, vv, lf, og, sga, sgr) = _proj(
            l, x, sh1, sc1, g_norm1[l].reshape(1, D), wa, wrec, wg, g_cq[l].reshape(1, -1),
            g_ckv[l].reshape(1, -1), gk, w_q_up[l].astype(BF16), w_idx_q[l].astype(BF16), lb_logits, tm)
        wvt = jnp.swapaxes(w_v_up[l], 1, 2).astype(BF16)
        ya = _dsa(q4, qi4, wt, kv, kvt, kidx2, wvt)
        yr = _hgrn(qs, kk, vv, lf, og, jnp.tile(g_rec[l], N_REC_HEADS).reshape(1, REC_WIDTH), ts)
        wr = jnp.pad(jnp.concatenate([w_grp[l], w_exp_router[l]], axis=1),
                     ((0, 0), (0, LANES - N_GROUPS - N_EXPERTS)))
        wrh, wrl = _hi_lo(wr)
        br = jnp.pad(jnp.concatenate([b_grp[l], b_exp_router[l]]), (0, LANES - N_GROUPS - N_EXPERTS)).reshape(1, LANES)
        x1, h2, route = _merge(x, ya, yr, sga, sgr, gt1, sh2, sc2, g_norm2[l].reshape(1, D),
                               w_branch_a[l].astype(BF16), w_branch_r[l].astype(BF16), w_out[l].astype(BF16),
                               wrh, wrl, br, tm)
        blk_expert, buf_tok3, buf_gate, dest = _dispatch(route.reshape(T, LANES), blk)
        ys = _moe(blk_expert, buf_tok3, buf_gate, h2.reshape(T, D), w_gate[l].astype(BF16),
                  w_up[l].astype(BF16), w_down[l].astype(BF16), blk)
        dest3 = dest.reshape(T // tmc, tmc, TOP_K_IN_GROUP).transpose(0, 2, 1).reshape(T // tmc, 1, 2 * tmc)
        x = _combine(dest3, x1.reshape(T, D), gt2, g_final.reshape(1, D), ys, S, tmc, l == L - 1).reshape(B, S, D)
    return x
```

```python
import functools

import jax
import jax.numpy as jnp
import numpy as np
from jax import lax
from jax.experimental import pallas as pl
from jax.experimental.pallas import tpu as pltpu

N_ATTN_HEADS = 8
Q_RANK = 256
KV_RANK = 128
ATTN_V_DIM = 64
ATTN_WIDTH = N_ATTN_HEADS * ATTN_V_DIM
ATTN_SCALE = KV_RANK ** -0.5
IDX_HEADS = 8
IDX_DIM = 64
IDX_W_SCALE = (IDX_HEADS * IDX_DIM) ** -0.5
TOPK_MAX = 256
N_REC_HEADS = 8
REC_K_DIM = 64
REC_V_DIM = 64
REC_WIDTH = N_REC_HEADS * REC_V_DIM
N_GROUPS = 4
EXPERTS_PER_GROUP = 8
N_EXPERTS = N_GROUPS * EXPERTS_PER_GROUP
TOP_K_IN_GROUP = 2
D_EXPERT = 512
EPS = 1e-6
NEG = -1e30
TINY = 1e-30

LANES = 128
SUBLANES = 8
Q_BLOCK = 128
ONES_ROWS = 16
LOG2E = 1.4426950408889634
SUB = 16
DIAG_GROUP = 2
VMEM_LIMIT = 56 * 1024 * 1024

F32 = jnp.float32
BF16 = jnp.bfloat16
I32 = jnp.int32


def _cparams(sem):
    return pltpu.CompilerParams(dimension_semantics=sem, vmem_limit_bytes=VMEM_LIMIT)


def _nt_dot(a, b):
    return lax.dot_general(a, b, (((1,), (1,)), ((), ())), preferred_element_type=F32)


def _tn_dot(a, b):
    return lax.dot_general(a, b, (((0,), (0,)), ((), ())), preferred_element_type=F32)


def _dot(a, b):
    return jnp.dot(a, b, preferred_element_type=F32)


def _split_dot(a_f32, b_hi, b_lo):
    a_hi = a_f32.astype(BF16)
    a_lo = (a_f32 - a_hi.astype(F32)).astype(BF16)
    return _dot(a_hi, b_hi) + (_dot(a_hi, b_lo) + _dot(a_lo, b_hi))


def _hi_lo(w):
    hi = w.astype(BF16)
    lo = (w - hi.astype(F32)).astype(BF16)
    return hi, lo


def _mod_kernel(c_ref, w_ref, b_ref, o_ref):
    c = c_ref[...]
    ca = c * jax.nn.sigmoid(c)
    w = w_ref[0]
    w_hi = w.astype(BF16)
    w_lo = (w - w_hi.astype(F32)).astype(BF16)
    o_ref[0] = _split_dot(ca, w_hi, w_lo) + b_ref[0]


def _modulation(c, w_mod, b_mod):
    L, D, D6 = w_mod.shape
    B = c.shape[0]
    tn = 1024
    return pl.pallas_call(
        _mod_kernel,
        out_shape=jax.ShapeDtypeStruct((L, B, D6), F32),
        grid=(L, D6 // tn),
        in_specs=[
            pl.BlockSpec((B, D), lambda l, n: (0, 0)),
            pl.BlockSpec((1, D, tn), lambda l, n: (l, 0, n)),
            pl.BlockSpec((1, 1, tn), lambda l, n: (l, 0, n)),
        ],
        out_specs=pl.BlockSpec((1, B, tn), lambda l, n: (l, 0, n)),
        compiler_params=_cparams(("arbitrary", "arbitrary")),
        name="modulation",
    )(c, w_mod, b_mod.reshape(L, 1, D6))


def _rms(x, eps=EPS):
    return x * lax.rsqrt(jnp.mean(x * x, axis=-1, keepdims=True) + eps)


def _proj_kernel(layer, x_ref, sh_ref, sc_ref, g1_ref, wa_ref, wrec_ref, wg_ref, gcq_ref, gckv_ref,
                 gk_ref, wq_ref, wiq_ref, lb_ref,
                 q_out, qi_out, kv_out, kvt_out, kidx_out, wt_out, qs_out, k_out, v_out, vt_out, lf_out, og_out,
                 sga_out, sgr_out):
    x = x_ref[0]
    tm = x.shape[0]
    h = (_rms(x) * g1_ref[...]) * (1.0 + sc_ref[0]) + sh_ref[0]
    hb = h.astype(BF16)

    ua = _dot(hb, wa_ref[...])
    cq = (_rms(ua[:, :Q_RANK]) * gcq_ref[...]).astype(BF16)
    q = (_dot(cq, wq_ref[...]) * (ATTN_SCALE * LOG2E)).astype(BF16)
    qi = _dot(cq, wiq_ref[...]).astype(BF16)
    for qb in range(tm // Q_BLOCK):
        rows = slice(qb * Q_BLOCK, (qb + 1) * Q_BLOCK)
        for hd in range(N_ATTN_HEADS):
            q_out[0, qb, hd * Q_BLOCK:(hd + 1) * Q_BLOCK, :] = q[rows, hd * LANES:(hd + 1) * LANES]
        for t in range(IDX_HEADS * IDX_DIM // LANES):
            qi_out[0, qb, t * Q_BLOCK:(t + 1) * Q_BLOCK, :] = qi[rows, t * LANES:(t + 1) * LANES]
    kvn = _rms(ua[:, Q_RANK:Q_RANK + KV_RANK]) * gckv_ref[...]
    kv_out[0] = kvn.astype(BF16)
    kvt_out[0] = kvn.T.astype(BF16)

    t3 = ua[:, Q_RANK + KV_RANK:]
    lane = lax.broadcasted_iota(I32, t3.shape, 1)
    ms = jnp.sum(jnp.where(lane < IDX_DIM, t3 * t3, 0.0), axis=-1, keepdims=True) * (1.0 / IDX_DIM)
    ka = t3 * lax.rsqrt(ms + EPS) * gk_ref[...]
    kb = pltpu.roll(ka, IDX_DIM, 1)
    kidx_out[0] = jnp.concatenate([ka, kb], axis=1).astype(BF16)
    wt_out[0] = (t3 * IDX_W_SCALE).T[IDX_DIM:IDX_DIM + IDX_HEADS, :]

    ur = _dot(hb, wrec_ref[...])
    kw = N_REC_HEADS * REC_K_DIM
    lbl = lb_ref[...]
    e = jnp.exp(lbl - jnp.max(lbl, axis=0, keepdims=True))
    p = e / jnp.sum(e, axis=0, keepdims=True)
    lb = jnp.zeros((1, kw), F32)
    for j in range(1, layer + 1):
        lb = lb + p[j:j + 1, :]
    lb = jnp.clip(lb, 0.0, 1.0)
    sig = jax.nn.sigmoid(ur[:, kw:2 * kw])
    f = lb + (1.0 - lb) * sig
    lf_out[0] = jnp.log(jnp.maximum(f, TINY))
    k_out[0] = ((1.0 - lb) * (1.0 - sig)).astype(BF16)
    qr = ur[:, :kw]
    qs_out[0] = (qr * jax.nn.sigmoid(qr)).astype(BF16)
    vr = ur[:, 2 * kw:2 * kw + REC_WIDTH]
    v_out[0] = vr.astype(BF16)
    vt_out[0] = vr.T.astype(BF16)
    og = ur[:, 2 * kw + REC_WIDTH:]
    og_out[0] = (og * jax.nn.sigmoid(og)).astype(BF16)

    ug = _dot(hb, wg_ref[...])
    D = x.shape[1]
    sga_out[0] = jax.nn.sigmoid(ug[:, :D]).astype(BF16)
    sgr_out[0] = jax.nn.sigmoid(ug[:, D:]).astype(BF16)


def _proj(layer, x, sh1, sc1, g1, wa, wrec, wg, gcq, gckv, gk, wq, wiq, lb_logits, tm):
    B, S, D = x.shape
    nqb = tm // Q_BLOCK
    nq = S // Q_BLOCK
    kw = N_REC_HEADS * REC_K_DIM
    c2 = lambda b, i: (0, 0)
    tok = lambda w: pl.BlockSpec((1, tm, w), lambda b, i: (b, i, 0))
    full = lambda a: pl.BlockSpec(a.shape, c2)
    out_shapes = (
        jax.ShapeDtypeStruct((B, nq, N_ATTN_HEADS * Q_BLOCK, KV_RANK), BF16),
        jax.ShapeDtypeStruct((B, nq, IDX_HEADS * IDX_DIM // LANES * Q_BLOCK, LANES), BF16),
        jax.ShapeDtypeStruct((B, S, KV_RANK), BF16),
        jax.ShapeDtypeStruct((B, KV_RANK, S), BF16),
        jax.ShapeDtypeStruct((B, S, 2 * LANES), BF16),
        jax.ShapeDtypeStruct((B, IDX_HEADS, S), F32),
        jax.ShapeDtypeStruct((B, S, kw), BF16),
        jax.ShapeDtypeStruct((B, S, kw), BF16),
        jax.ShapeDtypeStruct((B, S, REC_WIDTH), BF16),
        jax.ShapeDtypeStruct((B, REC_WIDTH, S), BF16),
        jax.ShapeDtypeStruct((B, S, kw), F32),
        jax.ShapeDtypeStruct((B, S, REC_WIDTH), BF16),
        jax.ShapeDtypeStruct((B, S, D), BF16),
        jax.ShapeDtypeStruct((B, S, D), BF16),
    )
    out_specs = (
        pl.BlockSpec((1, nqb, N_ATTN_HEADS * Q_BLOCK, KV_RANK), lambda b, i: (b, i, 0, 0)),
        pl.BlockSpec((1, nqb, IDX_HEADS * IDX_DIM // LANES * Q_BLOCK, LANES), lambda b, i: (b, i, 0, 0)),
        tok(KV_RANK),
        pl.BlockSpec((1, KV_RANK, tm), lambda b, i: (b, 0, i)),
        tok(2 * LANES),
        pl.BlockSpec((1, IDX_HEADS, tm), lambda b, i: (b, 0, i)),
        tok(kw), tok(kw), tok(REC_WIDTH),
        pl.BlockSpec((1, REC_WIDTH, tm), lambda b, i: (b, 0, i)),
        tok(kw), tok(REC_WIDTH),
        tok(D), tok(D),
    )
    vec = lambda: pl.BlockSpec((1, 1, D), lambda b, i: (b, 0, 0))
    return pl.pallas_call(
        functools.partial(_proj_kernel, layer),
        out_shape=out_shapes,
        grid=(B, S // tm),
        in_specs=[tok(D), vec(), vec(), full(g1), full(wa), full(wrec), full(wg), full(gcq), full(gckv),
                  full(gk), full(wq), full(wiq), full(lb_logits)],
        out_specs=out_specs,
        compiler_params=_cparams(("arbitrary", "arbitrary")),
        name="proj",
    )(x, sh1, sc1, g1, wa, wrec, wg, gcq, gckv, gk, wq, wiq, lb_logits)


def _sort_key(score):
    bits = lax.bitcast_convert_type(score, I32)
    return bits ^ (lax.shift_right_arithmetic(bits, 31) & jnp.int32(0x7FFFFFFF))


def _dsa_kernel(S, KC, KA, TOPK, q_ref, qi_ref, wt_ref, kv_ref, kvt_ref, kidx_ref, wvt_ref, ya_ref,
                keys_ref, acc_ref, m_ref, a_ref, p_ref):
    i = pl.program_id(1)
    n_ch = ((i + 1) * Q_BLOCK + KC - 1) // KC
    qpos = i * Q_BLOCK + lax.broadcasted_iota(I32, (1, Q_BLOCK), 1)
    wt = wt_ref[0]
    wrow = [wt[hd:hd + 1, :] for hd in range(IDX_HEADS)]
    qi = qi_ref[0, 0]
    n_t = IDX_HEADS * IDX_DIM // LANES

    def kpos_of(c):
        return c * KC + lax.broadcasted_iota(I32, (KC, 1), 0)

    def score_body(c, carry):
        koff = pl.multiple_of(c * KC, KC)
        la = _nt_dot(kidx_ref[0, pl.ds(koff, KC), 0:LANES], qi)
        lb = _nt_dot(kidx_ref[0, pl.ds(koff, KC), LANES:2 * LANES], qi)
        score = jnp.zeros((KC, Q_BLOCK), F32)
        for t in range(n_t):
            lanes = slice(t * Q_BLOCK, (t + 1) * Q_BLOCK)
            score = score + wrow[2 * t] * jnp.maximum(la[:, lanes], 0.0)
            score = score + wrow[2 * t + 1] * jnp.maximum(lb[:, lanes], 0.0)
        score = jnp.where(kpos_of(c) <= qpos, score, NEG)
        keys_ref[pl.ds(koff, KC), :] = _sort_key(score)
        return carry

    lax.fori_loop(0, n_ch, score_body, 0)

    def count(pred):
        def body(c, acc):
            kk = keys_ref[pl.ds(pl.multiple_of(c * KC, KC), KC), :]
            ones = jnp.where(pred(kk), 1, 0).astype(I32)
            return acc + jnp.sum(ones.reshape(KC // SUBLANES, SUBLANES, Q_BLOCK), axis=0)
        acc = lax.fori_loop(0, n_ch, body, jnp.zeros((SUBLANES, Q_BLOCK), I32))
        return jnp.sum(acc, axis=0, keepdims=True)

    def bis_body(it, v):
        cand = v + lax.shift_left(jnp.int32(1), 31 - it)
        cnt = count(lambda kk: kk >= cand)
        return jnp.where(cnt >= TOPK, cand, v)

    v = lax.fori_loop(0, 32, bis_body, jnp.full((1, Q_BLOCK), -2 ** 31, I32))
    n_gt = count(lambda kk: kk > v)
    n_ge = count(lambda kk: kk >= v)
    need = (TOPK - n_gt).astype(F32)
    any_split = jnp.max(jnp.where(n_ge > TOPK, 1, 0)) > 0

    @pl.when(any_split)
    def _():
        tri = jnp.where(lax.broadcasted_iota(I32, (KC, KC), 1) <= lax.broadcasted_iota(I32, (KC, KC), 0),
                        1.0, 0.0).astype(BF16)

        def tie_body(c, seen):
            koff = pl.multiple_of(c * KC, KC)
            kk = keys_ref[pl.ds(koff, KC), :]
            tie = kk == v
            tie_f = jnp.where(tie, 1.0, 0.0)
            rank = _dot(tri, tie_f.astype(BF16)) + seen
            keys_ref[pl.ds(koff, KC), :] = jnp.where(tie & (rank > need), v - 1, kk)
            return seen + jnp.sum(tie_f, axis=0, keepdims=True)

        lax.fori_loop(0, n_ch, tie_body, jnp.zeros((1, Q_BLOCK), F32))

    m_ref[...] = jnp.full(m_ref.shape, NEG, F32)
    acc_ref[...] = jnp.zeros(acc_ref.shape, F32)
    ones_rows = jnp.ones((ONES_ROWS, KA), BF16)

    def att_body(c, carry):
        koff = pl.multiple_of(c * KA, KA)
        kvc = kv_ref[0, pl.ds(koff, KA), :]
        kpos = c * KA + lax.broadcasted_iota(I32, (KA, 1), 0)
        msk = (keys_ref[pl.ds(koff, KA), :] >= v) & (kpos <= qpos)
        for hd in range(N_ATTN_HEADS):
            lanes = slice(hd * Q_BLOCK, (hd + 1) * Q_BLOCK)
            s = jnp.where(msk, _nt_dot(kvc, q_ref[0, 0, hd * Q_BLOCK:(hd + 1) * Q_BLOCK, :]), -jnp.inf)
            m_prev = m_ref[:, lanes]
            m_next = jnp.maximum(m_prev, jnp.max(s, axis=0, keepdims=True))
            a_ref[:, lanes] = jnp.exp2(m_prev - m_next)
            m_ref[:, lanes] = m_next
            p_ref[:, lanes] = jnp.exp2(s - m_next).astype(BF16)
        kvt1 = jnp.concatenate([kvt_ref[0, :, pl.ds(koff, KA)], ones_rows], axis=0)
        acc_ref[...] = acc_ref[...] * a_ref[...] + _dot(kvt1, p_ref[...])
        return carry

    lax.fori_loop(0, ((i + 1) * Q_BLOCK + KA - 1) // KA, att_body, 0)
    o = (acc_ref[0:KV_RANK, :] / acc_ref[KV_RANK:KV_RANK + 1, :]).astype(BF16)
    ya_t = jnp.concatenate([_dot(wvt_ref[hd], o[:, hd * Q_BLOCK:(hd + 1) * Q_BLOCK])
                            for hd in range(N_ATTN_HEADS)], axis=0)
    ya_ref[0] = ya_t.T.astype(BF16)


def _dsa(q4, qi4, wt, kv, kvt, kidx2, wvt):
    B, nq = q4.shape[0], q4.shape[1]
    S = kv.shape[1]
    KC = min(512, S)
    KA = min(512, S)
    topk = min(TOPK_MAX, S // 4)
    assert S % KC == 0 and KC % KA == 0 and KA % Q_BLOCK == 0
    hq = N_ATTN_HEADS * Q_BLOCK
    return pl.pallas_call(
        functools.partial(_dsa_kernel, S, KC, KA, topk),
        out_shape=jax.ShapeDtypeStruct((B, S, ATTN_WIDTH), BF16),
        grid=(B, nq),
        in_specs=[
            pl.BlockSpec((1, 1) + q4.shape[2:], lambda b, i: (b, i, 0, 0)),
            pl.BlockSpec((1, 1) + qi4.shape[2:], lambda b, i: (b, i, 0, 0)),
            pl.BlockSpec((1, IDX_HEADS, Q_BLOCK), lambda b, i: (b, 0, i)),
            pl.BlockSpec((1, S, KV_RANK), lambda b, i: (b, 0, 0)),
            pl.BlockSpec((1, KV_RANK, S), lambda b, i: (b, 0, 0)),
            pl.BlockSpec((1, S, 2 * LANES), lambda b, i: (b, 0, 0)),
            pl.BlockSpec(wvt.shape, lambda b, i: (0, 0, 0)),
        ],
        out_specs=pl.BlockSpec((1, Q_BLOCK, ATTN_WIDTH), lambda b, i: (b, i, 0)),
        scratch_shapes=[
            pltpu.VMEM((S, Q_BLOCK), I32),
            pltpu.VMEM((KV_RANK + ONES_ROWS, hq), F32),
            pltpu.VMEM((1, hq), F32),
            pltpu.VMEM((1, hq), F32),
            pltpu.VMEM((KA, hq), BF16),
        ],
        compiler_params=_cparams(("arbitrary", "arbitrary")),
        name="dsa",
    )(q4, qi4, wt, kv, kvt, kidx2, wvt)


def _hgrn_kernel(TS, qs_ref, k_ref, v_ref, vt_ref, lf_ref, og_ref, grec_ref, y_ref,
                 st_ref, aloc_ref, e_ref, qh_ref, kh_ref, o_ref, g_ref, sb_ref, kblk_ref):
    n_tiles = REC_WIDTH // LANES
    nsb = TS // SUB

    @pl.when(pl.program_id(1) == 0)
    def _():
        st_ref[...] = jnp.zeros(st_ref.shape, F32)

    a = lf_ref[0]
    r = lax.broadcasted_iota(I32, a.shape, 0) % SUB
    for sh in (1, 2, 4, 8):
        a = a + jnp.where(r >= sh, pltpu.roll(a, sh, 0), 0.0)
    aend = jnp.where(r == SUB - 1, a, 0.0)
    for sh in (1, 2, 4, 8):
        aend = aend + jnp.where(r + sh <= SUB - 1, pltpu.roll(aend, TS - sh, 0), 0.0)
    e = jnp.exp(a)
    aloc_ref[...] = a
    e_ref[...] = e
    qh_ref[...] = (qs_ref[0].astype(F32) * e).astype(BF16)
    kh_ref[...] = (k_ref[0].astype(F32) * jnp.exp(aend - a)).astype(BF16)

    half = REC_K_DIM
    li = lax.broadcasted_iota(I32, (LANES, LANES), 0) // half
    lj = lax.broadcasted_iota(I32, (LANES, LANES), 1) // half
    bd_f = jnp.where(li == lj, 1.0, 0.0)
    bd_b = bd_f.astype(BF16)
    gsel = jnp.where(lax.broadcasted_iota(I32, (SUB, SUB * SUB), 0)
                     == lax.broadcasted_iota(I32, (SUB, SUB * SUB), 1) // SUB, 1.0, 0.0).astype(BF16)
    srow = lax.broadcasted_iota(I32, (SUB, LANES), 0)

    def diag_body(g, carry):
        units = []
        for jj in range(DIAG_GROUP):
            rows = pl.ds(pl.multiple_of((g * DIAG_GROUP + jj) * SUB, SUB), SUB)
            for p in range(n_tiles):
                units.append((rows, slice(p * LANES, (p + 1) * LANES)))
        wsts = []
        for rows, lanes in units:
            a_blk = aloc_ref[rows, lanes]
            q_blk = qs_ref[0, rows, lanes].astype(F32)
            k_blk = k_ref[0, rows, lanes].astype(F32)
            ws = []
            for t in range(SUB):
                d = a_blk[t:t + 1, :] - a_blk
                w = jnp.exp(jnp.where(srow <= t, d, -jnp.inf)) * (k_blk * q_blk[t:t + 1, :])
                ws.append(w.astype(BF16))
            wsts.append(jnp.concatenate(ws, axis=0))
        rexps = [_dot(wst, bd_b) for wst in wsts]
        m2s = []
        for (rows, lanes), rexp in zip(units, rexps):
            v_f = v_ref[0, rows, lanes].astype(F32)
            m2s.append((rexp * jnp.concatenate([v_f] * SUB, axis=0)).astype(BF16))
        outs = [_dot(gsel, m2) for m2 in m2s]
        for (rows, lanes), out in zip(units, outs):
            o_ref[rows, lanes] = out
        return carry

    lax.fori_loop(0, nsb // DIAG_GROUP, diag_body, 0)

    @pl.when((pl.program_id(0) == 0) & (pl.program_id(1) == 0))
    def _():
        kblk_ref[...] = jnp.zeros(kblk_ref.shape, BF16)

    for p in range(n_tiles):
        lanes = slice(p * LANES, (p + 1) * LANES)
        for j in range(nsb):
            kblk_ref[j * SUB:(j + 1) * SUB, j * LANES:(j + 1) * LANES] = kh_ref[j * SUB:(j + 1) * SUB, lanes]
        g_ref[p] = _dot(vt_ref[0, lanes, :], kblk_ref[...])

    for p in range(n_tiles):
        lanes = slice(p * LANES, (p + 1) * LANES)
        st = st_ref[p]
        for j in range(nsb):
            sb_ref[j, p] = st.astype(BF16)
            dec = e_ref[(j + 1) * SUB - 1:(j + 1) * SUB, lanes]
            st = st * dec + g_ref[p, :, j * LANES:(j + 1) * LANES] * bd_f
        st_ref[p] = st

    for j in range(nsb):
        rows = slice(j * SUB, (j + 1) * SUB)
        for p in range(n_tiles):
            lanes = slice(p * LANES, (p + 1) * LANES)
            o_ref[rows, lanes] = o_ref[rows, lanes] + _nt_dot(qh_ref[rows, lanes], sb_ref[j, p])

    o = o_ref[...]
    o2 = o * o
    hi = o2.astype(BF16)
    lo = (o2 - hi.astype(F32)).astype(BF16)
    ms = jnp.concatenate(
        [_dot(hi[:, p * LANES:(p + 1) * LANES], bd_b) + _dot(lo[:, p * LANES:(p + 1) * LANES], bd_b)
         for p in range(n_tiles)], axis=1) * (1.0 / REC_V_DIM)
    y = o * lax.rsqrt(ms + EPS) * grec_ref[...]
    y_ref[0] = (y * og_ref[0].astype(F32)).astype(BF16)


def _hgrn(qs, k, v, vt, lf, og, grec_t, ts):
    B, S, W = qs.shape
    nsb = ts // SUB
    n_tiles = W // LANES
    tok = lambda: pl.BlockSpec((1, ts, W), lambda b, i: (b, i, 0))
    return pl.pallas_call(
        functools.partial(_hgrn_kernel, ts),
        out_shape=jax.ShapeDtypeStruct((B, S, W), BF16),
        grid=(B, S // ts),
        in_specs=[tok(), tok(), tok(), pl.BlockSpec((1, W, ts), lambda b, i: (b, 0, i)), tok(), tok(),
                  pl.BlockSpec((1, W), lambda b, i: (0, 0))],
        out_specs=tok(),
        scratch_shapes=[
            pltpu.VMEM((n_tiles, LANES, LANES), F32),
            pltpu.VMEM((ts, W), F32),
            pltpu.VMEM((ts, W), F32),
            pltpu.VMEM((ts, W), BF16),
            pltpu.VMEM((ts, W), BF16),
            pltpu.VMEM((ts, W), F32),
            pltpu.VMEM((n_tiles, LANES, nsb * LANES), F32),
            pltpu.VMEM((nsb, n_tiles, LANES, LANES), BF16),
            pltpu.VMEM((ts, nsb * LANES), BF16),
        ],
        compiler_params=_cparams(("arbitrary", "arbitrary")),
        name="hgrn",
    )(qs, k, v, vt, lf, og, grec_t)


def _merge_kernel(x_ref, ya_ref, yr_ref, sga_ref, sgr_ref, gt1_ref, sh2_ref, sc2_ref, g2_ref,
                  wba_ref, wbr_ref, wo_ref, wrh_ref, wrl_ref, br_ref, x1_out, h2_out, rt_out):
    merged = (sga_ref[0].astype(F32) * _dot(ya_ref[0], wba_ref[...])
              + sgr_ref[0].astype(F32) * _dot(yr_ref[0], wbr_ref[...]))
    x1 = x_ref[0] + gt1_ref[0] * _dot(merged.astype(BF16), wo_ref[...])
    x1_out[0] = x1
    h2 = (_rms(x1) * g2_ref[...]) * (1.0 + sc2_ref[0]) + sh2_ref[0]
    h2_out[0] = h2

    lg = _split_dot(h2, wrh_ref[...], wrl_ref[...]) + br_ref[...]
    lane = lax.broadcasted_iota(I32, lg.shape, 1)
    big = jnp.int32(1 << 20)
    gl = jnp.where(lane < N_GROUPS, lg, -jnp.inf)
    gmax = jnp.max(gl, axis=1, keepdims=True)
    gsel = jnp.min(jnp.where(gl == gmax, lane, big), axis=1, keepdims=True)
    ggate = 1.0 / jnp.sum(jnp.exp(gl - gmax), axis=1, keepdims=True)
    lo = N_GROUPS + EXPERTS_PER_GROUP * gsel
    el = jnp.where((lane >= lo) & (lane < lo + EXPERTS_PER_GROUP), lg, -jnp.inf)
    v1 = jnp.max(el, axis=1, keepdims=True)
    i1 = jnp.min(jnp.where(el == v1, lane, big), axis=1, keepdims=True)
    el2 = jnp.where(lane == i1, -jnp.inf, el)
    v2 = jnp.max(el2, axis=1, keepdims=True)
    i2 = jnp.min(jnp.where(el2 == v2, lane, big), axis=1, keepdims=True)
    e2 = jnp.exp(v2 - v1)
    den = 1.0 + e2
    g1 = (1.0 / den) * ggate
    g2 = (e2 / den) * ggate
    rt = jnp.where(lane == 0, (i1 - N_GROUPS).astype(F32),
                   jnp.where(lane == 1, (i2 - N_GROUPS).astype(F32),
                             jnp.where(lane == 2, g1, jnp.where(lane == 3, g2, 0.0))))
    rt_out[0] = rt


def _merge(x, ya, yr, sga, sgr, gt1, sh2, sc2, g2, wba, wbr, wo, wrh, wrl, br, tm):
    B, S, D = x.shape
    tok = lambda w: pl.BlockSpec((1, tm, w), lambda b, i: (b, i, 0))
    vec = lambda: pl.BlockSpec((1, 1, D), lambda b, i: (b, 0, 0))
    full = lambda a: pl.BlockSpec(a.shape, lambda b, i: (0,) * a.ndim)
    return pl.pallas_call(
        _merge_kernel,
        out_shape=(jax.ShapeDtypeStruct((B, S, D), F32), jax.ShapeDtypeStruct((B, S, D), F32),
                   jax.ShapeDtypeStruct((B, S, LANES), F32)),
        grid=(B, S // tm),
        in_specs=[tok(D), tok(ATTN_WIDTH), tok(REC_WIDTH), tok(D), tok(D), vec(), vec(), vec(), full(g2),
                  full(wba), full(wbr), full(wo), full(wrh), full(wrl), full(br)],
        out_specs=(tok(D), tok(D), tok(LANES)),
        compiler_params=_cparams(("arbitrary", "arbitrary")),
        name="merge",
    )(x, ya, yr, sga, sgr, gt1, sh2, sc2, g2, wba, wbr, wo, wrh, wrl, br)


def _row_gather(src_hbm, idx_ref, dst, sem, n):
    def body(r, carry):
        pltpu.make_async_copy(src_hbm.at[pl.ds(idx_ref[0, 0, r], 1)], dst.at[pl.ds(r, 1)], sem).start()
        return carry
    lax.fori_loop(0, n, body, 0, unroll=8)


def _row_gather_wait(src_hbm, dst, sem, n):
    pltpu.make_async_copy(src_hbm.at[pl.ds(0, n)], dst, sem).wait()


def _moe_kernel(BLK, be_ref, tok_ref, tokn_ref, gate_ref, h2_hbm, wg_ref, wu_ref, wd_ref, y_ref,
                xbuf, sem, wgb, wub, wdb):
    i = pl.program_id(0)
    n = pl.num_programs(0)
    slot = lax.rem(i, 2)

    @pl.when((i == 0) | (be_ref[i] != be_ref[jnp.maximum(i - 1, 0)]))
    def _():
        wgb[...] = wg_ref[0, 0].astype(BF16)
        wub[...] = wu_ref[0, 0].astype(BF16)
        wdb[...] = wd_ref[0, 0].astype(BF16)

    @pl.when(i == 0)
    def _():
        _row_gather(h2_hbm, tok_ref, xbuf.at[0], sem.at[0], BLK)

    @pl.when(i + 1 < n)
    def _():
        _row_gather(h2_hbm, tokn_ref, xbuf.at[1 - slot], sem.at[1 - slot], BLK)

    _row_gather_wait(h2_hbm, xbuf.at[slot], sem.at[slot], BLK)
    xb = xbuf[slot].astype(BF16)
    g = _dot(xb, wgb[...])
    hid = (g * jax.nn.sigmoid(g)) * _dot(xb, wub[...])
    y_ref[...] = _dot(hid.astype(BF16), wdb[...]) * gate_ref[...]


def _moe(layer, blk_expert, buf_tok3, buf_gate, h2, wgate, wup, wdown, blk):
    P = buf_gate.shape[0]
    n_blk = P // blk
    D = h2.shape[1]
    grid_spec = pltpu.PrefetchScalarGridSpec(
        num_scalar_prefetch=1,
        grid=(n_blk,),
        in_specs=[
            pl.BlockSpec((1, 1, blk), lambda i, be: (i, 0, 0), memory_space=pltpu.SMEM),
            pl.BlockSpec((1, 1, blk), lambda i, be: (jnp.minimum(i + 1, n_blk - 1), 0, 0),
                         memory_space=pltpu.SMEM),
            pl.BlockSpec((blk, 1), lambda i, be: (i, 0)),
            pl.BlockSpec(memory_space=pl.ANY),
            pl.BlockSpec((1, 1, D, D_EXPERT), lambda i, be: (layer, be[i], 0, 0)),
            pl.BlockSpec((1, 1, D, D_EXPERT), lambda i, be: (layer, be[i], 0, 0)),
            pl.BlockSpec((1, 1, D_EXPERT, D), lambda i, be: (layer, be[i], 0, 0)),
        ],
        out_specs=pl.BlockSpec((blk, D), lambda i, be: (i, 0)),
        scratch_shapes=[pltpu.VMEM((2, blk, D), F32), pltpu.SemaphoreType.DMA((2,)),
                        pltpu.VMEM((D, D_EXPERT), BF16), pltpu.VMEM((D, D_EXPERT), BF16),
                        pltpu.VMEM((D_EXPERT, D), BF16)],
    )
    return pl.pallas_call(
        functools.partial(_moe_kernel, blk),
        out_shape=jax.ShapeDtypeStruct((P, D), F32),
        grid_spec=grid_spec,
        compiler_params=_cparams(("arbitrary",)),
        name="moe",
    )(blk_expert, buf_tok3, buf_tok3, buf_gate.reshape(P, 1), h2, wgate, wup, wdown)


def _comb_kernel(TM, final, d_ref, dn_ref, x1_ref, gt2_ref, gf_ref, ys_hbm, o_ref, buf, sem):
    i = pl.program_id(0)
    n = pl.num_programs(0)
    slot = lax.rem(i, 2)

    @pl.when(i == 0)
    def _():
        _row_gather(ys_hbm, d_ref, buf.at[0], sem.at[0], 2 * TM)

    @pl.when(i + 1 < n)
    def _():
        _row_gather(ys_hbm, dn_ref, buf.at[1 - slot], sem.at[1 - slot], 2 * TM)

    _row_gather_wait(ys_hbm, buf.at[slot], sem.at[slot], 2 * TM)
    x2 = x1_ref[...] + gt2_ref[0] * (buf[slot, 0:TM, :] + buf[slot, TM:2 * TM, :])
    if final:
        x2 = _rms(x2) * gf_ref[...]
    o_ref[...] = x2


def _combine(dest3, x1, gt2, gf, ys, S, tm, final):
    T, D = x1.shape
    n = T // tm
    per_b = S // tm
    return pl.pallas_call(
        functools.partial(_comb_kernel, tm, final),
        out_shape=jax.ShapeDtypeStruct((T, D), F32),
        grid=(n,),
        in_specs=[
            pl.BlockSpec((1, 1, 2 * tm), lambda i: (i, 0, 0), memory_space=pltpu.SMEM),
            pl.BlockSpec((1, 1, 2 * tm), lambda i: (jnp.minimum(i + 1, n - 1), 0, 0), memory_space=pltpu.SMEM),
            pl.BlockSpec((tm, D), lambda i: (i, 0)),
            pl.BlockSpec((1, 1, D), lambda i: (i // per_b, 0, 0)),
            pl.BlockSpec((1, D), lambda i: (0, 0)),
            pl.BlockSpec(memory_space=pl.ANY),
        ],
        out_specs=pl.BlockSpec((tm, D), lambda i: (i, 0)),
        scratch_shapes=[pltpu.VMEM((2, 2 * tm, D), F32), pltpu.SemaphoreType.DMA((2,))],
        compiler_params=_cparams(("arbitrary",)),
        name="combine",
    )(dest3, dest3, x1, gt2, gf, ys)


def _dispatch(route, blk):
    T = route.shape[0]
    N = T * TOP_K_IN_GROUP
    eid = route[:, 0:TOP_K_IN_GROUP].astype(I32).reshape(N)
    gate = route[:, TOP_K_IN_GROUP:2 * TOP_K_IN_GROUP].reshape(N)
    oh = (eid[:, None] == jnp.arange(N_EXPERTS, dtype=I32)[None, :]).astype(I32)
    counts = jnp.sum(oh, axis=0)
    rank = jnp.take_along_axis(jnp.cumsum(oh, axis=0) - oh, eid[:, None], axis=1)[:, 0]
    padded = (counts + blk - 1) // blk * blk
    pad_end = jnp.cumsum(padded)
    pad_start = pad_end - padded
    dest = pad_start[eid] + rank
    P = N + N_EXPERTS * blk
    n_blk = P // blk
    token_id = jnp.repeat(jnp.arange(T, dtype=I32), TOP_K_IN_GROUP)
    buf_tok = jnp.zeros((P,), I32).at[dest].set(token_id)
    buf_gate = jnp.zeros((P,), F32).at[dest].set(gate)
    blk_expert = jnp.minimum(jnp.searchsorted(pad_end, jnp.arange(n_blk, dtype=I32) * blk, side="right"),
                             N_EXPERTS - 1).astype(I32)
    return blk_expert, buf_tok.reshape(n_blk, 1, blk), buf_gate, dest.reshape(T, TOP_K_IN_GROUP)


def kernel(x, c, w_mod, b_mod, g_norm1, g_norm2, w_in, g_cq, g_ckv, g_kidx, w_q_up, w_idx_q, w_v_up,
           lb_logits, g_rec, w_branch_a, w_branch_r, w_out, w_grp, b_grp, w_exp_router, b_exp_router,
           w_gate, w_up, w_down, g_final):
    B, S, D = x.shape
    L = w_mod.shape[0]
    T = B * S
    tm = min(256, S)
    ts = min(256, S)
    tmc = min(128, S)
    blk = 256
    kw = N_REC_HEADS * REC_K_DIM
    na = Q_RANK + KV_RANK + IDX_DIM + IDX_HEADS
    n_rec = 2 * kw + 2 * REC_WIDTH

    mod = _modulation(c, w_mod, b_mod)
    for l in range(L):
        m6 = mod[l].reshape(B, 6, 1, D)
        sh1, sc1, gt1, sh2, sc2, gt2 = (m6[:, j] for j in range(6))
        wa = jnp.pad(w_in[l, :, :na], ((0, 0), (0, 512 - na))).astype(BF16)
        wrec = w_in[l, :, na:na + n_rec].astype(BF16)
        wg = w_in[l, :, na + n_rec:].astype(BF16)
        gk = jnp.pad(g_kidx[l], (0, LANES - IDX_DIM)).reshape(1, LANES)
        (q4, qi4, kv, kvt, kidx2, wt, qs, kk, vv, vt, lf, og, sga, sgr) = _proj(
            l, x, sh1, sc1, g_norm1[l].reshape(1, D), wa, wrec, wg, g_cq[l].reshape(1, -1),
            g_ckv[l].reshape(1, -1), gk, w_q_up[l].astype(BF16), w_idx_q[l].astype(BF16), lb_logits, tm)
        wvt = jnp.swapaxes(w_v_up[l], 1, 2).astype(BF16)
        ya = _dsa(q4, qi4, wt, kv, kvt, kidx2, wvt)
        yr = _hgrn(qs, kk, vv, vt, lf, og, jnp.tile(g_rec[l], N_REC_HEADS).reshape(1, REC_WIDTH), ts)
        wr = jnp.pad(jnp.concatenate([w_grp[l], w_exp_router[l]], axis=1),
                     ((0, 0), (0, LANES - N_GROUPS - N_EXPERTS)))
        wrh, wrl = _hi_lo(wr)
        br = jnp.pad(jnp.concatenate([b_grp[l], b_exp_router[l]]), (0, LANES - N_GROUPS - N_EXPERTS)).reshape(1, LANES)
        x1, h2, route = _merge(x, ya, yr, sga, sgr, gt1, sh2, sc2, g_norm2[l].reshape(1, D),
                               w_branch_a[l].astype(BF16), w_branch_r[l].astype(BF16), w_out[l].astype(BF16),
                               wrh, wrl, br, tm)
        blk_expert, buf_tok3, buf_gate, dest = _dispatch(route.reshape(T, LANES), blk)
        ys = _moe(l, blk_expert, buf_tok3, buf_gate, h2.reshape(T, D), w_gate, w_up, w_down, blk)
        dest3 = dest.reshape(T // tmc, tmc, TOP_K_IN_GROUP).transpose(0, 2, 1).reshape(T // tmc, 1, 2 * tmc)
        x = _combine(dest3, x1.reshape(T, D), gt2, g_final.reshape(1, D), ys, S, tmc, l == L - 1).reshape(B, S, D)
    return x
```

```python
import functools

import jax
import jax.numpy as jnp
import numpy as np
from jax import lax
from jax.experimental import pallas as pl
from jax.experimental.pallas import tpu as pltpu

N_ATTN_HEADS = 8
Q_RANK = 256
KV_RANK = 128
ATTN_V_DIM = 64
ATTN_WIDTH = N_ATTN_HEADS * ATTN_V_DIM
ATTN_SCALE = KV_RANK ** -0.5
IDX_HEADS = 8
IDX_DIM = 64
IDX_W_SCALE = (IDX_HEADS * IDX_DIM) ** -0.5
TOPK_MAX = 256
N_REC_HEADS = 8
REC_K_DIM = 64
REC_V_DIM = 64
REC_WIDTH = N_REC_HEADS * REC_V_DIM
N_GROUPS = 4
EXPERTS_PER_GROUP = 8
N_EXPERTS = N_GROUPS * EXPERTS_PER_GROUP
TOP_K_IN_GROUP = 2
D_EXPERT = 512
EPS = 1e-6
NEG = -1e30
TINY = 1e-30

LANES = 128
SUBLANES = 8
Q_BLOCK = 128
ONES_ROWS = 16
LOG2E = 1.4426950408889634
SUB = 16
DIAG_GROUP = 2
VMEM_LIMIT = 56 * 1024 * 1024

F32 = jnp.float32
BF16 = jnp.bfloat16
I32 = jnp.int32


def _cparams(sem):
    return pltpu.CompilerParams(dimension_semantics=sem, vmem_limit_bytes=VMEM_LIMIT)


def _nt_dot(a, b):
    return lax.dot_general(a, b, (((1,), (1,)), ((), ())), preferred_element_type=F32)


def _tn_dot(a, b):
    return lax.dot_general(a, b, (((0,), (0,)), ((), ())), preferred_element_type=F32)


def _dot(a, b):
    return jnp.dot(a, b, preferred_element_type=F32)


def _split_dot(a_f32, b_hi, b_lo):
    a_hi = a_f32.astype(BF16)
    a_lo = (a_f32 - a_hi.astype(F32)).astype(BF16)
    return _dot(a_hi, b_hi) + (_dot(a_hi, b_lo) + _dot(a_lo, b_hi))


def _hi_lo(w):
    hi = w.astype(BF16)
    lo = (w - hi.astype(F32)).astype(BF16)
    return hi, lo


def _mod_kernel(c_ref, w_ref, b_ref, o_ref):
    c = c_ref[...]
    ca = c * jax.nn.sigmoid(c)
    w = w_ref[0]
    w_hi = w.astype(BF16)
    w_lo = (w - w_hi.astype(F32)).astype(BF16)
    o_ref[0] = _split_dot(ca, w_hi, w_lo) + b_ref[0]


def _modulation(c, w_mod, b_mod):
    L, D, D6 = w_mod.shape
    B = c.shape[0]
    tn = 1024
    return pl.pallas_call(
        _mod_kernel,
        out_shape=jax.ShapeDtypeStruct((L, B, D6), F32),
        grid=(L, D6 // tn),
        in_specs=[
            pl.BlockSpec((B, D), lambda l, n: (0, 0)),
            pl.BlockSpec((1, D, tn), lambda l, n: (l, 0, n)),
            pl.BlockSpec((1, 1, tn), lambda l, n: (l, 0, n)),
        ],
        out_specs=pl.BlockSpec((1, B, tn), lambda l, n: (l, 0, n)),
        compiler_params=_cparams(("arbitrary", "arbitrary")),
        name="modulation",
    )(c, w_mod, b_mod.reshape(L, 1, D6))


def _rms(x, eps=EPS):
    return x * lax.rsqrt(jnp.mean(x * x, axis=-1, keepdims=True) + eps)


def _proj_kernel(layer, x_ref, sh_ref, sc_ref, g1_ref, wa_ref, wrec_ref, wg_ref, gcq_ref, gckv_ref,
                 gk_ref, wq_ref, wiq_ref, lb_ref,
                 q_out, qi_out, kv_out, kvt_out, kidx_out, wt_out, qs_out, k_out, v_out, vt_out, lf_out, og_out,
                 sga_out, sgr_out):
    x = x_ref[0]
    tm = x.shape[0]
    h = (_rms(x) * g1_ref[...]) * (1.0 + sc_ref[0]) + sh_ref[0]
    hb = h.astype(BF16)

    ua = _dot(hb, wa_ref[...])
    cq = (_rms(ua[:, :Q_RANK]) * gcq_ref[...]).astype(BF16)
    q = (_dot(cq, wq_ref[...]) * (ATTN_SCALE * LOG2E)).astype(BF16)
    qi = _dot(cq, wiq_ref[...]).astype(BF16)
    for qb in range(tm // Q_BLOCK):
        rows = slice(qb * Q_BLOCK, (qb + 1) * Q_BLOCK)
        for hd in range(N_ATTN_HEADS):
            q_out[0, qb, hd * Q_BLOCK:(hd + 1) * Q_BLOCK, :] = q[rows, hd * LANES:(hd + 1) * LANES]
        for t in range(IDX_HEADS * IDX_DIM // LANES):
            qi_out[0, qb, t * Q_BLOCK:(t + 1) * Q_BLOCK, :] = qi[rows, t * LANES:(t + 1) * LANES]
    kvn = _rms(ua[:, Q_RANK:Q_RANK + KV_RANK]) * gckv_ref[...]
    kv_out[0] = kvn.astype(BF16)
    kvt_out[0] = kvn.T.astype(BF16)

    t3 = ua[:, Q_RANK + KV_RANK:]
    lane = lax.broadcasted_iota(I32, t3.shape, 1)
    ms = jnp.sum(jnp.where(lane < IDX_DIM, t3 * t3, 0.0), axis=-1, keepdims=True) * (1.0 / IDX_DIM)
    ka = t3 * lax.rsqrt(ms + EPS) * gk_ref[...]
    kb = pltpu.roll(ka, IDX_DIM, 1)
    kidx_out[0] = jnp.concatenate([ka, kb], axis=1).astype(BF16)
    wt_out[0] = (t3 * IDX_W_SCALE).T[IDX_DIM:IDX_DIM + IDX_HEADS, :]

    ur = _dot(hb, wrec_ref[...])
    kw = N_REC_HEADS * REC_K_DIM
    lbl = lb_ref[...]
    e = jnp.exp(lbl - jnp.max(lbl, axis=0, keepdims=True))
    p = e / jnp.sum(e, axis=0, keepdims=True)
    lb = jnp.zeros((1, kw), F32)
    for j in range(1, layer + 1):
        lb = lb + p[j:j + 1, :]
    lb = jnp.clip(lb, 0.0, 1.0)
    sig = jax.nn.sigmoid(ur[:, kw:2 * kw])
    f = lb + (1.0 - lb) * sig
    lf_out[0] = jnp.log(jnp.maximum(f, TINY))
    k_out[0] = ((1.0 - lb) * (1.0 - sig)).astype(BF16)
    qr = ur[:, :kw]
    qs_out[0] = (qr * jax.nn.sigmoid(qr)).astype(BF16)
    vr = ur[:, 2 * kw:2 * kw + REC_WIDTH]
    v_out[0] = vr.astype(BF16)
    vt_out[0] = vr.T.astype(BF16)
    og = ur[:, 2 * kw + REC_WIDTH:]
    og_out[0] = (og * jax.nn.sigmoid(og)).astype(BF16)

    ug = _dot(hb, wg_ref[...])
    D = x.shape[1]
    sga_out[0] = jax.nn.sigmoid(ug[:, :D]).astype(BF16)
    sgr_out[0] = jax.nn.sigmoid(ug[:, D:]).astype(BF16)


def _proj(layer, x, sh1, sc1, g1, wa, wrec, wg, gcq, gckv, gk, wq, wiq, lb_logits, tm):
    B, S, D = x.shape
    nqb = tm // Q_BLOCK
    nq = S // Q_BLOCK
    kw = N_REC_HEADS * REC_K_DIM
    c2 = lambda b, i: (0, 0)
    tok = lambda w: pl.BlockSpec((1, tm, w), lambda b, i: (b, i, 0))
    full = lambda a: pl.BlockSpec(a.shape, c2)
    out_shapes = (
        jax.ShapeDtypeStruct((B, nq, N_ATTN_HEADS * Q_BLOCK, KV_RANK), BF16),
        jax.ShapeDtypeStruct((B, nq, IDX_HEADS * IDX_DIM // LANES * Q_BLOCK, LANES), BF16),
        jax.ShapeDtypeStruct((B, S, KV_RANK), BF16),
        jax.ShapeDtypeStruct((B, KV_RANK, S), BF16),
        jax.ShapeDtypeStruct((B, S, 2 * LANES), BF16),
        jax.ShapeDtypeStruct((B, IDX_HEADS, S), F32),
        jax.ShapeDtypeStruct((B, S, kw), BF16),
        jax.ShapeDtypeStruct((B, S, kw), BF16),
        jax.ShapeDtypeStruct((B, S, REC_WIDTH), BF16),
        jax.ShapeDtypeStruct((B, REC_WIDTH, S), BF16),
        jax.ShapeDtypeStruct((B, S, kw), F32),
        jax.ShapeDtypeStruct((B, S, REC_WIDTH), BF16),
        jax.ShapeDtypeStruct((B, S, D), BF16),
        jax.ShapeDtypeStruct((B, S, D), BF16),
    )
    out_specs = (
        pl.BlockSpec((1, nqb, N_ATTN_HEADS * Q_BLOCK, KV_RANK), lambda b, i: (b, i, 0, 0)),
        pl.BlockSpec((1, nqb, IDX_HEADS * IDX_DIM // LANES * Q_BLOCK, LANES), lambda b, i: (b, i, 0, 0)),
        tok(KV_RANK),
        pl.BlockSpec((1, KV_RANK, tm), lambda b, i: (b, 0, i)),
        tok(2 * LANES),
        pl.BlockSpec((1, IDX_HEADS, tm), lambda b, i: (b, 0, i)),
        tok(kw), tok(kw), tok(REC_WIDTH),
        pl.BlockSpec((1, REC_WIDTH, tm), lambda b, i: (b, 0, i)),
        tok(kw), tok(REC_WIDTH),
        tok(D), tok(D),
    )
    vec = lambda: pl.BlockSpec((1, 1, D), lambda b, i: (b, 0, 0))
    return pl.pallas_call(
        functools.partial(_proj_kernel, layer),
        out_shape=out_shapes,
        grid=(B, S // tm),
        in_specs=[tok(D), vec(), vec(), full(g1), full(wa), full(wrec), full(wg), full(gcq), full(gckv),
                  full(gk), full(wq), full(wiq), full(lb_logits)],
        out_specs=out_specs,
        compiler_params=_cparams(("arbitrary", "arbitrary")),
        name="proj",
    )(x, sh1, sc1, g1, wa, wrec, wg, gcq, gckv, gk, wq, wiq, lb_logits)


def _sort_key(score):
    bits = lax.bitcast_convert_type(score, I32)
    return bits ^ (lax.shift_right_arithmetic(bits, 31) & jnp.int32(0x7FFFFFFF))


def _dsa_kernel(S, KC, KA, TOPK, q_ref, qi_ref, wt_ref, kv_ref, kvt_ref, kidx_ref, wvt_ref, ya_ref,
                keys_ref, acc_ref, m_ref, a_ref, p_ref):
    i = pl.program_id(1)
    n_ch = ((i + 1) * Q_BLOCK + KC - 1) // KC
    qpos = i * Q_BLOCK + lax.broadcasted_iota(I32, (1, Q_BLOCK), 1)
    wt = wt_ref[0]
    wrow = [wt[hd:hd + 1, :] for hd in range(IDX_HEADS)]
    qi = qi_ref[0, 0]
    n_t = IDX_HEADS * IDX_DIM // LANES

    def kpos_of(c):
        return c * KC + lax.broadcasted_iota(I32, (KC, 1), 0)

    def score_body(c, carry):
        koff = pl.multiple_of(c * KC, KC)
        la = _nt_dot(kidx_ref[0, pl.ds(koff, KC), 0:LANES], qi)
        lb = _nt_dot(kidx_ref[0, pl.ds(koff, KC), LANES:2 * LANES], qi)
        score = jnp.zeros((KC, Q_BLOCK), F32)
        for t in range(n_t):
            lanes = slice(t * Q_BLOCK, (t + 1) * Q_BLOCK)
            score = score + wrow[2 * t] * jnp.maximum(la[:, lanes], 0.0)
            score = score + wrow[2 * t + 1] * jnp.maximum(lb[:, lanes], 0.0)
        score = jnp.where(kpos_of(c) <= qpos, score, NEG)
        keys_ref[pl.ds(koff, KC), :] = _sort_key(score)
        return carry

    lax.fori_loop(0, n_ch, score_body, 0)

    def count(pred):
        def body(c, acc):
            kk = keys_ref[pl.ds(pl.multiple_of(c * KC, KC), KC), :]
            ones = jnp.where(pred(kk), 1, 0).astype(I32)
            return acc + jnp.sum(ones.reshape(KC // SUBLANES, SUBLANES, Q_BLOCK), axis=0)
        acc = lax.fori_loop(0, n_ch, body, jnp.zeros((SUBLANES, Q_BLOCK), I32))
        return jnp.sum(acc, axis=0, keepdims=True)

    def bis_body(it, v):
        cand = v + lax.shift_left(jnp.int32(1), 31 - it)
        cnt = count(lambda kk: kk >= cand)
        return jnp.where(cnt >= TOPK, cand, v)

    v = lax.fori_loop(0, 32, bis_body, jnp.full((1, Q_BLOCK), -2 ** 31, I32))
    n_gt = count(lambda kk: kk > v)
    n_ge = count(lambda kk: kk >= v)
    need = (TOPK - n_gt).astype(F32)
    any_split = jnp.max(jnp.where(n_ge > TOPK, 1, 0)) > 0

    @pl.when(any_split)
    def _():
        tri = jnp.where(lax.broadcasted_iota(I32, (KC, KC), 1) <= lax.broadcasted_iota(I32, (KC, KC), 0),
                        1.0, 0.0).astype(BF16)

        def tie_body(c, seen):
            koff = pl.multiple_of(c * KC, KC)
            kk = keys_ref[pl.ds(koff, KC), :]
            tie = kk == v
            tie_f = jnp.where(tie, 1.0, 0.0)
            rank = _dot(tri, tie_f.astype(BF16)) + seen
            keys_ref[pl.ds(koff, KC), :] = jnp.where(tie & (rank > need), v - 1, kk)
            return seen + jnp.sum(tie_f, axis=0, keepdims=True)

        lax.fori_loop(0, n_ch, tie_body, jnp.zeros((1, Q_BLOCK), F32))

    m_ref[...] = jnp.full(m_ref.shape, NEG, F32)
    acc_ref[...] = jnp.zeros(acc_ref.shape, F32)
    ones_rows = jnp.ones((ONES_ROWS, KA), BF16)

    def att_body(c, carry):
        koff = pl.multiple_of(c * KA, KA)
        kvc = kv_ref[0, pl.ds(koff, KA), :]
        kpos = c * KA + lax.broadcasted_iota(I32, (KA, 1), 0)
        msk = (keys_ref[pl.ds(koff, KA), :] >= v) & (kpos <= qpos)
        for hd in range(N_ATTN_HEADS):
            lanes = slice(hd * Q_BLOCK, (hd + 1) * Q_BLOCK)
            s = jnp.where(msk, _nt_dot(kvc, q_ref[0, 0, hd * Q_BLOCK:(hd + 1) * Q_BLOCK, :]), -jnp.inf)
            m_prev = m_ref[:, lanes]
            m_next = jnp.maximum(m_prev, jnp.max(s, axis=0, keepdims=True))
            a_ref[:, lanes] = jnp.exp2(m_prev - m_next)
            m_ref[:, lanes] = m_next
            p_ref[:, lanes] = jnp.exp2(s - m_next).astype(BF16)
        kvt1 = jnp.concatenate([kvt_ref[0, :, pl.ds(koff, KA)], ones_rows], axis=0)
        acc_ref[...] = acc_ref[...] * a_ref[...] + _dot(kvt1, p_ref[...])
        return carry

    lax.fori_loop(0, ((i + 1) * Q_BLOCK + KA - 1) // KA, att_body, 0)
    o = (acc_ref[0:KV_RANK, :] / acc_ref[KV_RANK:KV_RANK + 1, :]).astype(BF16)
    ya_t = jnp.concatenate([_dot(wvt_ref[hd], o[:, hd * Q_BLOCK:(hd + 1) * Q_BLOCK])
                            for hd in range(N_ATTN_HEADS)], axis=0)
    ya_ref[0] = ya_t.T.astype(BF16)


def _dsa(q4, qi4, wt, kv, kvt, kidx2, wvt):
    B, nq = q4.shape[0], q4.shape[1]
    S = kv.shape[1]
    KC = min(512, S)
    KA = min(512, S)
    topk = min(TOPK_MAX, S // 4)
    assert S % KC == 0 and KC % KA == 0 and KA % Q_BLOCK == 0
    hq = N_ATTN_HEADS * Q_BLOCK
    return pl.pallas_call(
        functools.partial(_dsa_kernel, S, KC, KA, topk),
        out_shape=jax.ShapeDtypeStruct((B, S, ATTN_WIDTH), BF16),
        grid=(B, nq),
        in_specs=[
            pl.BlockSpec((1, 1) + q4.shape[2:], lambda b, i: (b, i, 0, 0)),
            pl.BlockSpec((1, 1) + qi4.shape[2:], lambda b, i: (b, i, 0, 0)),
            pl.BlockSpec((1, IDX_HEADS, Q_BLOCK), lambda b, i: (b, 0, i)),
            pl.BlockSpec((1, S, KV_RANK), lambda b, i: (b, 0, 0)),
            pl.BlockSpec((1, KV_RANK, S), lambda b, i: (b, 0, 0)),
            pl.BlockSpec((1, S, 2 * LANES), lambda b, i: (b, 0, 0)),
            pl.BlockSpec(wvt.shape, lambda b, i: (0, 0, 0)),
        ],
        out_specs=pl.BlockSpec((1, Q_BLOCK, ATTN_WIDTH), lambda b, i: (b, i, 0)),
        scratch_shapes=[
            pltpu.VMEM((S, Q_BLOCK), I32),
            pltpu.VMEM((KV_RANK + ONES_ROWS, hq), F32),
            pltpu.VMEM((1, hq), F32),
            pltpu.VMEM((1, hq), F32),
            pltpu.VMEM((KA, hq), BF16),
        ],
        compiler_params=_cparams(("arbitrary", "arbitrary")),
        name="dsa",
    )(q4, qi4, wt, kv, kvt, kidx2, wvt)


def _hgrn_kernel(TS, qs_ref, k_ref, v_ref, vt_ref, lf_ref, og_ref, grec_ref, y_ref,
                 st_ref, aloc_ref, e_ref, qh_ref, kh_ref, o_ref, g_ref, sb_ref, kblk_ref):
    n_tiles = REC_WIDTH // LANES
    nsb = TS // SUB

    @pl.when(pl.program_id(1) == 0)
    def _():
        st_ref[...] = jnp.zeros(st_ref.shape, F32)

    a = lf_ref[0]
    r = lax.broadcasted_iota(I32, a.shape, 0) % SUB
    for sh in (1, 2, 4, 8):
        a = a + jnp.where(r >= sh, pltpu.roll(a, sh, 0), 0.0)
    aend = jnp.where(r == SUB - 1, a, 0.0)
    for sh in (1, 2, 4, 8):
        aend = aend + jnp.where(r + sh <= SUB - 1, pltpu.roll(aend, TS - sh, 0), 0.0)
    e = jnp.exp(a)
    aloc_ref[...] = a
    e_ref[...] = e
    qh_ref[...] = (qs_ref[0].astype(F32) * e).astype(BF16)
    kh_ref[...] = (k_ref[0].astype(F32) * jnp.exp(aend - a)).astype(BF16)

    half = REC_K_DIM
    li = lax.broadcasted_iota(I32, (LANES, LANES), 0) // half
    lj = lax.broadcasted_iota(I32, (LANES, LANES), 1) // half
    bd_f = jnp.where(li == lj, 1.0, 0.0)
    bd_b = bd_f.astype(BF16)
    gsel = jnp.where(lax.broadcasted_iota(I32, (SUB, SUB * SUB), 0)
                     == lax.broadcasted_iota(I32, (SUB, SUB * SUB), 1) // SUB, 1.0, 0.0).astype(BF16)
    srow = lax.broadcasted_iota(I32, (SUB, LANES), 0)

    def diag_body(g, carry):
        units = []
        for jj in range(DIAG_GROUP):
            rows = pl.ds(pl.multiple_of((g * DIAG_GROUP + jj) * SUB, SUB), SUB)
            for p in range(n_tiles):
                units.append((rows, slice(p * LANES, (p + 1) * LANES)))
        wsts = []
        for rows, lanes in units:
            a_blk = aloc_ref[rows, lanes]
            q_blk = qs_ref[0, rows, lanes].astype(F32)
            k_blk = k_ref[0, rows, lanes].astype(F32)
            ws = []
            for t in range(SUB):
                d = a_blk[t:t + 1, :] - a_blk
                w = jnp.exp(jnp.where(srow <= t, d, -jnp.inf)) * (k_blk * q_blk[t:t + 1, :])
                ws.append(w.astype(BF16))
            wsts.append(jnp.concatenate(ws, axis=0))
        rexps = [_dot(wst, bd_b) for wst in wsts]
        m2s = []
        for (rows, lanes), rexp in zip(units, rexps):
            v_f = v_ref[0, rows, lanes].astype(F32)
            m2s.append((rexp * jnp.concatenate([v_f] * SUB, axis=0)).astype(BF16))
        outs = [_dot(gsel, m2) for m2 in m2s]
        for (rows, lanes), out in zip(units, outs):
            o_ref[rows, lanes] = out
        return carry

    lax.fori_loop(0, nsb // DIAG_GROUP, diag_body, 0)

    @pl.when((pl.program_id(0) == 0) & (pl.program_id(1) == 0))
    def _():
        kblk_ref[...] = jnp.zeros(kblk_ref.shape, BF16)

    for p in range(n_tiles):
        lanes = slice(p * LANES, (p + 1) * LANES)
        for j in range(nsb):
            kblk_ref[j * SUB:(j + 1) * SUB, j * LANES:(j + 1) * LANES] = kh_ref[j * SUB:(j + 1) * SUB, lanes]
        g_ref[p] = _dot(vt_ref[0, lanes, :], kblk_ref[...])

    for p in range(n_tiles):
        lanes = slice(p * LANES, (p + 1) * LANES)
        st = st_ref[p]
        for j in range(nsb):
            sb_ref[j, p] = st.astype(BF16)
            dec = e_ref[(j + 1) * SUB - 1:(j + 1) * SUB, lanes]
            st = st * dec + g_ref[p, :, j * LANES:(j + 1) * LANES] * bd_f
        st_ref[p] = st

    for j in range(nsb):
        rows = slice(j * SUB, (j + 1) * SUB)
        for p in range(n_tiles):
            lanes = slice(p * LANES, (p + 1) * LANES)
            o_ref[rows, lanes] = o_ref[rows, lanes] + _nt_dot(qh_ref[rows, lanes], sb_ref[j, p])

    o = o_ref[...]
    o2 = o * o
    hi = o2.astype(BF16)
    lo = (o2 - hi.astype(F32)).astype(BF16)
    ms = jnp.concatenate(
        [_dot(hi[:, p * LANES:(p + 1) * LANES], bd_b) + _dot(lo[:, p * LANES:(p + 1) * LANES], bd_b)
         for p in range(n_tiles)], axis=1) * (1.0 / REC_V_DIM)
    y = o * lax.rsqrt(ms + EPS) * grec_ref[...]
    y_ref[0] = (y * og_ref[0].astype(F32)).astype(BF16)


def _hgrn(qs, k, v, vt, lf, og, grec_t, ts):
    B, S, W = qs.shape
    nsb = ts // SUB
    n_tiles = W // LANES
    tok = lambda: pl.BlockSpec((1, ts, W), lambda b, i: (b, i, 0))
    return pl.pallas_call(
        functools.partial(_hgrn_kernel, ts),
        out_shape=jax.ShapeDtypeStruct((B, S, W), BF16),
        grid=(B, S // ts),
        in_specs=[tok(), tok(), tok(), pl.BlockSpec((1, W, ts), lambda b, i: (b, 0, i)), tok(), tok(),
                  pl.BlockSpec((1, W), lambda b, i: (0, 0))],
        out_specs=tok(),
        scratch_shapes=[
            pltpu.VMEM((n_tiles, LANES, LANES), F32),
            pltpu.VMEM((ts, W), F32),
            pltpu.VMEM((ts, W), F32),
            pltpu.VMEM((ts, W), BF16),
            pltpu.VMEM((ts, W), BF16),
            pltpu.VMEM((ts, W), F32),
            pltpu.VMEM((n_tiles, LANES, nsb * LANES), F32),
            pltpu.VMEM((nsb, n_tiles, LANES, LANES), BF16),
            pltpu.VMEM((ts, nsb * LANES), BF16),
        ],
        compiler_params=_cparams(("arbitrary", "arbitrary")),
        name="hgrn",
    )(qs, k, v, vt, lf, og, grec_t)


def _merge_kernel(x_ref, ya_ref, yr_ref, sga_ref, sgr_ref, gt1_ref, sh2_ref, sc2_ref, g2_ref,
                  wba_ref, wbr_ref, wo_ref, wrh_ref, wrl_ref, br_ref, x1_out, h2_out, rt_out):
    merged = (sga_ref[0].astype(F32) * _dot(ya_ref[0], wba_ref[...])
              + sgr_ref[0].astype(F32) * _dot(yr_ref[0], wbr_ref[...]))
    x1 = x_ref[0] + gt1_ref[0] * _dot(merged.astype(BF16), wo_ref[...])
    x1_out[0] = x1
    h2 = (_rms(x1) * g2_ref[...]) * (1.0 + sc2_ref[0]) + sh2_ref[0]
    _to_token_tiles(h2_out.at[0], h2)

    lg = _split_dot(h2, wrh_ref[...], wrl_ref[...]) + br_ref[...]
    lane = lax.broadcasted_iota(I32, lg.shape, 1)
    big = jnp.int32(1 << 20)
    gl = jnp.where(lane < N_GROUPS, lg, -jnp.inf)
    gmax = jnp.max(gl, axis=1, keepdims=True)
    gsel = jnp.min(jnp.where(gl == gmax, lane, big), axis=1, keepdims=True)
    ggate = 1.0 / jnp.sum(jnp.exp(gl - gmax), axis=1, keepdims=True)
    lo = N_GROUPS + EXPERTS_PER_GROUP * gsel
    el = jnp.where((lane >= lo) & (lane < lo + EXPERTS_PER_GROUP), lg, -jnp.inf)
    v1 = jnp.max(el, axis=1, keepdims=True)
    i1 = jnp.min(jnp.where(el == v1, lane, big), axis=1, keepdims=True)
    el2 = jnp.where(lane == i1, -jnp.inf, el)
    v2 = jnp.max(el2, axis=1, keepdims=True)
    i2 = jnp.min(jnp.where(el2 == v2, lane, big), axis=1, keepdims=True)
    e2 = jnp.exp(v2 - v1)
    den = 1.0 + e2
    g1 = (1.0 / den) * ggate
    g2 = (e2 / den) * ggate
    rt = jnp.where(lane == 0, (i1 - N_GROUPS).astype(F32),
                   jnp.where(lane == 1, (i2 - N_GROUPS).astype(F32),
                             jnp.where(lane == 2, g1, jnp.where(lane == 3, g2, 0.0))))
    rt_out[0] = rt


def _merge(x, ya, yr, sga, sgr, gt1, sh2, sc2, g2, wba, wbr, wo, wrh, wrl, br, tm):
    B, S, D = x.shape
    tok = lambda w: pl.BlockSpec((1, tm, w), lambda b, i: (b, i, 0))
    vec = lambda: pl.BlockSpec((1, 1, D), lambda b, i: (b, 0, 0))
    full = lambda a: pl.BlockSpec(a.shape, lambda b, i: (0,) * a.ndim)
    return pl.pallas_call(
        _merge_kernel,
        out_shape=(jax.ShapeDtypeStruct((B, S, D), F32), jax.ShapeDtypeStruct((B, S * SUBLANES, LANES), F32),
                   jax.ShapeDtypeStruct((B, S, LANES), F32)),
        grid=(B, S // tm),
        in_specs=[tok(D), tok(ATTN_WIDTH), tok(REC_WIDTH), tok(D), tok(D), vec(), vec(), vec(), full(g2),
                  full(wba), full(wbr), full(wo), full(wrh), full(wrl), full(br)],
        out_specs=(tok(D), pl.BlockSpec((1, tm * SUBLANES, LANES), lambda b, i: (b, i, 0)), tok(LANES)),
        compiler_params=_cparams(("arbitrary", "arbitrary")),
        name="merge",
    )(x, ya, yr, sga, sgr, gt1, sh2, sc2, g2, wba, wbr, wo, wrh, wrl, br)


def _to_token_tiles(ref, x):
    n = x.shape[0]
    for s in range(SUBLANES):
        ref[pl.ds(s, n, stride=SUBLANES), :] = x[:, s * LANES:(s + 1) * LANES]


def _from_token_tiles(ref, first_tile, n):
    return jnp.concatenate([ref[pl.ds(first_tile * SUBLANES + s, n, stride=SUBLANES), :]
                            for s in range(SUBLANES)], axis=1)


def _tile_gather(src_hbm, idx_ref, dst, sem, n):
    def body(r, carry):
        src = src_hbm.at[pl.ds(pl.multiple_of(idx_ref[0, 0, r] * SUBLANES, SUBLANES), SUBLANES)]
        pltpu.make_async_copy(src, dst.at[pl.ds(pl.multiple_of(r * SUBLANES, SUBLANES), SUBLANES)], sem).start()
        return carry
    lax.fori_loop(0, n, body, 0, unroll=8)


def _tile_gather_wait(src_hbm, dst, sem, n):
    pltpu.make_async_copy(src_hbm.at[pl.ds(0, n * SUBLANES)], dst, sem).wait()


def _moe_kernel(BLK, be_ref, tok_ref, tokn_ref, h2_hbm, wg_ref, wu_ref, wd_ref, y_ref,
                xbuf, sem, wgb, wub, wdb):
    i = pl.program_id(0)
    n = pl.num_programs(0)
    slot = lax.rem(i, 2)

    @pl.when((i == 0) | (be_ref[i] != be_ref[jnp.maximum(i - 1, 0)]))
    def _():
        wgb[...] = wg_ref[0, 0].astype(BF16)
        wub[...] = wu_ref[0, 0].astype(BF16)
        wdb[...] = wd_ref[0, 0].astype(BF16)

    @pl.when(i == 0)
    def _():
        _tile_gather(h2_hbm, tok_ref, xbuf.at[0], sem.at[0], BLK)

    @pl.when(i + 1 < n)
    def _():
        _tile_gather(h2_hbm, tokn_ref, xbuf.at[1 - slot], sem.at[1 - slot], BLK)

    _tile_gather_wait(h2_hbm, xbuf.at[slot], sem.at[slot], BLK)
    xb = _from_token_tiles(xbuf.at[slot], 0, BLK).astype(BF16)
    g = _dot(xb, wgb[...])
    hid = (g * jax.nn.sigmoid(g)) * _dot(xb, wub[...])
    _to_token_tiles(y_ref, _dot(hid.astype(BF16), wdb[...]))


def _moe(layer, blk_expert, buf_tok3, h2t, wgate, wup, wdown, blk):
    n_blk = buf_tok3.shape[0]
    D = wgate.shape[2]
    grid_spec = pltpu.PrefetchScalarGridSpec(
        num_scalar_prefetch=1,
        grid=(n_blk,),
        in_specs=[
            pl.BlockSpec((1, 1, blk), lambda i, be: (i, 0, 0), memory_space=pltpu.SMEM),
            pl.BlockSpec((1, 1, blk), lambda i, be: (jnp.minimum(i + 1, n_blk - 1), 0, 0),
                         memory_space=pltpu.SMEM),
            pl.BlockSpec(memory_space=pl.ANY),
            pl.BlockSpec((1, 1, D, D_EXPERT), lambda i, be: (layer, be[i], 0, 0)),
            pl.BlockSpec((1, 1, D, D_EXPERT), lambda i, be: (layer, be[i], 0, 0)),
            pl.BlockSpec((1, 1, D_EXPERT, D), lambda i, be: (layer, be[i], 0, 0)),
        ],
        out_specs=pl.BlockSpec((blk * SUBLANES, LANES), lambda i, be: (i, 0)),
        scratch_shapes=[pltpu.VMEM((2, blk * SUBLANES, LANES), F32), pltpu.SemaphoreType.DMA((2,)),
                        pltpu.VMEM((D, D_EXPERT), BF16), pltpu.VMEM((D, D_EXPERT), BF16),
                        pltpu.VMEM((D_EXPERT, D), BF16)],
    )
    return pl.pallas_call(
        functools.partial(_moe_kernel, blk),
        out_shape=jax.ShapeDtypeStruct((n_blk * blk * SUBLANES, LANES), F32),
        grid_spec=grid_spec,
        compiler_params=_cparams(("arbitrary",)),
        name="moe",
    )(blk_expert, buf_tok3, buf_tok3, h2t, wgate, wup, wdown)


def _comb_kernel(TM, final, d_ref, dn_ref, x1_ref, rt_ref, gt2_ref, gf_ref, ys_hbm, o_ref, buf, sem):
    i = pl.program_id(0)
    n = pl.num_programs(0)
    slot = lax.rem(i, 2)

    @pl.when(i == 0)
    def _():
        _tile_gather(ys_hbm, d_ref, buf.at[0], sem.at[0], 2 * TM)

    @pl.when(i + 1 < n)
    def _():
        _tile_gather(ys_hbm, dn_ref, buf.at[1 - slot], sem.at[1 - slot], 2 * TM)

    _tile_gather_wait(ys_hbm, buf.at[slot], sem.at[slot], 2 * TM)
    rt = rt_ref[...]
    y0 = _from_token_tiles(buf.at[slot], 0, TM)
    y1 = _from_token_tiles(buf.at[slot], TM, TM)
    mix = rt[:, TOP_K_IN_GROUP:TOP_K_IN_GROUP + 1] * y0 + rt[:, TOP_K_IN_GROUP + 1:TOP_K_IN_GROUP + 2] * y1
    x2 = x1_ref[...] + gt2_ref[0] * mix
    if final:
        x2 = _rms(x2) * gf_ref[...]
    o_ref[...] = x2


def _combine(dest3, x1, route, gt2, gf, ys, S, tm, final):
    T, D = x1.shape
    n = T // tm
    per_b = S // tm
    return pl.pallas_call(
        functools.partial(_comb_kernel, tm, final),
        out_shape=jax.ShapeDtypeStruct((T, D), F32),
        grid=(n,),
        in_specs=[
            pl.BlockSpec((1, 1, 2 * tm), lambda i: (i, 0, 0), memory_space=pltpu.SMEM),
            pl.BlockSpec((1, 1, 2 * tm), lambda i: (jnp.minimum(i + 1, n - 1), 0, 0), memory_space=pltpu.SMEM),
            pl.BlockSpec((tm, D), lambda i: (i, 0)),
            pl.BlockSpec((tm, LANES), lambda i: (i, 0)),
            pl.BlockSpec((1, 1, D), lambda i: (i // per_b, 0, 0)),
            pl.BlockSpec((1, D), lambda i: (0, 0)),
            pl.BlockSpec(memory_space=pl.ANY),
        ],
        out_specs=pl.BlockSpec((tm, D), lambda i: (i, 0)),
        scratch_shapes=[pltpu.VMEM((2, 2 * tm * SUBLANES, LANES), F32), pltpu.SemaphoreType.DMA((2,))],
        compiler_params=_cparams(("arbitrary",)),
        name="combine",
    )(dest3, dest3, x1, route, gt2, gf, ys)


def _dispatch(route, blk):
    T = route.shape[0]
    N = T * TOP_K_IN_GROUP
    eid = route[:, 0:TOP_K_IN_GROUP].astype(I32).reshape(N)
    oh = (eid[:, None] == jnp.arange(N_EXPERTS, dtype=I32)[None, :]).astype(I32)
    counts = jnp.sum(oh, axis=0)
    rank = jnp.take_along_axis(jnp.cumsum(oh, axis=0) - oh, eid[:, None], axis=1)[:, 0]
    padded = (counts + blk - 1) // blk * blk
    pad_end = jnp.cumsum(padded)
    pad_start = pad_end - padded
    dest = pad_start[eid] + rank
    P = N + N_EXPERTS * blk
    n_blk = P // blk
    token_id = jnp.repeat(jnp.arange(T, dtype=I32), TOP_K_IN_GROUP)
    buf_tok = jnp.zeros((P,), I32).at[dest].set(token_id)
    blk_expert = jnp.minimum(jnp.searchsorted(pad_end, jnp.arange(n_blk, dtype=I32) * blk, side="right"),
                             N_EXPERTS - 1).astype(I32)
    return blk_expert, buf_tok.reshape(n_blk, 1, blk), dest.reshape(T, TOP_K_IN_GROUP)


def kernel(x, c, w_mod, b_mod, g_norm1, g_norm2, w_in, g_cq, g_ckv, g_kidx, w_q_up, w_idx_q, w_v_up,
           lb_logits, g_rec, w_branch_a, w_branch_r, w_out, w_grp, b_grp, w_exp_router, b_exp_router,
           w_gate, w_up, w_down, g_final):
    B, S, D = x.shape
    L = w_mod.shape[0]
    T = B * S
    tm = min(256, S)
    ts = min(256, S)
    tmc = min(128, S)
    blk = 256
    kw = N_REC_HEADS * REC_K_DIM
    na = Q_RANK + KV_RANK + IDX_DIM + IDX_HEADS
    n_rec = 2 * kw + 2 * REC_WIDTH

    mod = _modulation(c, w_mod, b_mod)
    for l in range(L):
        m6 = mod[l].reshape(B, 6, 1, D)
        sh1, sc1, gt1, sh2, sc2, gt2 = (m6[:, j] for j in range(6))
        wa = jnp.pad(w_in[l, :, :na], ((0, 0), (0, 512 - na))).astype(BF16)
        wrec = w_in[l, :, na:na + n_rec].astype(BF16)
        wg = w_in[l, :, na + n_rec:].astype(BF16)
        gk = jnp.pad(g_kidx[l], (0, LANES - IDX_DIM)).reshape(1, LANES)
        (q4, qi4, kv, kvt, kidx2, wt, qs, kk, vv, vt, lf, og, sga, sgr) = _proj(
            l, x, sh1, sc1, g_norm1[l].reshape(1, D), wa, wrec, wg, g_cq[l].reshape(1, -1),
            g_ckv[l].reshape(1, -1), gk, w_q_up[l].astype(BF16), w_idx_q[l].astype(BF16), lb_logits, tm)
        wvt = jnp.swapaxes(w_v_up[l], 1, 2).astype(BF16)
        ya = _dsa(q4, qi4, wt, kv, kvt, kidx2, wvt)
        yr = _hgrn(qs, kk, vv, vt, lf, og, jnp.tile(g_rec[l], N_REC_HEADS).reshape(1, REC_WIDTH), ts)
        wr = jnp.pad(jnp.concatenate([w_grp[l], w_exp_router[l]], axis=1),
                     ((0, 0), (0, LANES - N_GROUPS - N_EXPERTS)))
        wrh, wrl = _hi_lo(wr)
        br = jnp.pad(jnp.concatenate([b_grp[l], b_exp_router[l]]), (0, LANES - N_GROUPS - N_EXPERTS)).reshape(1, LANES)
        x1, h2t, route = _merge(x, ya, yr, sga, sgr, gt1, sh2, sc2, g_norm2[l].reshape(1, D),
                               w_branch_a[l].astype(BF16), w_branch_r[l].astype(BF16), w_out[l].astype(BF16),
                               wrh, wrl, br, tm)
        route = route.reshape(T, LANES)
        blk_expert, buf_tok3, dest = _dispatch(route, blk)
        ys = _moe(l, blk_expert, buf_tok3, h2t.reshape(T * SUBLANES, LANES), w_gate, w_up, w_down, blk)
        dest3 = dest.reshape(T // tmc, tmc, TOP_K_IN_GROUP).transpose(0, 2, 1).reshape(T // tmc, 1, 2 * tmc)
        x = _combine(dest3, x1.reshape(T, D), route, gt2, g_final.reshape(1, D), ys, S, tmc, l == L - 1).reshape(B, S, D)
    return x
```

```python
import functools

import jax
import jax.numpy as jnp
import numpy as np
from jax import lax
from jax.experimental import pallas as pl
from jax.experimental.pallas import tpu as pltpu

N_ATTN_HEADS = 8
Q_RANK = 256
KV_RANK = 128
ATTN_V_DIM = 64
ATTN_WIDTH = N_ATTN_HEADS * ATTN_V_DIM
ATTN_SCALE = KV_RANK ** -0.5
IDX_HEADS = 8
IDX_DIM = 64
IDX_W_SCALE = (IDX_HEADS * IDX_DIM) ** -0.5
TOPK_MAX = 256
N_REC_HEADS = 8
REC_K_DIM = 64
REC_V_DIM = 64
REC_WIDTH = N_REC_HEADS * REC_V_DIM
N_GROUPS = 4
EXPERTS_PER_GROUP = 8
N_EXPERTS = N_GROUPS * EXPERTS_PER_GROUP
TOP_K_IN_GROUP = 2
D_EXPERT = 512
EPS = 1e-6
NEG = -1e30
TINY = 1e-30

LANES = 128
SUBLANES = 8
Q_BLOCK = 128
COUNT_ACCS = 8
ONES_ROWS = 16
LOG2E = 1.4426950408889634
SUB = 16
DIAG_GROUP = 4
VMEM_LIMIT = 56 * 1024 * 1024

F32 = jnp.float32
BF16 = jnp.bfloat16
I32 = jnp.int32


def _cparams(sem):
    return pltpu.CompilerParams(dimension_semantics=sem, vmem_limit_bytes=VMEM_LIMIT)


def _nt_dot(a, b):
    return lax.dot_general(a, b, (((1,), (1,)), ((), ())), preferred_element_type=F32)


def _tn_dot(a, b):
    return lax.dot_general(a, b, (((0,), (0,)), ((), ())), preferred_element_type=F32)


def _dot(a, b):
    return jnp.dot(a, b, preferred_element_type=F32)


def _split_dot(a_f32, b_hi, b_lo):
    a_hi = a_f32.astype(BF16)
    a_lo = (a_f32 - a_hi.astype(F32)).astype(BF16)
    return _dot(a_hi, b_hi) + (_dot(a_hi, b_lo) + _dot(a_lo, b_hi))


def _hi_lo(w):
    hi = w.astype(BF16)
    lo = (w - hi.astype(F32)).astype(BF16)
    return hi, lo


def _mod_kernel(c_ref, w_ref, b_ref, o_ref):
    c = c_ref[...]
    ca = c * jax.nn.sigmoid(c)
    w = w_ref[0]
    w_hi = w.astype(BF16)
    w_lo = (w - w_hi.astype(F32)).astype(BF16)
    o_ref[0] = _split_dot(ca, w_hi, w_lo) + b_ref[0]


def _modulation(c, w_mod, b_mod):
    L, D, D6 = w_mod.shape
    B = c.shape[0]
    tn = 1024
    return pl.pallas_call(
        _mod_kernel,
        out_shape=jax.ShapeDtypeStruct((L, B, D6), F32),
        grid=(L, D6 // tn),
        in_specs=[
            pl.BlockSpec((B, D), lambda l, n: (0, 0)),
            pl.BlockSpec((1, D, tn), lambda l, n: (l, 0, n)),
            pl.BlockSpec((1, 1, tn), lambda l, n: (l, 0, n)),
        ],
        out_specs=pl.BlockSpec((1, B, tn), lambda l, n: (l, 0, n)),
        compiler_params=_cparams(("arbitrary", "arbitrary")),
        name="modulation",
    )(c, w_mod, b_mod.reshape(L, 1, D6))


def _rms(x, eps=EPS):
    return x * lax.rsqrt(jnp.mean(x * x, axis=-1, keepdims=True) + eps)


def _proj_kernel(layer, x_ref, sh_ref, sc_ref, g1_ref, wa_ref, wrec_ref, wg_ref, gcq_ref, gckv_ref,
                 gk_ref, wq_ref, wiq_ref, lb_ref,
                 q_out, qi_out, kv_out, kvt_out, kidx_out, wt_out, qs_out, k_out, v_out, vt_out, lf_out, og_out,
                 sga_out, sgr_out):
    x = x_ref[0]
    tm = x.shape[0]
    h = (_rms(x) * g1_ref[...]) * (1.0 + sc_ref[0]) + sh_ref[0]
    hb = h.astype(BF16)

    ua = _dot(hb, wa_ref[...])
    cq = (_rms(ua[:, :Q_RANK]) * gcq_ref[...]).astype(BF16)
    q = (_dot(cq, wq_ref[...]) * (ATTN_SCALE * LOG2E)).astype(BF16)
    qi = _dot(cq, wiq_ref[...]).astype(BF16)
    for qb in range(tm // Q_BLOCK):
        rows = slice(qb * Q_BLOCK, (qb + 1) * Q_BLOCK)
        for hd in range(N_ATTN_HEADS):
            q_out[0, qb, hd * Q_BLOCK:(hd + 1) * Q_BLOCK, :] = q[rows, hd * LANES:(hd + 1) * LANES]
        for t in range(IDX_HEADS * IDX_DIM // LANES):
            qi_out[0, qb, t * Q_BLOCK:(t + 1) * Q_BLOCK, :] = qi[rows, t * LANES:(t + 1) * LANES]
    kvn = _rms(ua[:, Q_RANK:Q_RANK + KV_RANK]) * gckv_ref[...]
    kv_out[0] = kvn.astype(BF16)
    kvt_out[0] = kvn.T.astype(BF16)

    t3 = ua[:, Q_RANK + KV_RANK:]
    lane = lax.broadcasted_iota(I32, t3.shape, 1)
    ms = jnp.sum(jnp.where(lane < IDX_DIM, t3 * t3, 0.0), axis=-1, keepdims=True) * (1.0 / IDX_DIM)
    ka = t3 * lax.rsqrt(ms + EPS) * gk_ref[...]
    kb = pltpu.roll(ka, IDX_DIM, 1)
    kidx_out[0] = jnp.concatenate([ka, kb], axis=1).astype(BF16)
    wt_out[0] = (t3 * IDX_W_SCALE).T[IDX_DIM:IDX_DIM + IDX_HEADS, :]

    ur = _dot(hb, wrec_ref[...])
    kw = N_REC_HEADS * REC_K_DIM
    lbl = lb_ref[...]
    e = jnp.exp(lbl - jnp.max(lbl, axis=0, keepdims=True))
    p = e / jnp.sum(e, axis=0, keepdims=True)
    lb = jnp.zeros((1, kw), F32)
    for j in range(1, layer + 1):
        lb = lb + p[j:j + 1, :]
    lb = jnp.clip(lb, 0.0, 1.0)
    sig = jax.nn.sigmoid(ur[:, kw:2 * kw])
    f = lb + (1.0 - lb) * sig
    lf_out[0] = jnp.log(jnp.maximum(f, TINY))
    k_out[0] = ((1.0 - lb) * (1.0 - sig)).astype(BF16)
    qr = ur[:, :kw]
    qs_out[0] = (qr * jax.nn.sigmoid(qr)).astype(BF16)
    vr = ur[:, 2 * kw:2 * kw + REC_WIDTH]
    v_out[0] = vr.astype(BF16)
    vt_out[0] = vr.T.astype(BF16)
    og = ur[:, 2 * kw + REC_WIDTH:]
    og_out[0] = (og * jax.nn.sigmoid(og)).astype(BF16)

    ug = _dot(hb, wg_ref[...])
    D = x.shape[1]
    sga_out[0] = jax.nn.sigmoid(ug[:, :D]).astype(BF16)
    sgr_out[0] = jax.nn.sigmoid(ug[:, D:]).astype(BF16)


def _proj(layer, x, sh1, sc1, g1, wa, wrec, wg, gcq, gckv, gk, wq, wiq, lb_logits, tm):
    B, S, D = x.shape
    nqb = tm // Q_BLOCK
    nq = S // Q_BLOCK
    kw = N_REC_HEADS * REC_K_DIM
    c2 = lambda b, i: (0, 0)
    tok = lambda w: pl.BlockSpec((1, tm, w), lambda b, i: (b, i, 0))
    full = lambda a: pl.BlockSpec(a.shape, c2)
    out_shapes = (
        jax.ShapeDtypeStruct((B, nq, N_ATTN_HEADS * Q_BLOCK, KV_RANK), BF16),
        jax.ShapeDtypeStruct((B, nq, IDX_HEADS * IDX_DIM // LANES * Q_BLOCK, LANES), BF16),
        jax.ShapeDtypeStruct((B, S, KV_RANK), BF16),
        jax.ShapeDtypeStruct((B, KV_RANK, S), BF16),
        jax.ShapeDtypeStruct((B, S, 2 * LANES), BF16),
        jax.ShapeDtypeStruct((B, IDX_HEADS, S), F32),
        jax.ShapeDtypeStruct((B, S, kw), BF16),
        jax.ShapeDtypeStruct((B, S, kw), BF16),
        jax.ShapeDtypeStruct((B, S, REC_WIDTH), BF16),
        jax.ShapeDtypeStruct((B, REC_WIDTH, S), BF16),
        jax.ShapeDtypeStruct((B, S, kw), F32),
        jax.ShapeDtypeStruct((B, S, REC_WIDTH), BF16),
        jax.ShapeDtypeStruct((B, S, D), BF16),
        jax.ShapeDtypeStruct((B, S, D), BF16),
    )
    out_specs = (
        pl.BlockSpec((1, nqb, N_ATTN_HEADS * Q_BLOCK, KV_RANK), lambda b, i: (b, i, 0, 0)),
        pl.BlockSpec((1, nqb, IDX_HEADS * IDX_DIM // LANES * Q_BLOCK, LANES), lambda b, i: (b, i, 0, 0)),
        tok(KV_RANK),
        pl.BlockSpec((1, KV_RANK, tm), lambda b, i: (b, 0, i)),
        tok(2 * LANES),
        pl.BlockSpec((1, IDX_HEADS, tm), lambda b, i: (b, 0, i)),
        tok(kw), tok(kw), tok(REC_WIDTH),
        pl.BlockSpec((1, REC_WIDTH, tm), lambda b, i: (b, 0, i)),
        tok(kw), tok(REC_WIDTH),
        tok(D), tok(D),
    )
    vec = lambda: pl.BlockSpec((1, 1, D), lambda b, i: (b, 0, 0))
    return pl.pallas_call(
        functools.partial(_proj_kernel, layer),
        out_shape=out_shapes,
        grid=(B, S // tm),
        in_specs=[tok(D), vec(), vec(), full(g1), full(wa), full(wrec), full(wg), full(gcq), full(gckv),
                  full(gk), full(wq), full(wiq), full(lb_logits)],
        out_specs=out_specs,
        compiler_params=_cparams(("arbitrary", "arbitrary")),
        name="proj",
    )(x, sh1, sc1, g1, wa, wrec, wg, gcq, gckv, gk, wq, wiq, lb_logits)


def _sort_key(score):
    bits = lax.bitcast_convert_type(score, I32)
    return bits ^ (lax.shift_right_arithmetic(bits, 31) & jnp.int32(0x7FFFFFFF))


def _dsa_kernel(S, KC, KA, TOPK, q_ref, qi_ref, wt_ref, kv_ref, kvt_ref, kidx_ref, wvt_ref, ya_ref,
                keys_ref, acc_ref, m_ref, a_ref, p_ref):
    i = pl.program_id(1)
    n_ch = ((i + 1) * Q_BLOCK + KC - 1) // KC
    qpos = i * Q_BLOCK + lax.broadcasted_iota(I32, (1, Q_BLOCK), 1)
    wt = wt_ref[0]
    wrow = [wt[hd:hd + 1, :] for hd in range(IDX_HEADS)]
    qi = qi_ref[0, 0]
    n_t = IDX_HEADS * IDX_DIM // LANES

    def kpos_of(c):
        return c * KC + lax.broadcasted_iota(I32, (KC, 1), 0)

    def score_body(c, carry):
        koff = pl.multiple_of(c * KC, KC)
        la = _nt_dot(kidx_ref[0, pl.ds(koff, KC), 0:LANES], qi)
        lb = _nt_dot(kidx_ref[0, pl.ds(koff, KC), LANES:2 * LANES], qi)
        score = jnp.zeros((KC, Q_BLOCK), F32)
        for t in range(n_t):
            lanes = slice(t * Q_BLOCK, (t + 1) * Q_BLOCK)
            score = score + wrow[2 * t] * jnp.maximum(la[:, lanes], 0.0)
            score = score + wrow[2 * t + 1] * jnp.maximum(lb[:, lanes], 0.0)
        score = jnp.where(kpos_of(c) <= qpos, score, NEG)
        keys_ref[pl.ds(koff, KC), :] = _sort_key(score)
        return carry

    lax.fori_loop(0, n_ch, score_body, 0)

    def count(pred):
        def body(c, accs):
            kk = keys_ref[pl.ds(pl.multiple_of(c * KC, KC), KC), :]
            accs = list(accs)
            for r in range(KC // SUBLANES):
                a = accs[r % COUNT_ACCS]
                accs[r % COUNT_ACCS] = jnp.where(pred(kk[r * SUBLANES:(r + 1) * SUBLANES]), a + 1, a)
            return tuple(accs)
        accs = lax.fori_loop(0, n_ch, body, tuple(jnp.zeros((SUBLANES, Q_BLOCK), I32) for _ in range(COUNT_ACCS)))
        return jnp.sum(functools.reduce(lambda x, y: x + y, accs), axis=0, keepdims=True)

    def bis_body(it, v):
        cand = v + lax.shift_left(jnp.int32(1), 31 - it)
        cnt = count(lambda kk: kk >= cand)
        return jnp.where(cnt >= TOPK, cand, v)

    v = lax.fori_loop(0, 32, bis_body, jnp.full((1, Q_BLOCK), -2 ** 31, I32))
    n_gt = count(lambda kk: kk > v)
    n_ge = count(lambda kk: kk >= v)
    need = (TOPK - n_gt).astype(F32)
    any_split = jnp.max(jnp.where(n_ge > TOPK, 1, 0)) > 0

    @pl.when(any_split)
    def _():
        tri = jnp.where(lax.broadcasted_iota(I32, (KC, KC), 1) <= lax.broadcasted_iota(I32, (KC, KC), 0),
                        1.0, 0.0).astype(BF16)

        def tie_body(c, seen):
            koff = pl.multiple_of(c * KC, KC)
            kk = keys_ref[pl.ds(koff, KC), :]
            tie = kk == v
            tie_f = jnp.where(tie, 1.0, 0.0)
            rank = _dot(tri, tie_f.astype(BF16)) + seen
            keys_ref[pl.ds(koff, KC), :] = jnp.where(tie & (rank > need), v - 1, kk)
            return seen + jnp.sum(tie_f, axis=0, keepdims=True)

        lax.fori_loop(0, n_ch, tie_body, jnp.zeros((1, Q_BLOCK), F32))

    m_ref[...] = jnp.full(m_ref.shape, NEG, F32)
    acc_ref[...] = jnp.zeros(acc_ref.shape, F32)
    ones_rows = jnp.ones((ONES_ROWS, KA), BF16)

    def att_body(c, carry):
        koff = pl.multiple_of(c * KA, KA)
        kvc = kv_ref[0, pl.ds(koff, KA), :]
        kpos = c * KA + lax.broadcasted_iota(I32, (KA, 1), 0)
        msk = (keys_ref[pl.ds(koff, KA), :] >= v) & (kpos <= qpos)
        for hd in range(N_ATTN_HEADS):
            lanes = slice(hd * Q_BLOCK, (hd + 1) * Q_BLOCK)
            s = jnp.where(msk, _nt_dot(kvc, q_ref[0, 0, hd * Q_BLOCK:(hd + 1) * Q_BLOCK, :]), -jnp.inf)
            m_prev = m_ref[:, lanes]
            m_next = jnp.maximum(m_prev, jnp.max(s, axis=0, keepdims=True))
            a_ref[:, lanes] = jnp.exp2(m_prev - m_next)
            m_ref[:, lanes] = m_next
            p_ref[:, lanes] = jnp.exp2(s - m_next).astype(BF16)
        kvt1 = jnp.concatenate([kvt_ref[0, :, pl.ds(koff, KA)], ones_rows], axis=0)
        acc_ref[...] = acc_ref[...] * a_ref[...] + _dot(kvt1, p_ref[...])
        return carry

    lax.fori_loop(0, ((i + 1) * Q_BLOCK + KA - 1) // KA, att_body, 0)
    o = (acc_ref[0:KV_RANK, :] / acc_ref[KV_RANK:KV_RANK + 1, :]).astype(BF16)
    ya_t = jnp.concatenate([_dot(wvt_ref[hd], o[:, hd * Q_BLOCK:(hd + 1) * Q_BLOCK])
                            for hd in range(N_ATTN_HEADS)], axis=0)
    ya_ref[0] = ya_t.T.astype(BF16)


def _dsa(q4, qi4, wt, kv, kvt, kidx2, wvt):
    B, nq = q4.shape[0], q4.shape[1]
    S = kv.shape[1]
    KC = min(512, S)
    KA = min(512, S)
    topk = min(TOPK_MAX, S // 4)
    assert S % KC == 0 and KC % KA == 0 and KA % Q_BLOCK == 0
    hq = N_ATTN_HEADS * Q_BLOCK
    return pl.pallas_call(
        functools.partial(_dsa_kernel, S, KC, KA, topk),
        out_shape=jax.ShapeDtypeStruct((B, S, ATTN_WIDTH), BF16),
        grid=(B, nq),
        in_specs=[
            pl.BlockSpec((1, 1) + q4.shape[2:], lambda b, i: (b, i, 0, 0)),
            pl.BlockSpec((1, 1) + qi4.shape[2:], lambda b, i: (b, i, 0, 0)),
            pl.BlockSpec((1, IDX_HEADS, Q_BLOCK), lambda b, i: (b, 0, i)),
            pl.BlockSpec((1, S, KV_RANK), lambda b, i: (b, 0, 0)),
            pl.BlockSpec((1, KV_RANK, S), lambda b, i: (b, 0, 0)),
            pl.BlockSpec((1, S, 2 * LANES), lambda b, i: (b, 0, 0)),
            pl.BlockSpec(wvt.shape, lambda b, i: (0, 0, 0)),
        ],
        out_specs=pl.BlockSpec((1, Q_BLOCK, ATTN_WIDTH), lambda b, i: (b, i, 0)),
        scratch_shapes=[
            pltpu.VMEM((S, Q_BLOCK), I32),
            pltpu.VMEM((KV_RANK + ONES_ROWS, hq), F32),
            pltpu.VMEM((1, hq), F32),
            pltpu.VMEM((1, hq), F32),
            pltpu.VMEM((KA, hq), BF16),
        ],
        compiler_params=_cparams(("arbitrary", "arbitrary")),
        name="dsa",
    )(q4, qi4, wt, kv, kvt, kidx2, wvt)


def _hgrn_kernel(TS, qs_ref, k_ref, v_ref, vt_ref, lf_ref, og_ref, grec_ref, y_ref,
                 st_ref, aloc_ref, e_ref, qh_ref, kh_ref, o_ref, g_ref, sb_ref, kblk_ref):
    n_tiles = REC_WIDTH // LANES
    nsb = TS // SUB

    @pl.when(pl.program_id(1) == 0)
    def _():
        st_ref[...] = jnp.zeros(st_ref.shape, F32)

    a = lf_ref[0]
    r = lax.broadcasted_iota(I32, a.shape, 0) % SUB
    for sh in (1, 2, 4, 8):
        a = a + jnp.where(r >= sh, pltpu.roll(a, sh, 0), 0.0)
    aend = jnp.where(r == SUB - 1, a, 0.0)
    for sh in (1, 2, 4, 8):
        aend = aend + jnp.where(r + sh <= SUB - 1, pltpu.roll(aend, TS - sh, 0), 0.0)
    e = jnp.exp(a)
    aloc_ref[...] = a
    e_ref[...] = e
    qh_ref[...] = (qs_ref[0].astype(F32) * e).astype(BF16)
    kh_ref[...] = (k_ref[0].astype(F32) * jnp.exp(aend - a)).astype(BF16)

    half = REC_K_DIM
    li = lax.broadcasted_iota(I32, (LANES, LANES), 0) // half
    lj = lax.broadcasted_iota(I32, (LANES, LANES), 1) // half
    bd_f = jnp.where(li == lj, 1.0, 0.0)
    bd_b = bd_f.astype(BF16)
    gsel = jnp.where(lax.broadcasted_iota(I32, (SUB, SUB * SUB), 0)
                     == lax.broadcasted_iota(I32, (SUB, SUB * SUB), 1) // SUB, 1.0, 0.0).astype(BF16)
    srow = lax.broadcasted_iota(I32, (SUB, LANES), 0)

    def diag_body(g, carry):
        units = []
        for jj in range(DIAG_GROUP):
            rows = pl.ds(pl.multiple_of((g * DIAG_GROUP + jj) * SUB, SUB), SUB)
            for p in range(n_tiles):
                units.append((rows, slice(p * LANES, (p + 1) * LANES)))
        wsts = []
        for rows, lanes in units:
            a_blk = aloc_ref[rows, lanes]
            q_blk = qs_ref[0, rows, lanes].astype(F32)
            k_blk = k_ref[0, rows, lanes].astype(F32)
            ws = []
            for t in range(SUB):
                d = a_blk[t:t + 1, :] - a_blk
                w = jnp.exp(jnp.where(srow <= t, d, -jnp.inf)) * (k_blk * q_blk[t:t + 1, :])
                ws.append(w.astype(BF16))
            wsts.append(jnp.concatenate(ws, axis=0))
        rexps = [_dot(wst, bd_b) for wst in wsts]
        m2s = []
        for (rows, lanes), rexp in zip(units, rexps):
            v_f = v_ref[0, rows, lanes].astype(F32)
            m2s.append((rexp * jnp.concatenate([v_f] * SUB, axis=0)).astype(BF16))
        outs = [_dot(gsel, m2) for m2 in m2s]
        for (rows, lanes), out in zip(units, outs):
            o_ref[rows, lanes] = out
        return carry

    lax.fori_loop(0, nsb // DIAG_GROUP, diag_body, 0)

    @pl.when((pl.program_id(0) == 0) & (pl.program_id(1) == 0))
    def _():
        kblk_ref[...] = jnp.zeros(kblk_ref.shape, BF16)

    for p in range(n_tiles):
        lanes = slice(p * LANES, (p + 1) * LANES)
        for j in range(nsb):
            kblk_ref[j * SUB:(j + 1) * SUB, j * LANES:(j + 1) * LANES] = kh_ref[j * SUB:(j + 1) * SUB, lanes]
        g_ref[p] = _dot(vt_ref[0, lanes, :], kblk_ref[...])

    for p in range(n_tiles):
        lanes = slice(p * LANES, (p + 1) * LANES)
        st = st_ref[p]
        for j in range(nsb):
            sb_ref[j, p] = st.astype(BF16)
            dec = e_ref[(j + 1) * SUB - 1:(j + 1) * SUB, lanes]
            st = st * dec + g_ref[p, :, j * LANES:(j + 1) * LANES] * bd_f
        st_ref[p] = st

    for j in range(nsb):
        rows = slice(j * SUB, (j + 1) * SUB)
        for p in range(n_tiles):
            lanes = slice(p * LANES, (p + 1) * LANES)
            o_ref[rows, lanes] = o_ref[rows, lanes] + _nt_dot(qh_ref[rows, lanes], sb_ref[j, p])

    o = o_ref[...]
    o2 = o * o
    hi = o2.astype(BF16)
    lo = (o2 - hi.astype(F32)).astype(BF16)
    ms = jnp.concatenate(
        [_dot(hi[:, p * LANES:(p + 1) * LANES], bd_b) + _dot(lo[:, p * LANES:(p + 1) * LANES], bd_b)
         for p in range(n_tiles)], axis=1) * (1.0 / REC_V_DIM)
    y = o * lax.rsqrt(ms + EPS) * grec_ref[...]
    y_ref[0] = (y * og_ref[0].astype(F32)).astype(BF16)


def _hgrn(qs, k, v, vt, lf, og, grec_t, ts):
    B, S, W = qs.shape
    nsb = ts // SUB
    n_tiles = W // LANES
    tok = lambda: pl.BlockSpec((1, ts, W), lambda b, i: (b, i, 0))
    return pl.pallas_call(
        functools.partial(_hgrn_kernel, ts),
        out_shape=jax.ShapeDtypeStruct((B, S, W), BF16),
        grid=(B, S // ts),
        in_specs=[tok(), tok(), tok(), pl.BlockSpec((1, W, ts), lambda b, i: (b, 0, i)), tok(), tok(),
                  pl.BlockSpec((1, W), lambda b, i: (0, 0))],
        out_specs=tok(),
        scratch_shapes=[
            pltpu.VMEM((n_tiles, LANES, LANES), F32),
            pltpu.VMEM((ts, W), F32),
            pltpu.VMEM((ts, W), F32),
            pltpu.VMEM((ts, W), BF16),
            pltpu.VMEM((ts, W), BF16),
            pltpu.VMEM((ts, W), F32),
            pltpu.VMEM((n_tiles, LANES, nsb * LANES), F32),
            pltpu.VMEM((nsb, n_tiles, LANES, LANES), BF16),
            pltpu.VMEM((ts, nsb * LANES), BF16),
        ],
        compiler_params=_cparams(("arbitrary", "arbitrary")),
        name="hgrn",
    )(qs, k, v, vt, lf, og, grec_t)


def _merge_kernel(x_ref, ya_ref, yr_ref, sga_ref, sgr_ref, gt1_ref, sh2_ref, sc2_ref, g2_ref,
                  wba_ref, wbr_ref, wo_ref, wrh_ref, wrl_ref, br_ref, x1_out, h2_out, rt_out, cnt_out):
    merged = (sga_ref[0].astype(F32) * _dot(ya_ref[0], wba_ref[...])
              + sgr_ref[0].astype(F32) * _dot(yr_ref[0], wbr_ref[...]))
    x1 = x_ref[0] + gt1_ref[0] * _dot(merged.astype(BF16), wo_ref[...])
    x1_out[0] = x1
    h2 = (_rms(x1) * g2_ref[...]) * (1.0 + sc2_ref[0]) + sh2_ref[0]
    _to_token_tiles(h2_out.at[0], h2)

    lg = _split_dot(h2, wrh_ref[...], wrl_ref[...]) + br_ref[...]
    lane = lax.broadcasted_iota(I32, lg.shape, 1)
    big = jnp.int32(1 << 20)
    gl = jnp.where(lane < N_GROUPS, lg, -jnp.inf)
    gmax = jnp.max(gl, axis=1, keepdims=True)
    gsel = jnp.min(jnp.where(gl == gmax, lane, big), axis=1, keepdims=True)
    ggate = 1.0 / jnp.sum(jnp.exp(gl - gmax), axis=1, keepdims=True)
    lo = N_GROUPS + EXPERTS_PER_GROUP * gsel
    el = jnp.where((lane >= lo) & (lane < lo + EXPERTS_PER_GROUP), lg, -jnp.inf)
    v1 = jnp.max(el, axis=1, keepdims=True)
    i1 = jnp.min(jnp.where(el == v1, lane, big), axis=1, keepdims=True)
    el2 = jnp.where(lane == i1, -jnp.inf, el)
    v2 = jnp.max(el2, axis=1, keepdims=True)
    i2 = jnp.min(jnp.where(el2 == v2, lane, big), axis=1, keepdims=True)
    e2 = jnp.exp(v2 - v1)
    den = 1.0 + e2
    g1 = (1.0 / den) * ggate
    g2 = (e2 / den) * ggate
    tm = lg.shape[0]
    oh1 = jnp.where(lane == i1 - N_GROUPS, 1.0, 0.0)
    oh2 = jnp.where(lane == i2 - N_GROUPS, 1.0, 0.0)
    ohs = (oh1 + oh2).astype(BF16)
    ltri = jnp.where(lax.broadcasted_iota(I32, (tm, tm), 1) < lax.broadcasted_iota(I32, (tm, tm), 0),
                     1.0, 0.0).astype(BF16)
    before = _dot(ltri, ohs)
    r1 = jnp.sum(before * oh1, axis=1, keepdims=True)
    r2 = jnp.sum(before * oh2, axis=1, keepdims=True)
    cnt_out[0, 0] = jnp.broadcast_to(jnp.sum(oh1 + oh2, axis=0, keepdims=True), (SUBLANES, LANES))
    vals = ((i1 - N_GROUPS).astype(F32), (i2 - N_GROUPS).astype(F32), g1, g2, r1, r2)
    rt = jnp.zeros(lg.shape, F32)
    for j, val in enumerate(vals):
        rt = jnp.where(lane == j, val, rt)
    rt_out[0] = rt


def _merge(x, ya, yr, sga, sgr, gt1, sh2, sc2, g2, wba, wbr, wo, wrh, wrl, br, tm):
    B, S, D = x.shape
    tok = lambda w: pl.BlockSpec((1, tm, w), lambda b, i: (b, i, 0))
    vec = lambda: pl.BlockSpec((1, 1, D), lambda b, i: (b, 0, 0))
    full = lambda a: pl.BlockSpec(a.shape, lambda b, i: (0,) * a.ndim)
    return pl.pallas_call(
        _merge_kernel,
        out_shape=(jax.ShapeDtypeStruct((B, S, D), F32), jax.ShapeDtypeStruct((B, S * SUBLANES, LANES), F32),
                   jax.ShapeDtypeStruct((B, S, LANES), F32),
                   jax.ShapeDtypeStruct((B, S // tm, SUBLANES, LANES), F32)),
        grid=(B, S // tm),
        in_specs=[tok(D), tok(ATTN_WIDTH), tok(REC_WIDTH), tok(D), tok(D), vec(), vec(), vec(), full(g2),
                  full(wba), full(wbr), full(wo), full(wrh), full(wrl), full(br)],
        out_specs=(tok(D), pl.BlockSpec((1, tm * SUBLANES, LANES), lambda b, i: (b, i, 0)), tok(LANES),
                   pl.BlockSpec((1, 1, SUBLANES, LANES), lambda b, i: (b, i, 0, 0))),
        compiler_params=_cparams(("arbitrary", "arbitrary")),
        name="merge",
    )(x, ya, yr, sga, sgr, gt1, sh2, sc2, g2, wba, wbr, wo, wrh, wrl, br)


def _to_token_tiles(ref, x):
    n = x.shape[0]
    for s in range(SUBLANES):
        ref[pl.ds(s, n, stride=SUBLANES), :] = x[:, s * LANES:(s + 1) * LANES]


def _from_token_tiles(ref, first_tile, n):
    return jnp.concatenate([ref[pl.ds(first_tile * SUBLANES + s, n, stride=SUBLANES), :]
                            for s in range(SUBLANES)], axis=1)


def _tile_gather(src_hbm, idx_ref, dst, sem, n):
    def body(r, carry):
        src = src_hbm.at[pl.ds(pl.multiple_of(idx_ref[0, 0, r] * SUBLANES, SUBLANES), SUBLANES)]
        pltpu.make_async_copy(src, dst.at[pl.ds(pl.multiple_of(r * SUBLANES, SUBLANES), SUBLANES)], sem).start()
        return carry
    lax.fori_loop(0, n, body, 0, unroll=8)


def _tile_gather_static(src_hbm, idx_ref, dst, sem, lo, hi):
    for r in range(lo, hi):
        src = src_hbm.at[pl.ds(pl.multiple_of(idx_ref[0, 0, r] * SUBLANES, SUBLANES), SUBLANES)]
        pltpu.make_async_copy(src, dst.at[pl.ds(r * SUBLANES, SUBLANES)], sem).start()


def _tile_gather_wait(src_hbm, dst, sem, n):
    pltpu.make_async_copy(src_hbm.at[pl.ds(0, n * SUBLANES)], dst, sem).wait()


def _moe_kernel(BLK, be_ref, tok_ref, tokn_ref, h2_hbm, wg_ref, wu_ref, wd_ref, y_ref,
                xbuf, sem, wgb, wub, wdb):
    i = pl.program_id(0)
    n = pl.num_programs(0)
    slot = lax.rem(i, 2)

    @pl.when((i == 0) | (be_ref[i] != be_ref[jnp.maximum(i - 1, 0)]))
    def _():
        wgb[...] = wg_ref[0, 0].astype(BF16)
        wub[...] = wu_ref[0, 0].astype(BF16)
        wdb[...] = wd_ref[0, 0].astype(BF16)

    @pl.when(i == 0)
    def _():
        _tile_gather(h2_hbm, tok_ref, xbuf.at[0], sem.at[0], BLK)

    _tile_gather_wait(h2_hbm, xbuf.at[slot], sem.at[slot], BLK)
    xb = _from_token_tiles(xbuf.at[slot], 0, BLK).astype(BF16)
    nxt, nsem, part = xbuf.at[1 - slot], sem.at[1 - slot], BLK // 4
    _tile_gather_static(h2_hbm, tokn_ref, nxt, nsem, 0, part)
    g = _dot(xb, wgb[...])
    _tile_gather_static(h2_hbm, tokn_ref, nxt, nsem, part, 2 * part)
    u = _dot(xb, wub[...])
    _tile_gather_static(h2_hbm, tokn_ref, nxt, nsem, 2 * part, 3 * part)
    y = _dot(((g * jax.nn.sigmoid(g)) * u).astype(BF16), wdb[...])
    _tile_gather_static(h2_hbm, tokn_ref, nxt, nsem, 3 * part, BLK)
    _to_token_tiles(y_ref, y)

    @pl.when(i == n - 1)
    def _():
        _tile_gather_wait(h2_hbm, nxt, nsem, BLK)


def _moe(layer, blk_expert, buf_tok3, h2t, wgate, wup, wdown, blk):
    n_blk = buf_tok3.shape[0]
    D = wgate.shape[2]
    grid_spec = pltpu.PrefetchScalarGridSpec(
        num_scalar_prefetch=1,
        grid=(n_blk,),
        in_specs=[
            pl.BlockSpec((1, 1, blk), lambda i, be: (i, 0, 0), memory_space=pltpu.SMEM),
            pl.BlockSpec((1, 1, blk), lambda i, be: (jnp.minimum(i + 1, n_blk - 1), 0, 0),
                         memory_space=pltpu.SMEM),
            pl.BlockSpec(memory_space=pl.ANY),
            pl.BlockSpec((1, 1, D, D_EXPERT), lambda i, be: (layer, be[i], 0, 0)),
            pl.BlockSpec((1, 1, D, D_EXPERT), lambda i, be: (layer, be[i], 0, 0)),
            pl.BlockSpec((1, 1, D_EXPERT, D), lambda i, be: (layer, be[i], 0, 0)),
        ],
        out_specs=pl.BlockSpec((blk * SUBLANES, LANES), lambda i, be: (i, 0)),
        scratch_shapes=[pltpu.VMEM((2, blk * SUBLANES, LANES), F32), pltpu.SemaphoreType.DMA((2,)),
                        pltpu.VMEM((D, D_EXPERT), BF16), pltpu.VMEM((D, D_EXPERT), BF16),
                        pltpu.VMEM((D_EXPERT, D), BF16)],
    )
    return pl.pallas_call(
        functools.partial(_moe_kernel, blk),
        out_shape=jax.ShapeDtypeStruct((n_blk * blk * SUBLANES, LANES), F32),
        grid_spec=grid_spec,
        compiler_params=_cparams(("arbitrary",)),
        name="moe",
    )(blk_expert, buf_tok3, buf_tok3, h2t, wgate, wup, wdown)


def _comb_kernel(TM, final, d_ref, dn_ref, x1_ref, rt_ref, gt2_ref, gf_ref, ys_hbm, o_ref, buf, sem):
    i = pl.program_id(0)
    n = pl.num_programs(0)
    slot = lax.rem(i, 2)

    @pl.when(i == 0)
    def _():
        _tile_gather(ys_hbm, d_ref, buf.at[0], sem.at[0], 2 * TM)

    @pl.when(i + 1 < n)
    def _():
        _tile_gather(ys_hbm, dn_ref, buf.at[1 - slot], sem.at[1 - slot], 2 * TM)

    _tile_gather_wait(ys_hbm, buf.at[slot], sem.at[slot], 2 * TM)
    rt = rt_ref[...]
    y0 = _from_token_tiles(buf.at[slot], 0, TM)
    y1 = _from_token_tiles(buf.at[slot], TM, TM)
    mix = rt[:, TOP_K_IN_GROUP:TOP_K_IN_GROUP + 1] * y0 + rt[:, TOP_K_IN_GROUP + 1:TOP_K_IN_GROUP + 2] * y1
    x2 = x1_ref[...] + gt2_ref[0] * mix
    if final:
        x2 = _rms(x2) * gf_ref[...]
    o_ref[...] = x2


def _combine(dest3, x1, route, gt2, gf, ys, S, tm, final):
    T, D = x1.shape
    n = T // tm
    per_b = S // tm
    return pl.pallas_call(
        functools.partial(_comb_kernel, tm, final),
        out_shape=jax.ShapeDtypeStruct((T, D), F32),
        grid=(n,),
        in_specs=[
            pl.BlockSpec((1, 1, 2 * tm), lambda i: (i, 0, 0), memory_space=pltpu.SMEM),
            pl.BlockSpec((1, 1, 2 * tm), lambda i: (jnp.minimum(i + 1, n - 1), 0, 0), memory_space=pltpu.SMEM),
            pl.BlockSpec((tm, D), lambda i: (i, 0)),
            pl.BlockSpec((tm, LANES), lambda i: (i, 0)),
            pl.BlockSpec((1, 1, D), lambda i: (i // per_b, 0, 0)),
            pl.BlockSpec((1, D), lambda i: (0, 0)),
            pl.BlockSpec(memory_space=pl.ANY),
        ],
        out_specs=pl.BlockSpec((tm, D), lambda i: (i, 0)),
        scratch_shapes=[pltpu.VMEM((2, 2 * tm * SUBLANES, LANES), F32), pltpu.SemaphoreType.DMA((2,))],
        compiler_params=_cparams(("arbitrary",)),
        name="combine",
    )(dest3, dest3, x1, route, gt2, gf, ys)


def _dispatch(route, tile_cnt, blk):
    T = route.shape[0]
    N = T * TOP_K_IN_GROUP
    n_tiles = tile_cnt.shape[0]
    eid = route[:, 0:TOP_K_IN_GROUP].astype(I32).reshape(N)
    rank_in_tile = route[:, 2 * TOP_K_IN_GROUP:3 * TOP_K_IN_GROUP].astype(I32).reshape(N)
    tile_start = jnp.cumsum(tile_cnt, axis=0) - tile_cnt
    counts = jnp.sum(tile_cnt, axis=0)
    tile_of = jnp.repeat(jnp.arange(n_tiles, dtype=I32), N // n_tiles)
    padded = (counts + blk - 1) // blk * blk
    pad_end = jnp.cumsum(padded)
    pad_start = pad_end - padded
    dest = pad_start[eid] + tile_start.reshape(-1)[tile_of * N_EXPERTS + eid] + rank_in_tile
    P = N + N_EXPERTS * blk
    n_blk = P // blk
    token_id = jnp.repeat(jnp.arange(T, dtype=I32), TOP_K_IN_GROUP)
    buf_tok = jnp.zeros((P,), I32).at[dest].set(token_id)
    blk_expert = jnp.minimum(jnp.searchsorted(pad_end, jnp.arange(n_blk, dtype=I32) * blk, side="right"),
                             N_EXPERTS - 1).astype(I32)
    return blk_expert, buf_tok.reshape(n_blk, 1, blk), dest.reshape(T, TOP_K_IN_GROUP)


def kernel(x, c, w_mod, b_mod, g_norm1, g_norm2, w_in, g_cq, g_ckv, g_kidx, w_q_up, w_idx_q, w_v_up,
           lb_logits, g_rec, w_branch_a, w_branch_r, w_out, w_grp, b_grp, w_exp_router, b_exp_router,
           w_gate, w_up, w_down, g_final):
    B, S, D = x.shape
    L = w_mod.shape[0]
    T = B * S
    tm = min(256, S)
    ts = min(256, S)
    tmc = min(128, S)
    blk = 256
    kw = N_REC_HEADS * REC_K_DIM
    na = Q_RANK + KV_RANK + IDX_DIM + IDX_HEADS
    n_rec = 2 * kw + 2 * REC_WIDTH

    mod = _modulation(c, w_mod, b_mod)
    for l in range(L):
        m6 = mod[l].reshape(B, 6, 1, D)
        sh1, sc1, gt1, sh2, sc2, gt2 = (m6[:, j] for j in range(6))
        wa = jnp.pad(w_in[l, :, :na], ((0, 0), (0, 512 - na))).astype(BF16)
        wrec = w_in[l, :, na:na + n_rec].astype(BF16)
        wg = w_in[l, :, na + n_rec:].astype(BF16)
        gk = jnp.pad(g_kidx[l], (0, LANES - IDX_DIM)).reshape(1, LANES)
        (q4, qi4, kv, kvt, kidx2, wt, qs, kk, vv, vt, lf, og, sga, sgr) = _proj(
            l, x, sh1, sc1, g_norm1[l].reshape(1, D), wa, wrec, wg, g_cq[l].reshape(1, -1),
            g_ckv[l].reshape(1, -1), gk, w_q_up[l].astype(BF16), w_idx_q[l].astype(BF16), lb_logits, tm)
        wvt = jnp.swapaxes(w_v_up[l], 1, 2).astype(BF16)
        ya = _dsa(q4, qi4, wt, kv, kvt, kidx2, wvt)
        yr = _hgrn(qs, kk, vv, vt, lf, og, jnp.tile(g_rec[l], N_REC_HEADS).reshape(1, REC_WIDTH), ts)
        wr = jnp.pad(jnp.concatenate([w_grp[l], w_exp_router[l]], axis=1),
                     ((0, 0), (0, LANES - N_GROUPS - N_EXPERTS)))
        wrh, wrl = _hi_lo(wr)
        br = jnp.pad(jnp.concatenate([b_grp[l], b_exp_router[l]]), (0, LANES - N_GROUPS - N_EXPERTS)).reshape(1, LANES)
        x1, h2t, route, tile_cnt = _merge(x, ya, yr, sga, sgr, gt1, sh2, sc2, g_norm2[l].reshape(1, D),
                               w_branch_a[l].astype(BF16), w_branch_r[l].astype(BF16), w_out[l].astype(BF16),
                               wrh, wrl, br, tm)
        route = route.reshape(T, LANES)
        tile_cnt = tile_cnt[:, :, 0, :N_EXPERTS].reshape(-1, N_EXPERTS).astype(I32)
        blk_expert, buf_tok3, dest = _dispatch(route, tile_cnt, blk)
        ys = _moe(l, blk_expert, buf_tok3, h2t.reshape(T * SUBLANES, LANES), w_gate, w_up, w_down, blk)
        dest3 = dest.reshape(T // tmc, tmc, TOP_K_IN_GROUP).transpose(0, 2, 1).reshape(T // tmc, 1, 2 * tmc)
        x = _combine(dest3, x1.reshape(T, D), route, gt2, g_final.reshape(1, D), ys, S, tmc, l == L - 1).reshape(B, S, D)
    return x
```

```python
import functools

import jax
import jax.numpy as jnp
import numpy as np
from jax import lax
from jax.experimental import pallas as pl
from jax.experimental.pallas import tpu as pltpu

N_ATTN_HEADS = 8
Q_RANK = 256
KV_RANK = 128
ATTN_V_DIM = 64
ATTN_WIDTH = N_ATTN_HEADS * ATTN_V_DIM
ATTN_SCALE = KV_RANK ** -0.5
IDX_HEADS = 8
IDX_DIM = 64
IDX_W_SCALE = (IDX_HEADS * IDX_DIM) ** -0.5
TOPK_MAX = 256
N_REC_HEADS = 8
REC_K_DIM = 64
REC_V_DIM = 64
REC_WIDTH = N_REC_HEADS * REC_V_DIM
N_GROUPS = 4
EXPERTS_PER_GROUP = 8
N_EXPERTS = N_GROUPS * EXPERTS_PER_GROUP
TOP_K_IN_GROUP = 2
D_EXPERT = 512
EPS = 1e-6
NEG = -1e30
TINY = 1e-30

LANES = 128
SUBLANES = 8
Q_BLOCK = 128
COUNT_ACCS = 8
ONES_ROWS = 16
LOG2E = 1.4426950408889634
SUB = 16
DIAG_GROUP = 4
VMEM_LIMIT = 56 * 1024 * 1024

F32 = jnp.float32
BF16 = jnp.bfloat16
I32 = jnp.int32


def _cparams(sem):
    return pltpu.CompilerParams(dimension_semantics=sem, vmem_limit_bytes=VMEM_LIMIT)


def _nt_dot(a, b):
    return lax.dot_general(a, b, (((1,), (1,)), ((), ())), preferred_element_type=F32)


def _tn_dot(a, b):
    return lax.dot_general(a, b, (((0,), (0,)), ((), ())), preferred_element_type=F32)


def _dot(a, b):
    return jnp.dot(a, b, preferred_element_type=F32)


def _split_dot(a_f32, b_hi, b_lo):
    a_hi = a_f32.astype(BF16)
    a_lo = (a_f32 - a_hi.astype(F32)).astype(BF16)
    return _dot(a_hi, b_hi) + (_dot(a_hi, b_lo) + _dot(a_lo, b_hi))


def _hi_lo(w):
    hi = w.astype(BF16)
    lo = (w - hi.astype(F32)).astype(BF16)
    return hi, lo


def _mod_kernel(c_ref, w_ref, b_ref, o_ref):
    c = c_ref[...]
    ca = c * jax.nn.sigmoid(c)
    w = w_ref[0]
    w_hi = w.astype(BF16)
    w_lo = (w - w_hi.astype(F32)).astype(BF16)
    o_ref[0] = _split_dot(ca, w_hi, w_lo) + b_ref[0]


def _modulation(c, w_mod, b_mod):
    L, D, D6 = w_mod.shape
    B = c.shape[0]
    tn = 1024
    return pl.pallas_call(
        _mod_kernel,
        out_shape=jax.ShapeDtypeStruct((L, B, D6), F32),
        grid=(L, D6 // tn),
        in_specs=[
            pl.BlockSpec((B, D), lambda l, n: (0, 0)),
            pl.BlockSpec((1, D, tn), lambda l, n: (l, 0, n)),
            pl.BlockSpec((1, 1, tn), lambda l, n: (l, 0, n)),
        ],
        out_specs=pl.BlockSpec((1, B, tn), lambda l, n: (l, 0, n)),
        compiler_params=_cparams(("arbitrary", "arbitrary")),
        name="modulation",
    )(c, w_mod, b_mod.reshape(L, 1, D6))


def _rms(x, eps=EPS):
    return x * lax.rsqrt(jnp.mean(x * x, axis=-1, keepdims=True) + eps)


def _proj_kernel(layer, x_ref, sh_ref, sc_ref, g1_ref, wa_ref, wrec_ref, wg_ref, gcq_ref, gckv_ref,
                 gk_ref, wq_ref, wiq_ref, lb_ref,
                 q_out, qi_out, kv_out, kvt_out, kidx_out, wt_out, qs_out, k_out, v_out, vt_out, lf_out, og_out,
                 sga_out, sgr_out):
    x = x_ref[0]
    tm = x.shape[0]
    h = (_rms(x) * g1_ref[...]) * (1.0 + sc_ref[0]) + sh_ref[0]
    hb = h.astype(BF16)

    ua = _dot(hb, wa_ref[...])
    cq = (_rms(ua[:, :Q_RANK]) * gcq_ref[...]).astype(BF16)
    q = (_dot(cq, wq_ref[...]) * (ATTN_SCALE * LOG2E)).astype(BF16)
    qi = _dot(cq, wiq_ref[...]).astype(BF16)
    for qb in range(tm // Q_BLOCK):
        rows = slice(qb * Q_BLOCK, (qb + 1) * Q_BLOCK)
        for hd in range(N_ATTN_HEADS):
            q_out[0, qb, hd * Q_BLOCK:(hd + 1) * Q_BLOCK, :] = q[rows, hd * LANES:(hd + 1) * LANES]
        for t in range(IDX_HEADS * IDX_DIM // LANES):
            qi_out[0, qb, t * Q_BLOCK:(t + 1) * Q_BLOCK, :] = qi[rows, t * LANES:(t + 1) * LANES]
    kvn = _rms(ua[:, Q_RANK:Q_RANK + KV_RANK]) * gckv_ref[...]
    kv_out[0] = kvn.astype(BF16)
    kvt_out[0] = kvn.T.astype(BF16)

    t3 = ua[:, Q_RANK + KV_RANK:]
    lane = lax.broadcasted_iota(I32, t3.shape, 1)
    ms = jnp.sum(jnp.where(lane < IDX_DIM, t3 * t3, 0.0), axis=-1, keepdims=True) * (1.0 / IDX_DIM)
    ka = t3 * lax.rsqrt(ms + EPS) * gk_ref[...]
    kb = pltpu.roll(ka, IDX_DIM, 1)
    kidx_out[0] = jnp.concatenate([ka, kb], axis=1).astype(BF16)
    wt_out[0] = (t3 * IDX_W_SCALE).T[IDX_DIM:IDX_DIM + IDX_HEADS, :]

    ur = _dot(hb, wrec_ref[...])
    kw = N_REC_HEADS * REC_K_DIM
    lbl = lb_ref[...]
    e = jnp.exp(lbl - jnp.max(lbl, axis=0, keepdims=True))
    p = e / jnp.sum(e, axis=0, keepdims=True)
    lb = jnp.zeros((1, kw), F32)
    for j in range(1, layer + 1):
        lb = lb + p[j:j + 1, :]
    lb = jnp.clip(lb, 0.0, 1.0)
    sig = jax.nn.sigmoid(ur[:, kw:2 * kw])
    f = lb + (1.0 - lb) * sig
    lf_out[0] = jnp.log(jnp.maximum(f, TINY))
    k_out[0] = ((1.0 - lb) * (1.0 - sig)).astype(BF16)
    qr = ur[:, :kw]
    qs_out[0] = (qr * jax.nn.sigmoid(qr)).astype(BF16)
    vr = ur[:, 2 * kw:2 * kw + REC_WIDTH]
    v_out[0] = vr.astype(BF16)
    vt_out[0] = vr.T.astype(BF16)
    og = ur[:, 2 * kw + REC_WIDTH:]
    og_out[0] = (og * jax.nn.sigmoid(og)).astype(BF16)

    ug = _dot(hb, wg_ref[...])
    D = x.shape[1]
    sga_out[0] = jax.nn.sigmoid(ug[:, :D]).astype(BF16)
    sgr_out[0] = jax.nn.sigmoid(ug[:, D:]).astype(BF16)


def _proj(layer, x, sh1, sc1, g1, wa, wrec, wg, gcq, gckv, gk, wq, wiq, lb_logits, tm):
    B, S, D = x.shape
    nqb = tm // Q_BLOCK
    nq = S // Q_BLOCK
    kw = N_REC_HEADS * REC_K_DIM
    c2 = lambda b, i: (0, 0)
    tok = lambda w: pl.BlockSpec((1, tm, w), lambda b, i: (b, i, 0))
    full = lambda a: pl.BlockSpec(a.shape, c2)
    out_shapes = (
        jax.ShapeDtypeStruct((B, nq, N_ATTN_HEADS * Q_BLOCK, KV_RANK), BF16),
        jax.ShapeDtypeStruct((B, nq, IDX_HEADS * IDX_DIM // LANES * Q_BLOCK, LANES), BF16),
        jax.ShapeDtypeStruct((B, S, KV_RANK), BF16),
        jax.ShapeDtypeStruct((B, KV_RANK, S), BF16),
        jax.ShapeDtypeStruct((B, S, 2 * LANES), BF16),
        jax.ShapeDtypeStruct((B, IDX_HEADS, S), F32),
        jax.ShapeDtypeStruct((B, S, kw), BF16),
        jax.ShapeDtypeStruct((B, S, kw), BF16),
        jax.ShapeDtypeStruct((B, S, REC_WIDTH), BF16),
        jax.ShapeDtypeStruct((B, REC_WIDTH, S), BF16),
        jax.ShapeDtypeStruct((B, S, kw), F32),
        jax.ShapeDtypeStruct((B, S, REC_WIDTH), BF16),
        jax.ShapeDtypeStruct((B, S, D), BF16),
        jax.ShapeDtypeStruct((B, S, D), BF16),
    )
    out_specs = (
        pl.BlockSpec((1, nqb, N_ATTN_HEADS * Q_BLOCK, KV_RANK), lambda b, i: (b, i, 0, 0)),
        pl.BlockSpec((1, nqb, IDX_HEADS * IDX_DIM // LANES * Q_BLOCK, LANES), lambda b, i: (b, i, 0, 0)),
        tok(KV_RANK),
        pl.BlockSpec((1, KV_RANK, tm), lambda b, i: (b, 0, i)),
        tok(2 * LANES),
        pl.BlockSpec((1, IDX_HEADS, tm), lambda b, i: (b, 0, i)),
        tok(kw), tok(kw), tok(REC_WIDTH),
        pl.BlockSpec((1, REC_WIDTH, tm), lambda b, i: (b, 0, i)),
        tok(kw), tok(REC_WIDTH),
        tok(D), tok(D),
    )
    vec = lambda: pl.BlockSpec((1, 1, D), lambda b, i: (b, 0, 0))
    return pl.pallas_call(
        functools.partial(_proj_kernel, layer),
        out_shape=out_shapes,
        grid=(B, S // tm),
        in_specs=[tok(D), vec(), vec(), full(g1), full(wa), full(wrec), full(wg), full(gcq), full(gckv),
                  full(gk), full(wq), full(wiq), full(lb_logits)],
        out_specs=out_specs,
        compiler_params=_cparams(("arbitrary", "arbitrary")),
        name="proj",
    )(x, sh1, sc1, g1, wa, wrec, wg, gcq, gckv, gk, wq, wiq, lb_logits)


def _sort_key(score):
    bits = lax.bitcast_convert_type(score, I32)
    return bits ^ (lax.shift_right_arithmetic(bits, 31) & jnp.int32(0x7FFFFFFF))


def _dsa_kernel(S, KC, KA, TOPK, q_ref, qi_ref, wt_ref, kv_ref, kvt_ref, kidx_ref, wvt_ref, ya_ref,
                keys_ref, acc_ref, m_ref, a_ref, p_ref):
    i = pl.program_id(1)
    n_ch = ((i + 1) * Q_BLOCK + KC - 1) // KC
    qpos = i * Q_BLOCK + lax.broadcasted_iota(I32, (1, Q_BLOCK), 1)
    wt = wt_ref[0]
    wrow = [wt[hd:hd + 1, :] for hd in range(IDX_HEADS)]
    qi = qi_ref[0, 0]
    n_t = IDX_HEADS * IDX_DIM // LANES

    def kpos_of(c):
        return c * KC + lax.broadcasted_iota(I32, (KC, 1), 0)

    def score_body(c, carry):
        koff = pl.multiple_of(c * KC, KC)
        la = _nt_dot(kidx_ref[0, pl.ds(koff, KC), 0:LANES], qi)
        lb = _nt_dot(kidx_ref[0, pl.ds(koff, KC), LANES:2 * LANES], qi)
        score = jnp.zeros((KC, Q_BLOCK), F32)
        for t in range(n_t):
            lanes = slice(t * Q_BLOCK, (t + 1) * Q_BLOCK)
            score = score + wrow[2 * t] * jnp.maximum(la[:, lanes], 0.0)
            score = score + wrow[2 * t + 1] * jnp.maximum(lb[:, lanes], 0.0)
        score = jnp.where(kpos_of(c) <= qpos, score, NEG)
        keys_ref[pl.ds(koff, KC), :] = _sort_key(score)
        return carry

    lax.fori_loop(0, n_ch, score_body, 0)

    def count(pred):
        def body(c, accs):
            kk = keys_ref[pl.ds(pl.multiple_of(c * KC, KC), KC), :]
            accs = list(accs)
            for r in range(KC // SUBLANES):
                a = accs[r % COUNT_ACCS]
                accs[r % COUNT_ACCS] = jnp.where(pred(kk[r * SUBLANES:(r + 1) * SUBLANES]), a + 1, a)
            return tuple(accs)
        accs = lax.fori_loop(0, n_ch, body, tuple(jnp.zeros((SUBLANES, Q_BLOCK), I32) for _ in range(COUNT_ACCS)))
        return jnp.sum(functools.reduce(lambda x, y: x + y, accs), axis=0, keepdims=True)

    def bis_body(it, v):
        cand = v + lax.shift_left(jnp.int32(1), 31 - it)
        cnt = count(lambda kk: kk >= cand)
        return jnp.where(cnt >= TOPK, cand, v)

    v = lax.fori_loop(0, 32, bis_body, jnp.full((1, Q_BLOCK), -2 ** 31, I32))
    n_gt = count(lambda kk: kk > v)
    n_ge = count(lambda kk: kk >= v)
    need = (TOPK - n_gt).astype(F32)
    any_split = jnp.max(jnp.where(n_ge > TOPK, 1, 0)) > 0

    @pl.when(any_split)
    def _():
        tri = jnp.where(lax.broadcasted_iota(I32, (KC, KC), 1) <= lax.broadcasted_iota(I32, (KC, KC), 0),
                        1.0, 0.0).astype(BF16)

        def tie_body(c, seen):
            koff = pl.multiple_of(c * KC, KC)
            kk = keys_ref[pl.ds(koff, KC), :]
            tie = kk == v
            tie_f = jnp.where(tie, 1.0, 0.0)
            rank = _dot(tri, tie_f.astype(BF16)) + seen
            keys_ref[pl.ds(koff, KC), :] = jnp.where(tie & (rank > need), v - 1, kk)
            return seen + jnp.sum(tie_f, axis=0, keepdims=True)

        lax.fori_loop(0, n_ch, tie_body, jnp.zeros((1, Q_BLOCK), F32))

    m_ref[...] = jnp.full(m_ref.shape, NEG, F32)
    acc_ref[...] = jnp.zeros(acc_ref.shape, F32)
    ones_rows = jnp.ones((ONES_ROWS, KA), BF16)

    def att_body(c, carry):
        koff = pl.multiple_of(c * KA, KA)
        kvc = kv_ref[0, pl.ds(koff, KA), :]
        kpos = c * KA + lax.broadcasted_iota(I32, (KA, 1), 0)
        msk = (keys_ref[pl.ds(koff, KA), :] >= v) & (kpos <= qpos)
        for hd in range(N_ATTN_HEADS):
            lanes = slice(hd * Q_BLOCK, (hd + 1) * Q_BLOCK)
            s = jnp.where(msk, _nt_dot(kvc, q_ref[0, 0, hd * Q_BLOCK:(hd + 1) * Q_BLOCK, :]), -jnp.inf)
            m_prev = m_ref[:, lanes]
            m_next = jnp.maximum(m_prev, jnp.max(s, axis=0, keepdims=True))
            a_ref[:, lanes] = jnp.exp2(m_prev - m_next)
            m_ref[:, lanes] = m_next
            p_ref[:, lanes] = jnp.exp2(s - m_next).astype(BF16)
        kvt1 = jnp.concatenate([kvt_ref[0, :, pl.ds(koff, KA)], ones_rows], axis=0)
        acc_ref[...] = acc_ref[...] * a_ref[...] + _dot(kvt1, p_ref[...])
        return carry

    lax.fori_loop(0, ((i + 1) * Q_BLOCK + KA - 1) // KA, att_body, 0)
    o = (acc_ref[0:KV_RANK, :] / acc_ref[KV_RANK:KV_RANK + 1, :]).astype(BF16)
    ya_t = jnp.concatenate([_dot(wvt_ref[hd], o[:, hd * Q_BLOCK:(hd + 1) * Q_BLOCK])
                            for hd in range(N_ATTN_HEADS)], axis=0)
    ya_ref[0] = ya_t.T.astype(BF16)


def _dsa(q4, qi4, wt, kv, kvt, kidx2, wvt):
    B, nq = q4.shape[0], q4.shape[1]
    S = kv.shape[1]
    KC = min(512, S)
    KA = min(512, S)
    topk = min(TOPK_MAX, S // 4)
    assert S % KC == 0 and KC % KA == 0 and KA % Q_BLOCK == 0
    hq = N_ATTN_HEADS * Q_BLOCK
    return pl.pallas_call(
        functools.partial(_dsa_kernel, S, KC, KA, topk),
        out_shape=jax.ShapeDtypeStruct((B, S, ATTN_WIDTH), BF16),
        grid=(B, nq),
        in_specs=[
            pl.BlockSpec((1, 1) + q4.shape[2:], lambda b, i: (b, i, 0, 0)),
            pl.BlockSpec((1, 1) + qi4.shape[2:], lambda b, i: (b, i, 0, 0)),
            pl.BlockSpec((1, IDX_HEADS, Q_BLOCK), lambda b, i: (b, 0, i)),
            pl.BlockSpec((1, S, KV_RANK), lambda b, i: (b, 0, 0)),
            pl.BlockSpec((1, KV_RANK, S), lambda b, i: (b, 0, 0)),
            pl.BlockSpec((1, S, 2 * LANES), lambda b, i: (b, 0, 0)),
            pl.BlockSpec(wvt.shape, lambda b, i: (0, 0, 0)),
        ],
        out_specs=pl.BlockSpec((1, Q_BLOCK, ATTN_WIDTH), lambda b, i: (b, i, 0)),
        scratch_shapes=[
            pltpu.VMEM((S, Q_BLOCK), I32),
            pltpu.VMEM((KV_RANK + ONES_ROWS, hq), F32),
            pltpu.VMEM((1, hq), F32),
            pltpu.VMEM((1, hq), F32),
            pltpu.VMEM((KA, hq), BF16),
        ],
        compiler_params=_cparams(("arbitrary", "arbitrary")),
        name="dsa",
    )(q4, qi4, wt, kv, kvt, kidx2, wvt)


def _hgrn_kernel(TS, qs_ref, k_ref, v_ref, vt_ref, lf_ref, og_ref, grec_ref, y_ref,
                 st_ref, aloc_ref, e_ref, qh_ref, kh_ref, o_ref, g_ref, sb_ref, kblk_ref):
    n_tiles = REC_WIDTH // LANES
    nsb = TS // SUB

    @pl.when(pl.program_id(1) == 0)
    def _():
        st_ref[...] = jnp.zeros(st_ref.shape, F32)

    a = lf_ref[0]
    r = lax.broadcasted_iota(I32, a.shape, 0) % SUB
    for sh in (1, 2, 4, 8):
        a = a + jnp.where(r >= sh, pltpu.roll(a, sh, 0), 0.0)
    aend = jnp.where(r == SUB - 1, a, 0.0)
    for sh in (1, 2, 4, 8):
        aend = aend + jnp.where(r + sh <= SUB - 1, pltpu.roll(aend, TS - sh, 0), 0.0)
    e = jnp.exp(a)
    aloc_ref[...] = a
    e_ref[...] = e
    qh_ref[...] = (qs_ref[0].astype(F32) * e).astype(BF16)
    kh_ref[...] = (k_ref[0].astype(F32) * jnp.exp(aend - a)).astype(BF16)

    half = REC_K_DIM
    li = lax.broadcasted_iota(I32, (LANES, LANES), 0) // half
    lj = lax.broadcasted_iota(I32, (LANES, LANES), 1) // half
    bd_f = jnp.where(li == lj, 1.0, 0.0)
    bd_b = bd_f.astype(BF16)
    gsel = jnp.where(lax.broadcasted_iota(I32, (SUB, SUB * SUB), 0)
                     == lax.broadcasted_iota(I32, (SUB, SUB * SUB), 1) // SUB, 1.0, 0.0).astype(BF16)
    srow = lax.broadcasted_iota(I32, (SUB, LANES), 0)

    def diag_body(g, carry):
        units = []
        for jj in range(DIAG_GROUP):
            rows = pl.ds(pl.multiple_of((g * DIAG_GROUP + jj) * SUB, SUB), SUB)
            for p in range(n_tiles):
                units.append((rows, slice(p * LANES, (p + 1) * LANES)))
        wsts = []
        for rows, lanes in units:
            a_blk = aloc_ref[rows, lanes]
            q_blk = qs_ref[0, rows, lanes].astype(F32)
            k_blk = k_ref[0, rows, lanes].astype(F32)
            ws = []
            for t in range(SUB):
                d = a_blk[t:t + 1, :] - a_blk
                w = jnp.exp(jnp.where(srow <= t, d, -jnp.inf)) * (k_blk * q_blk[t:t + 1, :])
                ws.append(w.astype(BF16))
            wsts.append(jnp.concatenate(ws, axis=0))
        rexps = [_dot(wst, bd_b) for wst in wsts]
        m2s = []
        for (rows, lanes), rexp in zip(units, rexps):
            v_f = v_ref[0, rows, lanes].astype(F32)
            m2s.append((rexp * jnp.concatenate([v_f] * SUB, axis=0)).astype(BF16))
        outs = [_dot(gsel, m2) for m2 in m2s]
        for (rows, lanes), out in zip(units, outs):
            o_ref[rows, lanes] = out
        return carry

    lax.fori_loop(0, nsb // DIAG_GROUP, diag_body, 0)

    @pl.when((pl.program_id(0) == 0) & (pl.program_id(1) == 0))
    def _():
        kblk_ref[...] = jnp.zeros(kblk_ref.shape, BF16)

    for p in range(n_tiles):
        lanes = slice(p * LANES, (p + 1) * LANES)
        for j in range(nsb):
            kblk_ref[j * SUB:(j + 1) * SUB, j * LANES:(j + 1) * LANES] = kh_ref[j * SUB:(j + 1) * SUB, lanes]
        g_ref[p] = _dot(vt_ref[0, lanes, :], kblk_ref[...])

    for p in range(n_tiles):
        lanes = slice(p * LANES, (p + 1) * LANES)
        st = st_ref[p]
        for j in range(nsb):
            sb_ref[j, p] = st.astype(BF16)
            dec = e_ref[(j + 1) * SUB - 1:(j + 1) * SUB, lanes]
            st = st * dec + g_ref[p, :, j * LANES:(j + 1) * LANES] * bd_f
        st_ref[p] = st

    for j in range(nsb):
        rows = slice(j * SUB, (j + 1) * SUB)
        for p in range(n_tiles):
            lanes = slice(p * LANES, (p + 1) * LANES)
            o_ref[rows, lanes] = o_ref[rows, lanes] + _nt_dot(qh_ref[rows, lanes], sb_ref[j, p])

    o = o_ref[...]
    o2 = o * o
    hi = o2.astype(BF16)
    lo = (o2 - hi.astype(F32)).astype(BF16)
    ms = jnp.concatenate(
        [_dot(hi[:, p * LANES:(p + 1) * LANES], bd_b) + _dot(lo[:, p * LANES:(p + 1) * LANES], bd_b)
         for p in range(n_tiles)], axis=1) * (1.0 / REC_V_DIM)
    y = o * lax.rsqrt(ms + EPS) * grec_ref[...]
    y_ref[0] = (y * og_ref[0].astype(F32)).astype(BF16)


def _hgrn(qs, k, v, vt, lf, og, grec_t, ts):
    B, S, W = qs.shape
    nsb = ts // SUB
    n_tiles = W // LANES
    tok = lambda: pl.BlockSpec((1, ts, W), lambda b, i: (b, i, 0))
    return pl.pallas_call(
        functools.partial(_hgrn_kernel, ts),
        out_shape=jax.ShapeDtypeStruct((B, S, W), BF16),
        grid=(B, S // ts),
        in_specs=[tok(), tok(), tok(), pl.BlockSpec((1, W, ts), lambda b, i: (b, 0, i)), tok(), tok(),
                  pl.BlockSpec((1, W), lambda b, i: (0, 0))],
        out_specs=tok(),
        scratch_shapes=[
            pltpu.VMEM((n_tiles, LANES, LANES), F32),
            pltpu.VMEM((ts, W), F32),
            pltpu.VMEM((ts, W), F32),
            pltpu.VMEM((ts, W), BF16),
            pltpu.VMEM((ts, W), BF16),
            pltpu.VMEM((ts, W), F32),
            pltpu.VMEM((n_tiles, LANES, nsb * LANES), F32),
            pltpu.VMEM((nsb, n_tiles, LANES, LANES), BF16),
            pltpu.VMEM((ts, nsb * LANES), BF16),
        ],
        compiler_params=_cparams(("arbitrary", "arbitrary")),
        name="hgrn",
    )(qs, k, v, vt, lf, og, grec_t)


def _merge_kernel(x_ref, ya_ref, yr_ref, sga_ref, sgr_ref, gt1_ref, sh2_ref, sc2_ref, g2_ref,
                  wba_ref, wbr_ref, wo_ref, wrh_ref, wrl_ref, br_ref, x1_out, h2_out, rt_out, cnt_out):
    merged = (sga_ref[0].astype(F32) * _dot(ya_ref[0], wba_ref[...])
              + sgr_ref[0].astype(F32) * _dot(yr_ref[0], wbr_ref[...]))
    x1 = x_ref[0] + gt1_ref[0] * _dot(merged.astype(BF16), wo_ref[...])
    x1_out[0] = x1
    h2 = (_rms(x1) * g2_ref[...]) * (1.0 + sc2_ref[0]) + sh2_ref[0]
    _to_token_tiles(h2_out.at[0], h2)

    lg = _split_dot(h2, wrh_ref[...], wrl_ref[...]) + br_ref[...]
    lane = lax.broadcasted_iota(I32, lg.shape, 1)
    big = jnp.int32(1 << 20)
    gl = jnp.where(lane < N_GROUPS, lg, -jnp.inf)
    gmax = jnp.max(gl, axis=1, keepdims=True)
    gsel = jnp.min(jnp.where(gl == gmax, lane, big), axis=1, keepdims=True)
    ggate = 1.0 / jnp.sum(jnp.exp(gl - gmax), axis=1, keepdims=True)
    lo = N_GROUPS + EXPERTS_PER_GROUP * gsel
    el = jnp.where((lane >= lo) & (lane < lo + EXPERTS_PER_GROUP), lg, -jnp.inf)
    v1 = jnp.max(el, axis=1, keepdims=True)
    i1 = jnp.min(jnp.where(el == v1, lane, big), axis=1, keepdims=True)
    el2 = jnp.where(lane == i1, -jnp.inf, el)
    v2 = jnp.max(el2, axis=1, keepdims=True)
    i2 = jnp.min(jnp.where(el2 == v2, lane, big), axis=1, keepdims=True)
    e2 = jnp.exp(v2 - v1)
    den = 1.0 + e2
    g1 = (1.0 / den) * ggate
    g2 = (e2 / den) * ggate
    tm = lg.shape[0]
    oh1 = jnp.where(lane == i1 - N_GROUPS, 1.0, 0.0)
    oh2 = jnp.where(lane == i2 - N_GROUPS, 1.0, 0.0)
    ohs = (oh1 + oh2).astype(BF16)
    ltri = jnp.where(lax.broadcasted_iota(I32, (tm, tm), 1) < lax.broadcasted_iota(I32, (tm, tm), 0),
                     1.0, 0.0).astype(BF16)
    before = _dot(ltri, ohs)
    r1 = jnp.sum(before * oh1, axis=1, keepdims=True)
    r2 = jnp.sum(before * oh2, axis=1, keepdims=True)
    cnt_out[0, 0] = jnp.broadcast_to(jnp.sum(oh1 + oh2, axis=0, keepdims=True), (SUBLANES, LANES))
    vals = ((i1 - N_GROUPS).astype(F32), (i2 - N_GROUPS).astype(F32), g1, g2, r1, r2)
    rt = jnp.zeros(lg.shape, F32)
    for j, val in enumerate(vals):
        rt = jnp.where(lane == j, val, rt)
    rt_out[0] = rt


def _merge(x, ya, yr, sga, sgr, gt1, sh2, sc2, g2, wba, wbr, wo, wrh, wrl, br, tm):
    B, S, D = x.shape
    tok = lambda w: pl.BlockSpec((1, tm, w), lambda b, i: (b, i, 0))
    vec = lambda: pl.BlockSpec((1, 1, D), lambda b, i: (b, 0, 0))
    full = lambda a: pl.BlockSpec(a.shape, lambda b, i: (0,) * a.ndim)
    return pl.pallas_call(
        _merge_kernel,
        out_shape=(jax.ShapeDtypeStruct((B, S, D), F32), jax.ShapeDtypeStruct((B, S * SUBLANES, LANES), F32),
                   jax.ShapeDtypeStruct((B, S, LANES), F32),
                   jax.ShapeDtypeStruct((B, S // tm, SUBLANES, LANES), F32)),
        grid=(B, S // tm),
        in_specs=[tok(D), tok(ATTN_WIDTH), tok(REC_WIDTH), tok(D), tok(D), vec(), vec(), vec(), full(g2),
                  full(wba), full(wbr), full(wo), full(wrh), full(wrl), full(br)],
        out_specs=(tok(D), pl.BlockSpec((1, tm * SUBLANES, LANES), lambda b, i: (b, i, 0)), tok(LANES),
                   pl.BlockSpec((1, 1, SUBLANES, LANES), lambda b, i: (b, i, 0, 0))),
        compiler_params=_cparams(("arbitrary", "arbitrary")),
        name="merge",
    )(x, ya, yr, sga, sgr, gt1, sh2, sc2, g2, wba, wbr, wo, wrh, wrl, br)


def _to_token_tiles(ref, x):
    n = x.shape[0]
    for s in range(SUBLANES):
        ref[pl.ds(s, n, stride=SUBLANES), :] = x[:, s * LANES:(s + 1) * LANES]


def _from_token_tiles(ref, first_tile, n):
    return jnp.concatenate([ref[pl.ds(first_tile * SUBLANES + s, n, stride=SUBLANES), :]
                            for s in range(SUBLANES)], axis=1)


def _tile_gather(src_hbm, idx_ref, dst, sem, n):
    def body(r, carry):
        src = src_hbm.at[pl.ds(pl.multiple_of(idx_ref[0, 0, r] * SUBLANES, SUBLANES), SUBLANES)]
        pltpu.make_async_copy(src, dst.at[pl.ds(pl.multiple_of(r * SUBLANES, SUBLANES), SUBLANES)], sem).start()
        return carry
    lax.fori_loop(0, n, body, 0, unroll=8)


def _tile_gather_wait(src_hbm, dst, sem, n):
    pltpu.make_async_copy(src_hbm.at[pl.ds(0, n * SUBLANES)], dst, sem).wait()


def _moe_kernel(BLK, be_ref, tok_ref, tokn_ref, h2_hbm, wg_ref, wu_ref, wd_ref, y_ref,
                xbuf, sem, wgb, wub, wdb):
    i = pl.program_id(0)
    n = pl.num_programs(0)
    slot = lax.rem(i, 2)

    @pl.when((i == 0) | (be_ref[i] != be_ref[jnp.maximum(i - 1, 0)]))
    def _():
        wgb[...] = wg_ref[0, 0].astype(BF16)
        wub[...] = wu_ref[0, 0].astype(BF16)
        wdb[...] = wd_ref[0, 0].astype(BF16)

    @pl.when(i == 0)
    def _():
        _tile_gather(h2_hbm, tok_ref, xbuf.at[0], sem.at[0], BLK)

    @pl.when(i + 1 < n)
    def _():
        _tile_gather(h2_hbm, tokn_ref, xbuf.at[1 - slot], sem.at[1 - slot], BLK)

    _tile_gather_wait(h2_hbm, xbuf.at[slot], sem.at[slot], BLK)
    xb = _from_token_tiles(xbuf.at[slot], 0, BLK).astype(BF16)
    g = _dot(xb, wgb[...])
    hid = (g * jax.nn.sigmoid(g)) * _dot(xb, wub[...])
    _to_token_tiles(y_ref, _dot(hid.astype(BF16), wdb[...]))


def _moe(layer, blk_expert, buf_tok3, h2t, wgate, wup, wdown, blk):
    n_blk = buf_tok3.shape[0]
    D = wgate.shape[2]
    grid_spec = pltpu.PrefetchScalarGridSpec(
        num_scalar_prefetch=1,
        grid=(n_blk,),
        in_specs=[
            pl.BlockSpec((1, 1, blk), lambda i, be: (i, 0, 0), memory_space=pltpu.SMEM),
            pl.BlockSpec((1, 1, blk), lambda i, be: (jnp.minimum(i + 1, n_blk - 1), 0, 0),
                         memory_space=pltpu.SMEM),
            pl.BlockSpec(memory_space=pl.ANY),
            pl.BlockSpec((1, 1, D, D_EXPERT), lambda i, be: (layer, be[i], 0, 0)),
            pl.BlockSpec((1, 1, D, D_EXPERT), lambda i, be: (layer, be[i], 0, 0)),
            pl.BlockSpec((1, 1, D_EXPERT, D), lambda i, be: (layer, be[i], 0, 0)),
        ],
        out_specs=pl.BlockSpec((blk * SUBLANES, LANES), lambda i, be: (i, 0)),
        scratch_shapes=[pltpu.VMEM((2, blk * SUBLANES, LANES), F32), pltpu.SemaphoreType.DMA((2,)),
                        pltpu.VMEM((D, D_EXPERT), BF16), pltpu.VMEM((D, D_EXPERT), BF16),
                        pltpu.VMEM((D_EXPERT, D), BF16)],
    )
    return pl.pallas_call(
        functools.partial(_moe_kernel, blk),
        out_shape=jax.ShapeDtypeStruct((n_blk * blk * SUBLANES, LANES), F32),
        grid_spec=grid_spec,
        compiler_params=_cparams(("arbitrary",)),
        name="moe",
    )(blk_expert, buf_tok3, buf_tok3, h2t, wgate, wup, wdown)


def _comb_kernel(TM, final, d_ref, dn_ref, x1_ref, rt_ref, gt2_ref, gf_ref, ys_hbm, o_ref, buf, sem):
    i = pl.program_id(0)
    n = pl.num_programs(0)
    slot = lax.rem(i, 2)

    @pl.when(i == 0)
    def _():
        _tile_gather(ys_hbm, d_ref, buf.at[0], sem.at[0], 2 * TM)

    @pl.when(i + 1 < n)
    def _():
        _tile_gather(ys_hbm, dn_ref, buf.at[1 - slot], sem.at[1 - slot], 2 * TM)

    _tile_gather_wait(ys_hbm, buf.at[slot], sem.at[slot], 2 * TM)
    rt = rt_ref[...]
    y0 = _from_token_tiles(buf.at[slot], 0, TM)
    y1 = _from_token_tiles(buf.at[slot], TM, TM)
    mix = rt[:, TOP_K_IN_GROUP:TOP_K_IN_GROUP + 1] * y0 + rt[:, TOP_K_IN_GROUP + 1:TOP_K_IN_GROUP + 2] * y1
    x2 = x1_ref[...] + gt2_ref[0] * mix
    if final:
        x2 = _rms(x2) * gf_ref[...]
    o_ref[...] = x2


def _combine(dest3, x1, route, gt2, gf, ys, S, tm, final):
    T, D = x1.shape
    n = T // tm
    per_b = S // tm
    return pl.pallas_call(
        functools.partial(_comb_kernel, tm, final),
        out_shape=jax.ShapeDtypeStruct((T, D), F32),
        grid=(n,),
        in_specs=[
            pl.BlockSpec((1, 1, 2 * tm), lambda i: (i, 0, 0), memory_space=pltpu.SMEM),
            pl.BlockSpec((1, 1, 2 * tm), lambda i: (jnp.minimum(i + 1, n - 1), 0, 0), memory_space=pltpu.SMEM),
            pl.BlockSpec((tm, D), lambda i: (i, 0)),
            pl.BlockSpec((tm, LANES), lambda i: (i, 0)),
            pl.BlockSpec((1, 1, D), lambda i: (i // per_b, 0, 0)),
            pl.BlockSpec((1, D), lambda i: (0, 0)),
            pl.BlockSpec(memory_space=pl.ANY),
        ],
        out_specs=pl.BlockSpec((tm, D), lambda i: (i, 0)),
        scratch_shapes=[pltpu.VMEM((2, 2 * tm * SUBLANES, LANES), F32), pltpu.SemaphoreType.DMA((2,))],
        compiler_params=_cparams(("arbitrary",)),
        name="combine",
    )(dest3, dest3, x1, route, gt2, gf, ys)


def _dispatch(route, tile_cnt, blk):
    T = route.shape[0]
    N = T * TOP_K_IN_GROUP
    n_tiles = tile_cnt.shape[0]
    eid = route[:, 0:TOP_K_IN_GROUP].astype(I32).reshape(N)
    rank_in_tile = route[:, 2 * TOP_K_IN_GROUP:3 * TOP_K_IN_GROUP].astype(I32).reshape(N)
    tile_start = jnp.cumsum(tile_cnt, axis=0) - tile_cnt
    counts = jnp.sum(tile_cnt, axis=0)
    padded = (counts + blk - 1) // blk * blk
    pad_end = jnp.cumsum(padded)
    pad_start = pad_end - padded
    base = jnp.repeat(pad_start[None, :] + tile_start, N // n_tiles, axis=0)
    hit = eid[:, None] == jnp.arange(N_EXPERTS, dtype=I32)[None, :]
    dest = jnp.sum(jnp.where(hit, base, 0), axis=1) + rank_in_tile
    P = N + N_EXPERTS * blk
    n_blk = P // blk
    token_id = jnp.repeat(jnp.arange(T, dtype=I32), TOP_K_IN_GROUP)
    buf_tok = jnp.zeros((P,), I32).at[dest].set(token_id)
    blk_expert = jnp.minimum(jnp.searchsorted(pad_end, jnp.arange(n_blk, dtype=I32) * blk, side="right"),
                             N_EXPERTS - 1).astype(I32)
    return blk_expert, buf_tok.reshape(n_blk, 1, blk), dest.reshape(T, TOP_K_IN_GROUP)


def kernel(x, c, w_mod, b_mod, g_norm1, g_norm2, w_in, g_cq, g_ckv, g_kidx, w_q_up, w_idx_q, w_v_up,
           lb_logits, g_rec, w_branch_a, w_branch_r, w_out, w_grp, b_grp, w_exp_router, b_exp_router,
           w_gate, w_up, w_down, g_final):
    B, S, D = x.shape
    L = w_mod.shape[0]
    T = B * S
    tm = min(256, S)
    ts = min(256, S)
    tmc = min(128, S)
    blk = 256
    kw = N_REC_HEADS * REC_K_DIM
    na = Q_RANK + KV_RANK + IDX_DIM + IDX_HEADS
    n_rec = 2 * kw + 2 * REC_WIDTH

    mod = _modulation(c, w_mod, b_mod)
    for l in range(L):
        m6 = mod[l].reshape(B, 6, 1, D)
        sh1, sc1, gt1, sh2, sc2, gt2 = (m6[:, j] for j in range(6))
        wa = jnp.pad(w_in[l, :, :na], ((0, 0), (0, 512 - na))).astype(BF16)
        wrec = w_in[l, :, na:na + n_rec].astype(BF16)
        wg = w_in[l, :, na + n_rec:].astype(BF16)
        gk = jnp.pad(g_kidx[l], (0, LANES - IDX_DIM)).reshape(1, LANES)
        (q4, qi4, kv, kvt, kidx2, wt, qs, kk, vv, vt, lf, og, sga, sgr) = _proj(
            l, x, sh1, sc1, g_norm1[l].reshape(1, D), wa, wrec, wg, g_cq[l].reshape(1, -1),
            g_ckv[l].reshape(1, -1), gk, w_q_up[l].astype(BF16), w_idx_q[l].astype(BF16), lb_logits, tm)
        wvt = jnp.swapaxes(w_v_up[l], 1, 2).astype(BF16)
        ya = _dsa(q4, qi4, wt, kv, kvt, kidx2, wvt)
        yr = _hgrn(qs, kk, vv, vt, lf, og, jnp.tile(g_rec[l], N_REC_HEADS).reshape(1, REC_WIDTH), ts)
        wr = jnp.pad(jnp.concatenate([w_grp[l], w_exp_router[l]], axis=1),
                     ((0, 0), (0, LANES - N_GROUPS - N_EXPERTS)))
        wrh, wrl = _hi_lo(wr)
        br = jnp.pad(jnp.concatenate([b_grp[l], b_exp_router[l]]), (0, LANES - N_GROUPS - N_EXPERTS)).reshape(1, LANES)
        x1, h2t, route, tile_cnt = _merge(x, ya, yr, sga, sgr, gt1, sh2, sc2, g_norm2[l].reshape(1, D),
                               w_branch_a[l].astype(BF16), w_branch_r[l].astype(BF16), w_out[l].astype(BF16),
                               wrh, wrl, br, tm)
        route = route.reshape(T, LANES)
        tile_cnt = tile_cnt[:, :, 0, :N_EXPERTS].reshape(-1, N_EXPERTS).astype(I32)
        blk_expert, buf_tok3, dest = _dispatch(route, tile_cnt, blk)
        ys = _moe(l, blk_expert, buf_tok3, h2t.reshape(T * SUBLANES, LANES), w_gate, w_up, w_down, blk)
        dest3 = dest.reshape(T // tmc, tmc, TOP_K_IN_GROUP).transpose(0, 2, 1).reshape(T // tmc, 1, 2 * tmc)
        x = _combine(dest3, x1.reshape(T, D), route, gt2, g_final.reshape(1, D), ys, S, tmc, l == L - 1).reshape(B, S, D)
    return x
```

```python
import functools

import jax
import jax.numpy as jnp
import numpy as np
from jax import lax
from jax.experimental import pallas as pl
from jax.experimental.pallas import tpu as pltpu

N_ATTN_HEADS = 8
Q_RANK = 256
KV_RANK = 128
ATTN_V_DIM = 64
ATTN_WIDTH = N_ATTN_HEADS * ATTN_V_DIM
ATTN_SCALE = KV_RANK ** -0.5
IDX_HEADS = 8
IDX_DIM = 64
IDX_W_SCALE = (IDX_HEADS * IDX_DIM) ** -0.5
TOPK_MAX = 256
N_REC_HEADS = 8
REC_K_DIM = 64
REC_V_DIM = 64
REC_WIDTH = N_REC_HEADS * REC_V_DIM
N_GROUPS = 4
EXPERTS_PER_GROUP = 8
N_EXPERTS = N_GROUPS * EXPERTS_PER_GROUP
TOP_K_IN_GROUP = 2
D_EXPERT = 512
EPS = 1e-6
NEG = -1e30
TINY = 1e-30

LANES = 128
SUBLANES = 8
Q_BLOCK = 128
COUNT_ACCS = 8
ONES_ROWS = 16
LOG2E = 1.4426950408889634
SUB = 16
DIAG_GROUP = 4
VMEM_LIMIT = 56 * 1024 * 1024

F32 = jnp.float32
BF16 = jnp.bfloat16
I32 = jnp.int32


def _cparams(sem):
    return pltpu.CompilerParams(dimension_semantics=sem, vmem_limit_bytes=VMEM_LIMIT)


def _nt_dot(a, b):
    return lax.dot_general(a, b, (((1,), (1,)), ((), ())), preferred_element_type=F32)


def _tn_dot(a, b):
    return lax.dot_general(a, b, (((0,), (0,)), ((), ())), preferred_element_type=F32)


def _dot(a, b):
    return jnp.dot(a, b, preferred_element_type=F32)


def _split_dot(a_f32, b_hi, b_lo):
    a_hi = a_f32.astype(BF16)
    a_lo = (a_f32 - a_hi.astype(F32)).astype(BF16)
    return _dot(a_hi, b_hi) + (_dot(a_hi, b_lo) + _dot(a_lo, b_hi))


def _hi_lo(w):
    hi = w.astype(BF16)
    lo = (w - hi.astype(F32)).astype(BF16)
    return hi, lo


def _mod_kernel(c_ref, w_ref, b_ref, o_ref):
    c = c_ref[...]
    ca = c * jax.nn.sigmoid(c)
    w = w_ref[0]
    w_hi = w.astype(BF16)
    w_lo = (w - w_hi.astype(F32)).astype(BF16)
    o_ref[0] = _split_dot(ca, w_hi, w_lo) + b_ref[0]


def _modulation(c, w_mod, b_mod):
    L, D, D6 = w_mod.shape
    B = c.shape[0]
    tn = 1024
    return pl.pallas_call(
        _mod_kernel,
        out_shape=jax.ShapeDtypeStruct((L, B, D6), F32),
        grid=(L, D6 // tn),
        in_specs=[
            pl.BlockSpec((B, D), lambda l, n: (0, 0)),
            pl.BlockSpec((1, D, tn), lambda l, n: (l, 0, n)),
            pl.BlockSpec((1, 1, tn), lambda l, n: (l, 0, n)),
        ],
        out_specs=pl.BlockSpec((1, B, tn), lambda l, n: (l, 0, n)),
        compiler_params=_cparams(("arbitrary", "arbitrary")),
        name="modulation",
    )(c, w_mod, b_mod.reshape(L, 1, D6))


def _rms(x, eps=EPS):
    return x * lax.rsqrt(jnp.mean(x * x, axis=-1, keepdims=True) + eps)


def _proj_kernel(layer, x_ref, sh_ref, sc_ref, g1_ref, wa_ref, wrec_ref, wg_ref, gcq_ref, gckv_ref,
                 gk_ref, wq_ref, wiq_ref, lb_ref,
                 q_out, qi_out, kv_out, kvt_out, kidx_out, wt_out, qs_out, k_out, v_out, vt_out, lf_out, og_out,
                 sga_out, sgr_out):
    x = x_ref[0]
    tm = x.shape[0]
    h = (_rms(x) * g1_ref[...]) * (1.0 + sc_ref[0]) + sh_ref[0]
    hb = h.astype(BF16)

    ua = _dot(hb, wa_ref[...])
    cq = (_rms(ua[:, :Q_RANK]) * gcq_ref[...]).astype(BF16)
    q = (_dot(cq, wq_ref[...]) * (ATTN_SCALE * LOG2E)).astype(BF16)
    qi = _dot(cq, wiq_ref[...]).astype(BF16)
    for qb in range(tm // Q_BLOCK):
        rows = slice(qb * Q_BLOCK, (qb + 1) * Q_BLOCK)
        for hd in range(N_ATTN_HEADS):
            q_out[0, qb, hd * Q_BLOCK:(hd + 1) * Q_BLOCK, :] = q[rows, hd * LANES:(hd + 1) * LANES]
        for t in range(IDX_HEADS * IDX_DIM // LANES):
            qi_out[0, qb, t * Q_BLOCK:(t + 1) * Q_BLOCK, :] = qi[rows, t * LANES:(t + 1) * LANES]
    kvn = _rms(ua[:, Q_RANK:Q_RANK + KV_RANK]) * gckv_ref[...]
    kv_out[0] = kvn.astype(BF16)
    kvt_out[0] = kvn.T.astype(BF16)

    t3 = ua[:, Q_RANK + KV_RANK:]
    lane = lax.broadcasted_iota(I32, t3.shape, 1)
    ms = jnp.sum(jnp.where(lane < IDX_DIM, t3 * t3, 0.0), axis=-1, keepdims=True) * (1.0 / IDX_DIM)
    ka = t3 * lax.rsqrt(ms + EPS) * gk_ref[...]
    kb = pltpu.roll(ka, IDX_DIM, 1)
    kidx_out[0] = jnp.concatenate([ka, kb], axis=1).astype(BF16)
    wt_out[0] = (t3 * IDX_W_SCALE).T[IDX_DIM:IDX_DIM + IDX_HEADS, :]

    ur = _dot(hb, wrec_ref[...])
    kw = N_REC_HEADS * REC_K_DIM
    lbl = lb_ref[...]
    e = jnp.exp(lbl - jnp.max(lbl, axis=0, keepdims=True))
    p = e / jnp.sum(e, axis=0, keepdims=True)
    lb = jnp.zeros((1, kw), F32)
    for j in range(1, layer + 1):
        lb = lb + p[j:j + 1, :]
    lb = jnp.clip(lb, 0.0, 1.0)
    sig = jax.nn.sigmoid(ur[:, kw:2 * kw])
    f = lb + (1.0 - lb) * sig
    lf_out[0] = jnp.log(jnp.maximum(f, TINY))
    k_out[0] = ((1.0 - lb) * (1.0 - sig)).astype(BF16)
    qr = ur[:, :kw]
    qs_out[0] = (qr * jax.nn.sigmoid(qr)).astype(BF16)
    vr = ur[:, 2 * kw:2 * kw + REC_WIDTH]
    v_out[0] = vr.astype(BF16)
    vt_out[0] = vr.T.astype(BF16)
    og = ur[:, 2 * kw + REC_WIDTH:]
    og_out[0] = (og * jax.nn.sigmoid(og)).astype(BF16)

    ug = _dot(hb, wg_ref[...])
    D = x.shape[1]
    sga_out[0] = jax.nn.sigmoid(ug[:, :D]).astype(BF16)
    sgr_out[0] = jax.nn.sigmoid(ug[:, D:]).astype(BF16)


def _proj(layer, x, sh1, sc1, g1, wa, wrec, wg, gcq, gckv, gk, wq, wiq, lb_logits, tm):
    B, S, D = x.shape
    nqb = tm // Q_BLOCK
    nq = S // Q_BLOCK
    kw = N_REC_HEADS * REC_K_DIM
    c2 = lambda b, i: (0, 0)
    tok = lambda w: pl.BlockSpec((1, tm, w), lambda b, i: (b, i, 0))
    full = lambda a: pl.BlockSpec(a.shape, c2)
    out_shapes = (
        jax.ShapeDtypeStruct((B, nq, N_ATTN_HEADS * Q_BLOCK, KV_RANK), BF16),
        jax.ShapeDtypeStruct((B, nq, IDX_HEADS * IDX_DIM // LANES * Q_BLOCK, LANES), BF16),
        jax.ShapeDtypeStruct((B, S, KV_RANK), BF16),
        jax.ShapeDtypeStruct((B, KV_RANK, S), BF16),
        jax.ShapeDtypeStruct((B, S, 2 * LANES), BF16),
        jax.ShapeDtypeStruct((B, IDX_HEADS, S), F32),
        jax.ShapeDtypeStruct((B, S, kw), BF16),
        jax.ShapeDtypeStruct((B, S, kw), BF16),
        jax.ShapeDtypeStruct((B, S, REC_WIDTH), BF16),
        jax.ShapeDtypeStruct((B, REC_WIDTH, S), BF16),
        jax.ShapeDtypeStruct((B, S, kw), F32),
        jax.ShapeDtypeStruct((B, S, REC_WIDTH), BF16),
        jax.ShapeDtypeStruct((B, S, D), BF16),
        jax.ShapeDtypeStruct((B, S, D), BF16),
    )
    out_specs = (
        pl.BlockSpec((1, nqb, N_ATTN_HEADS * Q_BLOCK, KV_RANK), lambda b, i: (b, i, 0, 0)),
        pl.BlockSpec((1, nqb, IDX_HEADS * IDX_DIM // LANES * Q_BLOCK, LANES), lambda b, i: (b, i, 0, 0)),
        tok(KV_RANK),
        pl.BlockSpec((1, KV_RANK, tm), lambda b, i: (b, 0, i)),
        tok(2 * LANES),
        pl.BlockSpec((1, IDX_HEADS, tm), lambda b, i: (b, 0, i)),
        tok(kw), tok(kw), tok(REC_WIDTH),
        pl.BlockSpec((1, REC_WIDTH, tm), lambda b, i: (b, 0, i)),
        tok(kw), tok(REC_WIDTH),
        tok(D), tok(D),
    )
    vec = lambda: pl.BlockSpec((1, 1, D), lambda b, i: (b, 0, 0))
    return pl.pallas_call(
        functools.partial(_proj_kernel, layer),
        out_shape=out_shapes,
        grid=(B, S // tm),
        in_specs=[tok(D), vec(), vec(), full(g1), full(wa), full(wrec), full(wg), full(gcq), full(gckv),
                  full(gk), full(wq), full(wiq), full(lb_logits)],
        out_specs=out_specs,
        compiler_params=_cparams(("arbitrary", "arbitrary")),
        name="proj",
    )(x, sh1, sc1, g1, wa, wrec, wg, gcq, gckv, gk, wq, wiq, lb_logits)


def _loop_pairs(n, body):
    def pair(p, carry):
        body(2 * p, carry)
        body(2 * p + 1, carry)
        return carry
    lax.fori_loop(0, n // 2, pair, 0)

    @pl.when(n % 2 == 1)
    def _():
        body(n - 1, 0)


def _sort_key(score):
    bits = lax.bitcast_convert_type(score, I32)
    return bits ^ (lax.shift_right_arithmetic(bits, 31) & jnp.int32(0x7FFFFFFF))


def _dsa_kernel(S, KC, KA, TOPK, q_ref, qi_ref, wt_ref, kv_ref, kvt_ref, kidx_ref, wvt_ref, ya_ref,
                keys_ref, acc_ref, m_ref, a_ref, p_ref):
    i = pl.program_id(1)
    n_ch = ((i + 1) * Q_BLOCK + KC - 1) // KC
    qpos = i * Q_BLOCK + lax.broadcasted_iota(I32, (1, Q_BLOCK), 1)
    wt = wt_ref[0]
    wrow = [wt[hd:hd + 1, :] for hd in range(IDX_HEADS)]
    qi = qi_ref[0, 0]
    n_t = IDX_HEADS * IDX_DIM // LANES

    def kpos_of(c):
        return c * KC + lax.broadcasted_iota(I32, (KC, 1), 0)

    def score_body(c, carry):
        koff = pl.multiple_of(c * KC, KC)
        la = _nt_dot(kidx_ref[0, pl.ds(koff, KC), 0:LANES], qi)
        lb = _nt_dot(kidx_ref[0, pl.ds(koff, KC), LANES:2 * LANES], qi)
        score = jnp.zeros((KC, Q_BLOCK), F32)
        for t in range(n_t):
            lanes = slice(t * Q_BLOCK, (t + 1) * Q_BLOCK)
            score = score + wrow[2 * t] * jnp.maximum(la[:, lanes], 0.0)
            score = score + wrow[2 * t + 1] * jnp.maximum(lb[:, lanes], 0.0)
        score = jnp.where(kpos_of(c) <= qpos, score, NEG)
        keys_ref[pl.ds(koff, KC), :] = _sort_key(score)
        return carry

    _loop_pairs(n_ch, score_body)

    def count(pred):
        def body(c, accs):
            kk = keys_ref[pl.ds(pl.multiple_of(c * KC, KC), KC), :]
            accs = list(accs)
            for r in range(KC // SUBLANES):
                a = accs[r % COUNT_ACCS]
                accs[r % COUNT_ACCS] = jnp.where(pred(kk[r * SUBLANES:(r + 1) * SUBLANES]), a + 1, a)
            return tuple(accs)
        accs = lax.fori_loop(0, n_ch, body, tuple(jnp.zeros((SUBLANES, Q_BLOCK), I32) for _ in range(COUNT_ACCS)))
        return jnp.sum(functools.reduce(lambda x, y: x + y, accs), axis=0, keepdims=True)

    def bis_body(it, v):
        cand = v + lax.shift_left(jnp.int32(1), 31 - it)
        cnt = count(lambda kk: kk >= cand)
        return jnp.where(cnt >= TOPK, cand, v)

    v = lax.fori_loop(0, 32, bis_body, jnp.full((1, Q_BLOCK), -2 ** 31, I32))
    n_gt = count(lambda kk: kk > v)
    n_ge = count(lambda kk: kk >= v)
    need = (TOPK - n_gt).astype(F32)
    any_split = jnp.max(jnp.where(n_ge > TOPK, 1, 0)) > 0

    @pl.when(any_split)
    def _():
        tri = jnp.where(lax.broadcasted_iota(I32, (KC, KC), 1) <= lax.broadcasted_iota(I32, (KC, KC), 0),
                        1.0, 0.0).astype(BF16)

        def tie_body(c, seen):
            koff = pl.multiple_of(c * KC, KC)
            kk = keys_ref[pl.ds(koff, KC), :]
            tie = kk == v
            tie_f = jnp.where(tie, 1.0, 0.0)
            rank = _dot(tri, tie_f.astype(BF16)) + seen
            keys_ref[pl.ds(koff, KC), :] = jnp.where(tie & (rank > need), v - 1, kk)
            return seen + jnp.sum(tie_f, axis=0, keepdims=True)

        lax.fori_loop(0, n_ch, tie_body, jnp.zeros((1, Q_BLOCK), F32))

    m_ref[...] = jnp.full(m_ref.shape, NEG, F32)
    acc_ref[...] = jnp.zeros(acc_ref.shape, F32)
    ones_rows = jnp.ones((ONES_ROWS, KA), BF16)

    def att_body(c, carry):
        koff = pl.multiple_of(c * KA, KA)
        kvc = kv_ref[0, pl.ds(koff, KA), :]
        kpos = c * KA + lax.broadcasted_iota(I32, (KA, 1), 0)
        msk = (keys_ref[pl.ds(koff, KA), :] >= v) & (kpos <= qpos)
        for hd in range(N_ATTN_HEADS):
            lanes = slice(hd * Q_BLOCK, (hd + 1) * Q_BLOCK)
            s = jnp.where(msk, _nt_dot(kvc, q_ref[0, 0, hd * Q_BLOCK:(hd + 1) * Q_BLOCK, :]), -jnp.inf)
            m_prev = m_ref[:, lanes]
            m_next = jnp.maximum(m_prev, jnp.max(s, axis=0, keepdims=True))
            a_ref[:, lanes] = jnp.exp2(m_prev - m_next)
            m_ref[:, lanes] = m_next
            p_ref[:, lanes] = jnp.exp2(s - m_next).astype(BF16)
        kvt1 = jnp.concatenate([kvt_ref[0, :, pl.ds(koff, KA)], ones_rows], axis=0)
        acc_ref[...] = acc_ref[...] * a_ref[...] + _dot(kvt1, p_ref[...])
        return carry

    _loop_pairs(((i + 1) * Q_BLOCK + KA - 1) // KA, att_body)
    o = (acc_ref[0:KV_RANK, :] / acc_ref[KV_RANK:KV_RANK + 1, :]).astype(BF16)
    ya_t = jnp.concatenate([_dot(wvt_ref[hd], o[:, hd * Q_BLOCK:(hd + 1) * Q_BLOCK])
                            for hd in range(N_ATTN_HEADS)], axis=0)
    ya_ref[0] = ya_t.T.astype(BF16)


def _dsa(q4, qi4, wt, kv, kvt, kidx2, wvt):
    B, nq = q4.shape[0], q4.shape[1]
    S = kv.shape[1]
    KC = min(512, S)
    KA = min(512, S)
    topk = min(TOPK_MAX, S // 4)
    assert S % KC == 0 and KC % KA == 0 and KA % Q_BLOCK == 0
    hq = N_ATTN_HEADS * Q_BLOCK
    return pl.pallas_call(
        functools.partial(_dsa_kernel, S, KC, KA, topk),
        out_shape=jax.ShapeDtypeStruct((B, S, ATTN_WIDTH), BF16),
        grid=(B, nq),
        in_specs=[
            pl.BlockSpec((1, 1) + q4.shape[2:], lambda b, i: (b, i, 0, 0)),
            pl.BlockSpec((1, 1) + qi4.shape[2:], lambda b, i: (b, i, 0, 0)),
            pl.BlockSpec((1, IDX_HEADS, Q_BLOCK), lambda b, i: (b, 0, i)),
            pl.BlockSpec((1, S, KV_RANK), lambda b, i: (b, 0, 0)),
            pl.BlockSpec((1, KV_RANK, S), lambda b, i: (b, 0, 0)),
            pl.BlockSpec((1, S, 2 * LANES), lambda b, i: (b, 0, 0)),
            pl.BlockSpec(wvt.shape, lambda b, i: (0, 0, 0)),
        ],
        out_specs=pl.BlockSpec((1, Q_BLOCK, ATTN_WIDTH), lambda b, i: (b, i, 0)),
        scratch_shapes=[
            pltpu.VMEM((S, Q_BLOCK), I32),
            pltpu.VMEM((KV_RANK + ONES_ROWS, hq), F32),
            pltpu.VMEM((1, hq), F32),
            pltpu.VMEM((1, hq), F32),
            pltpu.VMEM((KA, hq), BF16),
        ],
        compiler_params=_cparams(("arbitrary", "arbitrary")),
        name="dsa",
    )(q4, qi4, wt, kv, kvt, kidx2, wvt)


def _hgrn_kernel(TS, qs_ref, k_ref, v_ref, vt_ref, lf_ref, og_ref, grec_ref, y_ref,
                 st_ref, aloc_ref, e_ref, qh_ref, kh_ref, o_ref, g_ref, sb_ref, kblk_ref):
    n_tiles = REC_WIDTH // LANES
    nsb = TS // SUB

    @pl.when(pl.program_id(1) == 0)
    def _():
        st_ref[...] = jnp.zeros(st_ref.shape, F32)

    a = lf_ref[0]
    r = lax.broadcasted_iota(I32, a.shape, 0) % SUB
    for sh in (1, 2, 4, 8):
        a = a + jnp.where(r >= sh, pltpu.roll(a, sh, 0), 0.0)
    aend = jnp.where(r == SUB - 1, a, 0.0)
    for sh in (1, 2, 4, 8):
        aend = aend + jnp.where(r + sh <= SUB - 1, pltpu.roll(aend, TS - sh, 0), 0.0)
    e = jnp.exp(a)
    aloc_ref[...] = a
    e_ref[...] = e
    qh_ref[...] = (qs_ref[0].astype(F32) * e).astype(BF16)
    kh_ref[...] = (k_ref[0].astype(F32) * jnp.exp(aend - a)).astype(BF16)

    half = REC_K_DIM
    li = lax.broadcasted_iota(I32, (LANES, LANES), 0) // half
    lj = lax.broadcasted_iota(I32, (LANES, LANES), 1) // half
    bd_f = jnp.where(li == lj, 1.0, 0.0)
    bd_b = bd_f.astype(BF16)
    gsel = jnp.where(lax.broadcasted_iota(I32, (SUB, SUB * SUB), 0)
                     == lax.broadcasted_iota(I32, (SUB, SUB * SUB), 1) // SUB, 1.0, 0.0).astype(BF16)
    srow = lax.broadcasted_iota(I32, (SUB, LANES), 0)

    def diag_body(g, carry):
        units = []
        for jj in range(DIAG_GROUP):
            rows = pl.ds(pl.multiple_of((g * DIAG_GROUP + jj) * SUB, SUB), SUB)
            for p in range(n_tiles):
                units.append((rows, slice(p * LANES, (p + 1) * LANES)))
        wsts = []
        for rows, lanes in units:
            a_blk = aloc_ref[rows, lanes]
            q_blk = qs_ref[0, rows, lanes].astype(F32)
            k_blk = k_ref[0, rows, lanes].astype(F32)
            ws = []
            for t in range(SUB):
                d = a_blk[t:t + 1, :] - a_blk
                w = jnp.exp(jnp.where(srow <= t, d, -jnp.inf)) * (k_blk * q_blk[t:t + 1, :])
                ws.append(w.astype(BF16))
            wsts.append(jnp.concatenate(ws, axis=0))
        rexps = [_dot(wst, bd_b) for wst in wsts]
        m2s = []
        for (rows, lanes), rexp in zip(units, rexps):
            v_f = v_ref[0, rows, lanes].astype(F32)
            m2s.append((rexp * jnp.concatenate([v_f] * SUB, axis=0)).astype(BF16))
        outs = [_dot(gsel, m2) for m2 in m2s]
        for (rows, lanes), out in zip(units, outs):
            o_ref[rows, lanes] = out
        return carry

    lax.fori_loop(0, nsb // DIAG_GROUP, diag_body, 0)

    @pl.when((pl.program_id(0) == 0) & (pl.program_id(1) == 0))
    def _():
        kblk_ref[...] = jnp.zeros(kblk_ref.shape, BF16)

    for p in range(n_tiles):
        lanes = slice(p * LANES, (p + 1) * LANES)
        for j in range(nsb):
            kblk_ref[j * SUB:(j + 1) * SUB, j * LANES:(j + 1) * LANES] = kh_ref[j * SUB:(j + 1) * SUB, lanes]
        g_ref[p] = _dot(vt_ref[0, lanes, :], kblk_ref[...])

    for p in range(n_tiles):
        lanes = slice(p * LANES, (p + 1) * LANES)
        st = st_ref[p]
        for j in range(nsb):
            sb_ref[j, p] = st.astype(BF16)
            dec = e_ref[(j + 1) * SUB - 1:(j + 1) * SUB, lanes]
            st = st * dec + g_ref[p, :, j * LANES:(j + 1) * LANES] * bd_f
        st_ref[p] = st

    for j in range(nsb):
        rows = slice(j * SUB, (j + 1) * SUB)
        for p in range(n_tiles):
            lanes = slice(p * LANES, (p + 1) * LANES)
            o_ref[rows, lanes] = o_ref[rows, lanes] + _nt_dot(qh_ref[rows, lanes], sb_ref[j, p])

    o = o_ref[...]
    o2 = o * o
    hi = o2.astype(BF16)
    lo = (o2 - hi.astype(F32)).astype(BF16)
    ms = jnp.concatenate(
        [_dot(hi[:, p * LANES:(p + 1) * LANES], bd_b) + _dot(lo[:, p * LANES:(p + 1) * LANES], bd_b)
         for p in range(n_tiles)], axis=1) * (1.0 / REC_V_DIM)
    y = o * lax.rsqrt(ms + EPS) * grec_ref[...]
    y_ref[0] = (y * og_ref[0].astype(F32)).astype(BF16)


def _hgrn(qs, k, v, vt, lf, og, grec_t, ts):
    B, S, W = qs.shape
    nsb = ts // SUB
    n_tiles = W // LANES
    tok = lambda: pl.BlockSpec((1, ts, W), lambda b, i: (b, i, 0))
    return pl.pallas_call(
        functools.partial(_hgrn_kernel, ts),
        out_shape=jax.ShapeDtypeStruct((B, S, W), BF16),
        grid=(B, S // ts),
        in_specs=[tok(), tok(), tok(), pl.BlockSpec((1, W, ts), lambda b, i: (b, 0, i)), tok(), tok(),
                  pl.BlockSpec((1, W), lambda b, i: (0, 0))],
        out_specs=tok(),
        scratch_shapes=[
            pltpu.VMEM((n_tiles, LANES, LANES), F32),
            pltpu.VMEM((ts, W), F32),
            pltpu.VMEM((ts, W), F32),
            pltpu.VMEM((ts, W), BF16),
            pltpu.VMEM((ts, W), BF16),
            pltpu.VMEM((ts, W), F32),
            pltpu.VMEM((n_tiles, LANES, nsb * LANES), F32),
            pltpu.VMEM((nsb, n_tiles, LANES, LANES), BF16),
            pltpu.VMEM((ts, nsb * LANES), BF16),
        ],
        compiler_params=_cparams(("arbitrary", "arbitrary")),
        name="hgrn",
    )(qs, k, v, vt, lf, og, grec_t)


def _merge_kernel(x_ref, ya_ref, yr_ref, sga_ref, sgr_ref, gt1_ref, sh2_ref, sc2_ref, g2_ref,
                  wba_ref, wbr_ref, wo_ref, wrh_ref, wrl_ref, br_ref, x1_out, h2_out, rt_out, cnt_out):
    merged = (sga_ref[0].astype(F32) * _dot(ya_ref[0], wba_ref[...])
              + sgr_ref[0].astype(F32) * _dot(yr_ref[0], wbr_ref[...]))
    x1 = x_ref[0] + gt1_ref[0] * _dot(merged.astype(BF16), wo_ref[...])
    x1_out[0] = x1
    h2 = (_rms(x1) * g2_ref[...]) * (1.0 + sc2_ref[0]) + sh2_ref[0]
    _to_token_tiles(h2_out.at[0], h2)

    lg = _split_dot(h2, wrh_ref[...], wrl_ref[...]) + br_ref[...]
    lane = lax.broadcasted_iota(I32, lg.shape, 1)
    big = jnp.int32(1 << 20)
    gl = jnp.where(lane < N_GROUPS, lg, -jnp.inf)
    gmax = jnp.max(gl, axis=1, keepdims=True)
    gsel = jnp.min(jnp.where(gl == gmax, lane, big), axis=1, keepdims=True)
    ggate = 1.0 / jnp.sum(jnp.exp(gl - gmax), axis=1, keepdims=True)
    lo = N_GROUPS + EXPERTS_PER_GROUP * gsel
    el = jnp.where((lane >= lo) & (lane < lo + EXPERTS_PER_GROUP), lg, -jnp.inf)
    v1 = jnp.max(el, axis=1, keepdims=True)
    i1 = jnp.min(jnp.where(el == v1, lane, big), axis=1, keepdims=True)
    el2 = jnp.where(lane == i1, -jnp.inf, el)
    v2 = jnp.max(el2, axis=1, keepdims=True)
    i2 = jnp.min(jnp.where(el2 == v2, lane, big), axis=1, keepdims=True)
    e2 = jnp.exp(v2 - v1)
    den = 1.0 + e2
    g1 = (1.0 / den) * ggate
    g2 = (e2 / den) * ggate
    tm = lg.shape[0]
    oh1 = jnp.where(lane == i1 - N_GROUPS, 1.0, 0.0)
    oh2 = jnp.where(lane == i2 - N_GROUPS, 1.0, 0.0)
    ohs = (oh1 + oh2).astype(BF16)
    ltri = jnp.where(lax.broadcasted_iota(I32, (tm, tm), 1) < lax.broadcasted_iota(I32, (tm, tm), 0),
                     1.0, 0.0).astype(BF16)
    before = _dot(ltri, ohs)
    r1 = jnp.sum(before * oh1, axis=1, keepdims=True)
    r2 = jnp.sum(before * oh2, axis=1, keepdims=True)
    cnt_out[0, 0] = jnp.broadcast_to(jnp.sum(oh1 + oh2, axis=0, keepdims=True), (SUBLANES, LANES))
    vals = ((i1 - N_GROUPS).astype(F32), (i2 - N_GROUPS).astype(F32), g1, g2, r1, r2)
    rt = jnp.zeros(lg.shape, F32)
    for j, val in enumerate(vals):
        rt = jnp.where(lane == j, val, rt)
    rt_out[0] = rt


def _merge(x, ya, yr, sga, sgr, gt1, sh2, sc2, g2, wba, wbr, wo, wrh, wrl, br, tm):
    B, S, D = x.shape
    tok = lambda w: pl.BlockSpec((1, tm, w), lambda b, i: (b, i, 0))
    vec = lambda: pl.BlockSpec((1, 1, D), lambda b, i: (b, 0, 0))
    full = lambda a: pl.BlockSpec(a.shape, lambda b, i: (0,) * a.ndim)
    return pl.pallas_call(
        _merge_kernel,
        out_shape=(jax.ShapeDtypeStruct((B, S, D), F32), jax.ShapeDtypeStruct((B, S * SUBLANES, LANES), F32),
                   jax.ShapeDtypeStruct((B, S, LANES), F32),
                   jax.ShapeDtypeStruct((B, S // tm, SUBLANES, LANES), F32)),
        grid=(B, S // tm),
        in_specs=[tok(D), tok(ATTN_WIDTH), tok(REC_WIDTH), tok(D), tok(D), vec(), vec(), vec(), full(g2),
                  full(wba), full(wbr), full(wo), full(wrh), full(wrl), full(br)],
        out_specs=(tok(D), pl.BlockSpec((1, tm * SUBLANES, LANES), lambda b, i: (b, i, 0)), tok(LANES),
                   pl.BlockSpec((1, 1, SUBLANES, LANES), lambda b, i: (b, i, 0, 0))),
        compiler_params=_cparams(("arbitrary", "arbitrary")),
        name="merge",
    )(x, ya, yr, sga, sgr, gt1, sh2, sc2, g2, wba, wbr, wo, wrh, wrl, br)


def _to_token_tiles(ref, x):
    n = x.shape[0]
    for s in range(SUBLANES):
        ref[pl.ds(s, n, stride=SUBLANES), :] = x[:, s * LANES:(s + 1) * LANES]


def _from_token_tiles(ref, first_tile, n):
    return jnp.concatenate([ref[pl.ds(first_tile * SUBLANES + s, n, stride=SUBLANES), :]
                            for s in range(SUBLANES)], axis=1)


def _tile_gather(src_hbm, idx_ref, dst, sem, n):
    def body(r, carry):
        src = src_hbm.at[pl.ds(pl.multiple_of(idx_ref[0, 0, r] * SUBLANES, SUBLANES), SUBLANES)]
        pltpu.make_async_copy(src, dst.at[pl.ds(pl.multiple_of(r * SUBLANES, SUBLANES), SUBLANES)], sem).start()
        return carry
    lax.fori_loop(0, n, body, 0, unroll=8)


def _tile_gather_wait(src_hbm, dst, sem, n):
    pltpu.make_async_copy(src_hbm.at[pl.ds(0, n * SUBLANES)], dst, sem).wait()


def _scatter_kernel(TM, d_ref, h2_ref, xs_in, xs_out, sem):
    del xs_in

    for j in range(TOP_K_IN_GROUP):
        def body(t, carry):
            src = h2_ref.at[pl.ds(pl.multiple_of(t * SUBLANES, SUBLANES), SUBLANES)]
            dst = xs_out.at[pl.ds(pl.multiple_of(d_ref[0, 0, j * TM + t] * SUBLANES, SUBLANES), SUBLANES)]
            pltpu.make_async_copy(src, dst, sem).start()
            return carry
        lax.fori_loop(0, TM, body, 0, unroll=8)
    for _ in range(TOP_K_IN_GROUP):
        pltpu.make_async_copy(h2_ref, xs_out.at[pl.ds(0, TM * SUBLANES)], sem).wait()


def _scatter_rows(dest3, h2t, n_rows, tm):
    n = dest3.shape[0]
    xs0 = jnp.zeros((n_rows * SUBLANES, LANES), F32)
    return pl.pallas_call(
        functools.partial(_scatter_kernel, tm),
        out_shape=jax.ShapeDtypeStruct(xs0.shape, F32),
        grid=(n,),
        in_specs=[
            pl.BlockSpec((1, 1, TOP_K_IN_GROUP * tm), lambda i: (i, 0, 0), memory_space=pltpu.SMEM),
            pl.BlockSpec((tm * SUBLANES, LANES), lambda i: (i, 0)),
            pl.BlockSpec(memory_space=pl.ANY),
        ],
        out_specs=pl.BlockSpec(memory_space=pl.ANY),
        scratch_shapes=[pltpu.SemaphoreType.DMA(())],
        input_output_aliases={2: 0},
        compiler_params=_cparams(("arbitrary",)),
        name="scatter_rows",
    )(dest3, h2t, xs0)


def _moe_kernel(be_ref, x_ref, wg_ref, wu_ref, wd_ref, y_ref, wgb, wub, wdb):
    i = pl.program_id(0)

    @pl.when((i == 0) | (be_ref[i] != be_ref[jnp.maximum(i - 1, 0)]))
    def _():
        wgb[...] = wg_ref[0, 0].astype(BF16)
        wub[...] = wu_ref[0, 0].astype(BF16)
        wdb[...] = wd_ref[0, 0].astype(BF16)

    xb = _from_token_tiles(x_ref, 0, x_ref.shape[0] // SUBLANES).astype(BF16)
    g = _dot(xb, wgb[...])
    hid = (g * jax.nn.sigmoid(g)) * _dot(xb, wub[...])
    _to_token_tiles(y_ref, _dot(hid.astype(BF16), wdb[...]))


def _moe(layer, blk_expert, xs, wgate, wup, wdown, blk):
    n_blk = blk_expert.shape[0]
    D = wgate.shape[2]
    rows = pl.BlockSpec((blk * SUBLANES, LANES), lambda i, be: (i, 0))
    grid_spec = pltpu.PrefetchScalarGridSpec(
        num_scalar_prefetch=1,
        grid=(n_blk,),
        in_specs=[
            rows,
            pl.BlockSpec((1, 1, D, D_EXPERT), lambda i, be: (layer, be[i], 0, 0)),
            pl.BlockSpec((1, 1, D, D_EXPERT), lambda i, be: (layer, be[i], 0, 0)),
            pl.BlockSpec((1, 1, D_EXPERT, D), lambda i, be: (layer, be[i], 0, 0)),
        ],
        out_specs=rows,
        scratch_shapes=[pltpu.VMEM((D, D_EXPERT), BF16), pltpu.VMEM((D, D_EXPERT), BF16),
                        pltpu.VMEM((D_EXPERT, D), BF16)],
    )
    return pl.pallas_call(
        _moe_kernel,
        out_shape=jax.ShapeDtypeStruct(xs.shape, F32),
        grid_spec=grid_spec,
        compiler_params=_cparams(("arbitrary",)),
        name="moe",
    )(blk_expert, xs, wgate, wup, wdown)


def _comb_kernel(TM, final, d_ref, dn_ref, x1_ref, rt_ref, gt2_ref, gf_ref, ys_hbm, o_ref, buf, sem):
    i = pl.program_id(0)
    n = pl.num_programs(0)
    slot = lax.rem(i, 2)

    @pl.when(i == 0)
    def _():
        _tile_gather(ys_hbm, d_ref, buf.at[0], sem.at[0], 2 * TM)

    @pl.when(i + 1 < n)
    def _():
        _tile_gather(ys_hbm, dn_ref, buf.at[1 - slot], sem.at[1 - slot], 2 * TM)

    _tile_gather_wait(ys_hbm, buf.at[slot], sem.at[slot], 2 * TM)
    rt = rt_ref[...]
    y0 = _from_token_tiles(buf.at[slot], 0, TM)
    y1 = _from_token_tiles(buf.at[slot], TM, TM)
    mix = rt[:, TOP_K_IN_GROUP:TOP_K_IN_GROUP + 1] * y0 + rt[:, TOP_K_IN_GROUP + 1:TOP_K_IN_GROUP + 2] * y1
    x2 = x1_ref[...] + gt2_ref[0] * mix
    if final:
        x2 = _rms(x2) * gf_ref[...]
    o_ref[...] = x2


def _combine(dest3, x1, route, gt2, gf, ys, S, tm, final):
    T, D = x1.shape
    n = T // tm
    per_b = S // tm
    return pl.pallas_call(
        functools.partial(_comb_kernel, tm, final),
        out_shape=jax.ShapeDtypeStruct((T, D), F32),
        grid=(n,),
        in_specs=[
            pl.BlockSpec((1, 1, 2 * tm), lambda i: (i, 0, 0), memory_space=pltpu.SMEM),
            pl.BlockSpec((1, 1, 2 * tm), lambda i: (jnp.minimum(i + 1, n - 1), 0, 0), memory_space=pltpu.SMEM),
            pl.BlockSpec((tm, D), lambda i: (i, 0)),
            pl.BlockSpec((tm, LANES), lambda i: (i, 0)),
            pl.BlockSpec((1, 1, D), lambda i: (i // per_b, 0, 0)),
            pl.BlockSpec((1, D), lambda i: (0, 0)),
            pl.BlockSpec(memory_space=pl.ANY),
        ],
        out_specs=pl.BlockSpec((tm, D), lambda i: (i, 0)),
        scratch_shapes=[pltpu.VMEM((2, 2 * tm * SUBLANES, LANES), F32), pltpu.SemaphoreType.DMA((2,))],
        compiler_params=_cparams(("arbitrary",)),
        name="combine",
    )(dest3, dest3, x1, route, gt2, gf, ys)


def _dispatch(route, tile_cnt, blk):
    T = route.shape[0]
    N = T * TOP_K_IN_GROUP
    n_tiles = tile_cnt.shape[0]
    eid = route[:, 0:TOP_K_IN_GROUP].astype(I32).reshape(N)
    rank_in_tile = route[:, 2 * TOP_K_IN_GROUP:3 * TOP_K_IN_GROUP].astype(I32).reshape(N)
    tile_start = jnp.cumsum(tile_cnt, axis=0) - tile_cnt
    counts = jnp.sum(tile_cnt, axis=0)
    padded = (counts + blk - 1) // blk * blk
    pad_end = jnp.cumsum(padded)
    pad_start = pad_end - padded
    base = jnp.repeat(pad_start[None, :] + tile_start, N // n_tiles, axis=0)
    hit = eid[:, None] == jnp.arange(N_EXPERTS, dtype=I32)[None, :]
    dest = jnp.sum(jnp.where(hit, base, 0), axis=1) + rank_in_tile
    n_blk = (N + N_EXPERTS * blk) // blk
    blk_start = jnp.arange(n_blk, dtype=I32)[:, None] * blk
    blk_expert = jnp.minimum(jnp.sum((pad_end[None, :] <= blk_start).astype(I32), axis=1), N_EXPERTS - 1)
    return blk_expert, dest.reshape(T, TOP_K_IN_GROUP)


def _tile_plan(dest, tm):
    T = dest.shape[0]
    return dest.reshape(T // tm, tm, TOP_K_IN_GROUP).transpose(0, 2, 1).reshape(T // tm, 1, TOP_K_IN_GROUP * tm)


def kernel(x, c, w_mod, b_mod, g_norm1, g_norm2, w_in, g_cq, g_ckv, g_kidx, w_q_up, w_idx_q, w_v_up,
           lb_logits, g_rec, w_branch_a, w_branch_r, w_out, w_grp, b_grp, w_exp_router, b_exp_router,
           w_gate, w_up, w_down, g_final):
    B, S, D = x.shape
    L = w_mod.shape[0]
    T = B * S
    tm = min(256, S)
    ts = min(256, S)
    tmc = min(128, S)
    blk = 256
    kw = N_REC_HEADS * REC_K_DIM
    na = Q_RANK + KV_RANK + IDX_DIM + IDX_HEADS
    n_rec = 2 * kw + 2 * REC_WIDTH

    mod = _modulation(c, w_mod, b_mod)
    for l in range(L):
        m6 = mod[l].reshape(B, 6, 1, D)
        sh1, sc1, gt1, sh2, sc2, gt2 = (m6[:, j] for j in range(6))
        wa = jnp.pad(w_in[l, :, :na], ((0, 0), (0, 512 - na))).astype(BF16)
        wrec = w_in[l, :, na:na + n_rec].astype(BF16)
        wg = w_in[l, :, na + n_rec:].astype(BF16)
        gk = jnp.pad(g_kidx[l], (0, LANES - IDX_DIM)).reshape(1, LANES)
        (q4, qi4, kv, kvt, kidx2, wt, qs, kk, vv, vt, lf, og, sga, sgr) = _proj(
            l, x, sh1, sc1, g_norm1[l].reshape(1, D), wa, wrec, wg, g_cq[l].reshape(1, -1),
            g_ckv[l].reshape(1, -1), gk, w_q_up[l].astype(BF16), w_idx_q[l].astype(BF16), lb_logits, tm)
        wvt = jnp.swapaxes(w_v_up[l], 1, 2).astype(BF16)
        ya = _dsa(q4, qi4, wt, kv, kvt, kidx2, wvt)
        yr = _hgrn(qs, kk, vv, vt, lf, og, jnp.tile(g_rec[l], N_REC_HEADS).reshape(1, REC_WIDTH), ts)
        wr = jnp.pad(jnp.concatenate([w_grp[l], w_exp_router[l]], axis=1),
                     ((0, 0), (0, LANES - N_GROUPS - N_EXPERTS)))
        wrh, wrl = _hi_lo(wr)
        br = jnp.pad(jnp.concatenate([b_grp[l], b_exp_router[l]]), (0, LANES - N_GROUPS - N_EXPERTS)).reshape(1, LANES)
        x1, h2t, route, tile_cnt = _merge(x, ya, yr, sga, sgr, gt1, sh2, sc2, g_norm2[l].reshape(1, D),
                               w_branch_a[l].astype(BF16), w_branch_r[l].astype(BF16), w_out[l].astype(BF16),
                               wrh, wrl, br, tm)
        route = route.reshape(T, LANES)
        tile_cnt = tile_cnt[:, :, 0, :N_EXPERTS].reshape(-1, N_EXPERTS).astype(I32)
        blk_expert, dest = _dispatch(route, tile_cnt, blk)
        xs = _scatter_rows(_tile_plan(dest, tm), h2t.reshape(T * SUBLANES, LANES),
                           TOP_K_IN_GROUP * T + N_EXPERTS * blk, tm)
        ys = _moe(l, blk_expert, xs, w_gate, w_up, w_down, blk)
        x = _combine(_tile_plan(dest, tmc), x1.reshape(T, D), route, gt2, g_final.reshape(1, D), ys, S, tmc,
                     l == L - 1).reshape(B, S, D)
    return x
```

```python
import functools

import jax
import jax.numpy as jnp
from jax import lax
from jax.experimental import pallas as pl
from jax.experimental.pallas import tpu as pltpu

N_ATTN_HEADS = 8
Q_RANK = 256
KV_RANK = 128
ATTN_V_DIM = 64
ATTN_WIDTH = N_ATTN_HEADS * ATTN_V_DIM
ATTN_SCALE = KV_RANK ** -0.5
IDX_HEADS = 8
IDX_DIM = 64
IDX_W_SCALE = (IDX_HEADS * IDX_DIM) ** -0.5
TOPK_MAX = 256
N_REC_HEADS = 8
REC_K_DIM = 64
REC_V_DIM = 64
REC_WIDTH = N_REC_HEADS * REC_V_DIM
N_GROUPS = 4
EXPERTS_PER_GROUP = 8
N_EXPERTS = N_GROUPS * EXPERTS_PER_GROUP
TOP_K_IN_GROUP = 2
D_EXPERT = 512
EPS = 1e-6
NEG = -1e30
TINY = 1e-30

LANES = 128
SUBLANES = 8
Q_BLOCK = 128
COUNT_ACCS = 8
ONES_ROWS = 16
LOG2E = 1.4426950408889634
SUB = 16
DIAG_GROUP = 4
VMEM_LIMIT = 56 * 1024 * 1024


def _tiles(S):
    return dict(
        tm=min(256, S),
        ts=min(256, S),
        tmc=min(128, S),
        kc=min(512, S),
        blk=256,
    )

F32 = jnp.float32
BF16 = jnp.bfloat16
I32 = jnp.int32


def _cparams(sem):
    return pltpu.CompilerParams(dimension_semantics=sem, vmem_limit_bytes=VMEM_LIMIT)


def _nt_dot(a, b):
    return lax.dot_general(a, b, (((1,), (1,)), ((), ())), preferred_element_type=F32)


def _dot(a, b):
    return jnp.dot(a, b, preferred_element_type=F32)


def _split_dot(a_f32, b_hi, b_lo):
    a_hi = a_f32.astype(BF16)
    a_lo = (a_f32 - a_hi.astype(F32)).astype(BF16)
    return _dot(a_hi, b_hi) + (_dot(a_hi, b_lo) + _dot(a_lo, b_hi))


def _hi_lo(w):
    hi = w.astype(BF16)
    lo = (w - hi.astype(F32)).astype(BF16)
    return hi, lo


def _mod_kernel(c_ref, w_ref, b_ref, o_ref):
    c = c_ref[...]
    ca = c * jax.nn.sigmoid(c)
    w = w_ref[0]
    w_hi = w.astype(BF16)
    w_lo = (w - w_hi.astype(F32)).astype(BF16)
    o_ref[0] = _split_dot(ca, w_hi, w_lo) + b_ref[0]


def _modulation(c, w_mod, b_mod):
    L, D, D6 = w_mod.shape
    B = c.shape[0]
    tn = 1024
    return pl.pallas_call(
        _mod_kernel,
        out_shape=jax.ShapeDtypeStruct((L, B, D6), F32),
        grid=(L, D6 // tn),
        in_specs=[
            pl.BlockSpec((B, D), lambda l, n: (0, 0)),
            pl.BlockSpec((1, D, tn), lambda l, n: (l, 0, n)),
            pl.BlockSpec((1, 1, tn), lambda l, n: (l, 0, n)),
        ],
        out_specs=pl.BlockSpec((1, B, tn), lambda l, n: (l, 0, n)),
        compiler_params=_cparams(("arbitrary", "arbitrary")),
        name="modulation",
    )(c, w_mod, b_mod.reshape(L, 1, D6))


def _rms(x, eps=EPS):
    return x * lax.rsqrt(jnp.mean(x * x, axis=-1, keepdims=True) + eps)


def _proj_kernel(layer, x_ref, sh_ref, sc_ref, g1_ref, wa_ref, wrec_ref, wg_ref, gcq_ref, gckv_ref,
                 gk_ref, wq_ref, wiq_ref, lb_ref,
                 q_out, qi_out, kv_out, kvt_out, kidx_out, wt_out, qs_out, k_out, v_out, vt_out, lf_out, og_out,
                 sga_out, sgr_out):
    x = x_ref[0]
    tm = x.shape[0]
    h = (_rms(x) * g1_ref[...]) * (1.0 + sc_ref[0]) + sh_ref[0]
    hb = h.astype(BF16)

    ua = _dot(hb, wa_ref[...])
    cq = (_rms(ua[:, :Q_RANK]) * gcq_ref[...]).astype(BF16)
    q = (_dot(cq, wq_ref[...]) * (ATTN_SCALE * LOG2E)).astype(BF16)
    qi = _dot(cq, wiq_ref[...]).astype(BF16)
    for qb in range(tm // Q_BLOCK):
        rows = slice(qb * Q_BLOCK, (qb + 1) * Q_BLOCK)
        for hd in range(N_ATTN_HEADS):
            q_out[0, qb, hd * Q_BLOCK:(hd + 1) * Q_BLOCK, :] = q[rows, hd * LANES:(hd + 1) * LANES]
        for t in range(IDX_HEADS * IDX_DIM // LANES):
            qi_out[0, qb, t * Q_BLOCK:(t + 1) * Q_BLOCK, :] = qi[rows, t * LANES:(t + 1) * LANES]
    kvn = _rms(ua[:, Q_RANK:Q_RANK + KV_RANK]) * gckv_ref[...]
    kv_out[0] = kvn.astype(BF16)
    kvt_out[0] = kvn.T.astype(BF16)

    t3 = ua[:, Q_RANK + KV_RANK:]
    lane = lax.broadcasted_iota(I32, t3.shape, 1)
    ms = jnp.sum(jnp.where(lane < IDX_DIM, t3 * t3, 0.0), axis=-1, keepdims=True) * (1.0 / IDX_DIM)
    ka = t3 * lax.rsqrt(ms + EPS) * gk_ref[...]
    kb = pltpu.roll(ka, IDX_DIM, 1)
    kidx_out[0] = jnp.concatenate([ka, kb], axis=1).astype(BF16)
    wt_out[0] = (t3 * IDX_W_SCALE).T[IDX_DIM:IDX_DIM + IDX_HEADS, :]

    ur = _dot(hb, wrec_ref[...])
    kw = N_REC_HEADS * REC_K_DIM
    lbl = lb_ref[...]
    e = jnp.exp(lbl - jnp.max(lbl, axis=0, keepdims=True))
    p = e / jnp.sum(e, axis=0, keepdims=True)
    lb = jnp.zeros((1, kw), F32)
    for j in range(1, layer + 1):
        lb = lb + p[j:j + 1, :]
    lb = jnp.clip(lb, 0.0, 1.0)
    sig = jax.nn.sigmoid(ur[:, kw:2 * kw])
    f = lb + (1.0 - lb) * sig
    lf_out[0] = jnp.log(jnp.maximum(f, TINY))
    k_out[0] = ((1.0 - lb) * (1.0 - sig)).astype(BF16)
    qr = ur[:, :kw]
    qs_out[0] = (qr * jax.nn.sigmoid(qr)).astype(BF16)
    vr = ur[:, 2 * kw:2 * kw + REC_WIDTH]
    v_out[0] = vr.astype(BF16)
    vt_out[0] = vr.T.astype(BF16)
    og = ur[:, 2 * kw + REC_WIDTH:]
    og_out[0] = (og * jax.nn.sigmoid(og)).astype(BF16)

    ug = _dot(hb, wg_ref[...])
    D = x.shape[1]
    sga_out[0] = jax.nn.sigmoid(ug[:, :D]).astype(BF16)
    sgr_out[0] = jax.nn.sigmoid(ug[:, D:]).astype(BF16)


def _proj(layer, x, sh1, sc1, g1, wa, wrec, wg, gcq, gckv, gk, wq, wiq, lb_logits, tm):
    B, S, D = x.shape
    nqb = tm // Q_BLOCK
    nq = S // Q_BLOCK
    kw = N_REC_HEADS * REC_K_DIM
    c2 = lambda b, i: (0, 0)
    tok = lambda w: pl.BlockSpec((1, tm, w), lambda b, i: (b, i, 0))
    full = lambda a: pl.BlockSpec(a.shape, c2)
    out_shapes = (
        jax.ShapeDtypeStruct((B, nq, N_ATTN_HEADS * Q_BLOCK, KV_RANK), BF16),
        jax.ShapeDtypeStruct((B, nq, IDX_HEADS * IDX_DIM // LANES * Q_BLOCK, LANES), BF16),
        jax.ShapeDtypeStruct((B, S, KV_RANK), BF16),
        jax.ShapeDtypeStruct((B, KV_RANK, S), BF16),
        jax.ShapeDtypeStruct((B, S, 2 * LANES), BF16),
        jax.ShapeDtypeStruct((B, IDX_HEADS, S), F32),
        jax.ShapeDtypeStruct((B, S, kw), BF16),
        jax.ShapeDtypeStruct((B, S, kw), BF16),
        jax.ShapeDtypeStruct((B, S, REC_WIDTH), BF16),
        jax.ShapeDtypeStruct((B, REC_WIDTH, S), BF16),
        jax.ShapeDtypeStruct((B, S, kw), F32),
        jax.ShapeDtypeStruct((B, S, REC_WIDTH), BF16),
        jax.ShapeDtypeStruct((B, S, D), BF16),
        jax.ShapeDtypeStruct((B, S, D), BF16),
    )
    out_specs = (
        pl.BlockSpec((1, nqb, N_ATTN_HEADS * Q_BLOCK, KV_RANK), lambda b, i: (b, i, 0, 0)),
        pl.BlockSpec((1, nqb, IDX_HEADS * IDX_DIM // LANES * Q_BLOCK, LANES), lambda b, i: (b, i, 0, 0)),
        tok(KV_RANK),
        pl.BlockSpec((1, KV_RANK, tm), lambda b, i: (b, 0, i)),
        tok(2 * LANES),
        pl.BlockSpec((1, IDX_HEADS, tm), lambda b, i: (b, 0, i)),
        tok(kw), tok(kw), tok(REC_WIDTH),
        pl.BlockSpec((1, REC_WIDTH, tm), lambda b, i: (b, 0, i)),
        tok(kw), tok(REC_WIDTH),
        tok(D), tok(D),
    )
    vec = lambda: pl.BlockSpec((1, 1, D), lambda b, i: (b, 0, 0))
    return pl.pallas_call(
        functools.partial(_proj_kernel, layer),
        out_shape=out_shapes,
        grid=(B, S // tm),
        in_specs=[tok(D), vec(), vec(), full(g1), full(wa), full(wrec), full(wg), full(gcq), full(gckv),
                  full(gk), full(wq), full(wiq), full(lb_logits)],
        out_specs=out_specs,
        compiler_params=_cparams(("arbitrary", "arbitrary")),
        name="proj",
    )(x, sh1, sc1, g1, wa, wrec, wg, gcq, gckv, gk, wq, wiq, lb_logits)


def _loop_by_four(n, body):
    def quad(p, carry):
        for j in range(4):
            body(4 * p + j, carry)
        return carry
    lax.fori_loop(0, n // 4, quad, 0)
    done = n // 4 * 4

    @pl.when(n % 4 >= 2)
    def _():
        body(done, 0)
        body(done + 1, 0)

    @pl.when(n % 2 == 1)
    def _():
        body(n - 1, 0)


def _sort_key(score):
    bits = lax.bitcast_convert_type(score, I32)
    return bits ^ (lax.shift_right_arithmetic(bits, 31) & jnp.int32(0x7FFFFFFF))


def _dsa_kernel(S, KC, KA, TOPK, q_ref, qi_ref, wt_ref, kv_ref, kvt_ref, kidx_ref, wvt_ref, ya_ref,
                keys_ref, acc_ref, m_ref, a_ref, p_ref):
    i = pl.program_id(1)
    n_ch = ((i + 1) * Q_BLOCK + KC - 1) // KC
    qpos = i * Q_BLOCK + lax.broadcasted_iota(I32, (1, Q_BLOCK), 1)
    wt = wt_ref[0]
    wrow = [wt[hd:hd + 1, :] for hd in range(IDX_HEADS)]
    qi = qi_ref[0, 0]
    n_t = IDX_HEADS * IDX_DIM // LANES

    def kpos_of(c):
        return c * KC + lax.broadcasted_iota(I32, (KC, 1), 0)

    def score_body(c, carry):
        koff = pl.multiple_of(c * KC, KC)
        la = _nt_dot(kidx_ref[0, pl.ds(koff, KC), 0:LANES], qi)
        lb = _nt_dot(kidx_ref[0, pl.ds(koff, KC), LANES:2 * LANES], qi)
        score = jnp.zeros((KC, Q_BLOCK), F32)
        for t in range(n_t):
            lanes = slice(t * Q_BLOCK, (t + 1) * Q_BLOCK)
            score = score + wrow[2 * t] * jnp.maximum(la[:, lanes], 0.0)
            score = score + wrow[2 * t + 1] * jnp.maximum(lb[:, lanes], 0.0)
        score = jnp.where(kpos_of(c) <= qpos, score, NEG)
        keys_ref[pl.ds(koff, KC), :] = _sort_key(score)
        return carry

    _loop_by_four(n_ch, score_body)

    def count(pred):
        def body(c, accs):
            kk = keys_ref[pl.ds(pl.multiple_of(c * KC, KC), KC), :]
            accs = list(accs)
            for r in range(KC // SUBLANES):
                a = accs[r % COUNT_ACCS]
                accs[r % COUNT_ACCS] = jnp.where(pred(kk[r * SUBLANES:(r + 1) * SUBLANES]), a + 1, a)
            return tuple(accs)
        accs = lax.fori_loop(0, n_ch, body, tuple(jnp.zeros((SUBLANES, Q_BLOCK), I32) for _ in range(COUNT_ACCS)))
        return jnp.sum(functools.reduce(lambda x, y: x + y, accs), axis=0, keepdims=True)

    def bis_body(it, v):
        cand = v + lax.shift_left(jnp.int32(1), 31 - it)
        cnt = count(lambda kk: kk >= cand)
        return jnp.where(cnt >= TOPK, cand, v)

    v = lax.fori_loop(0, 32, bis_body, jnp.full((1, Q_BLOCK), -2 ** 31, I32))
    n_gt = count(lambda kk: kk > v)
    n_ge = count(lambda kk: kk >= v)
    need = (TOPK - n_gt).astype(F32)
    any_split = jnp.max(jnp.where(n_ge > TOPK, 1, 0)) > 0

    @pl.when(any_split)
    def _():
        tri = jnp.where(lax.broadcasted_iota(I32, (KC, KC), 1) <= lax.broadcasted_iota(I32, (KC, KC), 0),
                        1.0, 0.0).astype(BF16)

        def tie_body(c, seen):
            koff = pl.multiple_of(c * KC, KC)
            kk = keys_ref[pl.ds(koff, KC), :]
            tie = kk == v
            tie_f = jnp.where(tie, 1.0, 0.0)
            rank = _dot(tri, tie_f.astype(BF16)) + seen
            keys_ref[pl.ds(koff, KC), :] = jnp.where(tie & (rank > need), v - 1, kk)
            return seen + jnp.sum(tie_f, axis=0, keepdims=True)

        lax.fori_loop(0, n_ch, tie_body, jnp.zeros((1, Q_BLOCK), F32))

    m_ref[...] = jnp.full(m_ref.shape, NEG, F32)
    acc_ref[...] = jnp.zeros(acc_ref.shape, F32)
    ones_rows = jnp.ones((ONES_ROWS, KA), BF16)

    def att_body(c, carry):
        koff = pl.multiple_of(c * KA, KA)
        kvc = kv_ref[0, pl.ds(koff, KA), :]
        kpos = c * KA + lax.broadcasted_iota(I32, (KA, 1), 0)
        msk = (keys_ref[pl.ds(koff, KA), :] >= v) & (kpos <= qpos)
        for hd in range(N_ATTN_HEADS):
            lanes = slice(hd * Q_BLOCK, (hd + 1) * Q_BLOCK)
            s = jnp.where(msk, _nt_dot(kvc, q_ref[0, 0, hd * Q_BLOCK:(hd + 1) * Q_BLOCK, :]), -jnp.inf)
            m_prev = m_ref[:, lanes]
            m_next = jnp.maximum(m_prev, jnp.max(s, axis=0, keepdims=True))
            a_ref[:, lanes] = jnp.exp2(m_prev - m_next)
            m_ref[:, lanes] = m_next
            p_ref[:, lanes] = jnp.exp2(s - m_next).astype(BF16)
        kvt1 = jnp.concatenate([kvt_ref[0, :, pl.ds(koff, KA)], ones_rows], axis=0)
        acc_ref[...] = acc_ref[...] * a_ref[...] + _dot(kvt1, p_ref[...])
        return carry

    _loop_by_four(((i + 1) * Q_BLOCK + KA - 1) // KA, att_body)
    o = (acc_ref[0:KV_RANK, :] / acc_ref[KV_RANK:KV_RANK + 1, :]).astype(BF16)
    ya_t = jnp.concatenate([_dot(wvt_ref[hd], o[:, hd * Q_BLOCK:(hd + 1) * Q_BLOCK])
                            for hd in range(N_ATTN_HEADS)], axis=0)
    ya_ref[0] = ya_t.T.astype(BF16)


def _dsa(q4, qi4, wt, kv, kvt, kidx2, wvt):
    B, nq = q4.shape[0], q4.shape[1]
    S = kv.shape[1]
    KC = KA = _tiles(S)["kc"]
    topk = min(TOPK_MAX, S // 4)
    assert S % KC == 0 and KC % KA == 0 and KA % Q_BLOCK == 0
    hq = N_ATTN_HEADS * Q_BLOCK
    return pl.pallas_call(
        functools.partial(_dsa_kernel, S, KC, KA, topk),
        out_shape=jax.ShapeDtypeStruct((B, S, ATTN_WIDTH), BF16),
        grid=(B, nq),
        in_specs=[
            pl.BlockSpec((1, 1) + q4.shape[2:], lambda b, i: (b, i, 0, 0)),
            pl.BlockSpec((1, 1) + qi4.shape[2:], lambda b, i: (b, i, 0, 0)),
            pl.BlockSpec((1, IDX_HEADS, Q_BLOCK), lambda b, i: (b, 0, i)),
            pl.BlockSpec((1, S, KV_RANK), lambda b, i: (b, 0, 0)),
            pl.BlockSpec((1, KV_RANK, S), lambda b, i: (b, 0, 0)),
            pl.BlockSpec((1, S, 2 * LANES), lambda b, i: (b, 0, 0)),
            pl.BlockSpec(wvt.shape, lambda b, i: (0, 0, 0)),
        ],
        out_specs=pl.BlockSpec((1, Q_BLOCK, ATTN_WIDTH), lambda b, i: (b, i, 0)),
        scratch_shapes=[
            pltpu.VMEM((S, Q_BLOCK), I32),
            pltpu.VMEM((KV_RANK + ONES_ROWS, hq), F32),
            pltpu.VMEM((1, hq), F32),
            pltpu.VMEM((1, hq), F32),
            pltpu.VMEM((KA, hq), BF16),
        ],
        compiler_params=_cparams(("arbitrary", "arbitrary")),
        name="dsa",
    )(q4, qi4, wt, kv, kvt, kidx2, wvt)


def _hgrn_kernel(TS, qs_ref, k_ref, v_ref, vt_ref, lf_ref, og_ref, grec_ref, y_ref,
                 st_ref, aloc_ref, e_ref, qh_ref, kh_ref, o_ref, g_ref, sb_ref, kblk_ref):
    n_tiles = REC_WIDTH // LANES
    nsb = TS // SUB

    @pl.when(pl.program_id(1) == 0)
    def _():
        st_ref[...] = jnp.zeros(st_ref.shape, F32)

    a = lf_ref[0]
    r = lax.broadcasted_iota(I32, a.shape, 0) % SUB
    for sh in (1, 2, 4, 8):
        a = a + jnp.where(r >= sh, pltpu.roll(a, sh, 0), 0.0)
    aend = jnp.where(r == SUB - 1, a, 0.0)
    for sh in (1, 2, 4, 8):
        aend = aend + jnp.where(r + sh <= SUB - 1, pltpu.roll(aend, TS - sh, 0), 0.0)
    e = jnp.exp(a)
    aloc_ref[...] = a
    e_ref[...] = e
    qh_ref[...] = (qs_ref[0].astype(F32) * e).astype(BF16)
    kh_ref[...] = (k_ref[0].astype(F32) * jnp.exp(aend - a)).astype(BF16)

    half = REC_K_DIM
    li = lax.broadcasted_iota(I32, (LANES, LANES), 0) // half
    lj = lax.broadcasted_iota(I32, (LANES, LANES), 1) // half
    bd_f = jnp.where(li == lj, 1.0, 0.0)
    bd_b = bd_f.astype(BF16)
    gsel = jnp.where(lax.broadcasted_iota(I32, (SUB, SUB * SUB), 0)
                     == lax.broadcasted_iota(I32, (SUB, SUB * SUB), 1) // SUB, 1.0, 0.0).astype(BF16)
    srow = lax.broadcasted_iota(I32, (SUB, LANES), 0)

    def diag_body(g, carry):
        units = []
        for jj in range(DIAG_GROUP):
            rows = pl.ds(pl.multiple_of((g * DIAG_GROUP + jj) * SUB, SUB), SUB)
            for p in range(n_tiles):
                units.append((rows, slice(p * LANES, (p + 1) * LANES)))
        wsts = []
        for rows, lanes in units:
            a_blk = aloc_ref[rows, lanes]
            q_blk = qs_ref[0, rows, lanes].astype(F32)
            k_blk = k_ref[0, rows, lanes].astype(F32)
            ws = []
            for t in range(SUB):
                d = a_blk[t:t + 1, :] - a_blk
                w = jnp.exp(jnp.where(srow <= t, d, -jnp.inf)) * (k_blk * q_blk[t:t + 1, :])
                ws.append(w.astype(BF16))
            wsts.append(jnp.concatenate(ws, axis=0))
        rexps = [_dot(wst, bd_b) for wst in wsts]
        m2s = []
        for (rows, lanes), rexp in zip(units, rexps):
            v_f = v_ref[0, rows, lanes].astype(F32)
            m2s.append((rexp * jnp.concatenate([v_f] * SUB, axis=0)).astype(BF16))
        outs = [_dot(gsel, m2) for m2 in m2s]
        for (rows, lanes), out in zip(units, outs):
            o_ref[rows, lanes] = out
        return carry

    lax.fori_loop(0, nsb // DIAG_GROUP, diag_body, 0)

    @pl.when((pl.program_id(0) == 0) & (pl.program_id(1) == 0))
    def _():
        kblk_ref[...] = jnp.zeros(kblk_ref.shape, BF16)

    for p in range(n_tiles):
        lanes = slice(p * LANES, (p + 1) * LANES)
        for j in range(nsb):
            kblk_ref[j * SUB:(j + 1) * SUB, j * LANES:(j + 1) * LANES] = kh_ref[j * SUB:(j + 1) * SUB, lanes]
        g_ref[p] = _dot(vt_ref[0, lanes, :], kblk_ref[...])

    for p in range(n_tiles):
        lanes = slice(p * LANES, (p + 1) * LANES)
        st = st_ref[p]
        for j in range(nsb):
            sb_ref[j, p] = st.astype(BF16)
            dec = e_ref[(j + 1) * SUB - 1:(j + 1) * SUB, lanes]
            st = st * dec + g_ref[p, :, j * LANES:(j + 1) * LANES] * bd_f
        st_ref[p] = st

    for j in range(nsb):
        rows = slice(j * SUB, (j + 1) * SUB)
        for p in range(n_tiles):
            lanes = slice(p * LANES, (p + 1) * LANES)
            o_ref[rows, lanes] = o_ref[rows, lanes] + _nt_dot(qh_ref[rows, lanes], sb_ref[j, p])

    o = o_ref[...]
    o2 = o * o
    hi = o2.astype(BF16)
    lo = (o2 - hi.astype(F32)).astype(BF16)
    ms = jnp.concatenate(
        [_dot(hi[:, p * LANES:(p + 1) * LANES], bd_b) + _dot(lo[:, p * LANES:(p + 1) * LANES], bd_b)
         for p in range(n_tiles)], axis=1) * (1.0 / REC_V_DIM)
    y = o * lax.rsqrt(ms + EPS) * grec_ref[...]
    y_ref[0] = (y * og_ref[0].astype(F32)).astype(BF16)


def _hgrn(qs, k, v, vt, lf, og, grec_t, ts):
    B, S, W = qs.shape
    nsb = ts // SUB
    n_tiles = W // LANES
    tok = lambda: pl.BlockSpec((1, ts, W), lambda b, i: (b, i, 0))
    return pl.pallas_call(
        functools.partial(_hgrn_kernel, ts),
        out_shape=jax.ShapeDtypeStruct((B, S, W), BF16),
        grid=(B, S // ts),
        in_specs=[tok(), tok(), tok(), pl.BlockSpec((1, W, ts), lambda b, i: (b, 0, i)), tok(), tok(),
                  pl.BlockSpec((1, W), lambda b, i: (0, 0))],
        out_specs=tok(),
        scratch_shapes=[
            pltpu.VMEM((n_tiles, LANES, LANES), F32),
            pltpu.VMEM((ts, W), F32),
            pltpu.VMEM((ts, W), F32),
            pltpu.VMEM((ts, W), BF16),
            pltpu.VMEM((ts, W), BF16),
            pltpu.VMEM((ts, W), F32),
            pltpu.VMEM((n_tiles, LANES, nsb * LANES), F32),
            pltpu.VMEM((nsb, n_tiles, LANES, LANES), BF16),
            pltpu.VMEM((ts, nsb * LANES), BF16),
        ],
        compiler_params=_cparams(("arbitrary", "arbitrary")),
        name="hgrn",
    )(qs, k, v, vt, lf, og, grec_t)


def _merge_kernel(x_ref, ya_ref, yr_ref, sga_ref, sgr_ref, gt1_ref, sh2_ref, sc2_ref, g2_ref,
                  wba_ref, wbr_ref, wo_ref, wrh_ref, wrl_ref, br_ref, x1_out, h2_out, rt_out, cnt_out):
    merged = (sga_ref[0].astype(F32) * _dot(ya_ref[0], wba_ref[...])
              + sgr_ref[0].astype(F32) * _dot(yr_ref[0], wbr_ref[...]))
    x1 = x_ref[0] + gt1_ref[0] * _dot(merged.astype(BF16), wo_ref[...])
    x1_out[0] = x1
    h2 = (_rms(x1) * g2_ref[...]) * (1.0 + sc2_ref[0]) + sh2_ref[0]
    _to_token_tiles(h2_out.at[0], h2)

    lg = _split_dot(h2, wrh_ref[...], wrl_ref[...]) + br_ref[...]
    lane = lax.broadcasted_iota(I32, lg.shape, 1)
    big = jnp.int32(1 << 20)
    gl = jnp.where(lane < N_GROUPS, lg, -jnp.inf)
    gmax = jnp.max(gl, axis=1, keepdims=True)
    gsel = jnp.min(jnp.where(gl == gmax, lane, big), axis=1, keepdims=True)
    ggate = 1.0 / jnp.sum(jnp.exp(gl - gmax), axis=1, keepdims=True)
    lo = N_GROUPS + EXPERTS_PER_GROUP * gsel
    el = jnp.where((lane >= lo) & (lane < lo + EXPERTS_PER_GROUP), lg, -jnp.inf)
    v1 = jnp.max(el, axis=1, keepdims=True)
    i1 = jnp.min(jnp.where(el == v1, lane, big), axis=1, keepdims=True)
    el2 = jnp.where(lane == i1, -jnp.inf, el)
    v2 = jnp.max(el2, axis=1, keepdims=True)
    i2 = jnp.min(jnp.where(el2 == v2, lane, big), axis=1, keepdims=True)
    e2 = jnp.exp(v2 - v1)
    den = 1.0 + e2
    g1 = (1.0 / den) * ggate
    g2 = (e2 / den) * ggate
    tm = lg.shape[0]
    oh1 = jnp.where(lane == i1 - N_GROUPS, 1.0, 0.0)
    oh2 = jnp.where(lane == i2 - N_GROUPS, 1.0, 0.0)
    ohs = (oh1 + oh2).astype(BF16)
    ltri = jnp.where(lax.broadcasted_iota(I32, (tm, tm), 1) < lax.broadcasted_iota(I32, (tm, tm), 0),
                     1.0, 0.0).astype(BF16)
    before = _dot(ltri, ohs)
    r1 = jnp.sum(before * oh1, axis=1, keepdims=True)
    r2 = jnp.sum(before * oh2, axis=1, keepdims=True)
    cnt_out[0, 0] = jnp.broadcast_to(jnp.sum(oh1 + oh2, axis=0, keepdims=True), (SUBLANES, LANES))
    vals = ((i1 - N_GROUPS).astype(F32), (i2 - N_GROUPS).astype(F32), g1, g2, r1, r2)
    rt = jnp.zeros(lg.shape, F32)
    for j, val in enumerate(vals):
        rt = jnp.where(lane == j, val, rt)
    rt_out[0] = rt


def _merge(x, ya, yr, sga, sgr, gt1, sh2, sc2, g2, wba, wbr, wo, wrh, wrl, br, tm):
    B, S, D = x.shape
    tok = lambda w: pl.BlockSpec((1, tm, w), lambda b, i: (b, i, 0))
    vec = lambda: pl.BlockSpec((1, 1, D), lambda b, i: (b, 0, 0))
    full = lambda a: pl.BlockSpec(a.shape, lambda b, i: (0,) * a.ndim)
    return pl.pallas_call(
        _merge_kernel,
        out_shape=(jax.ShapeDtypeStruct((B, S, D), F32), jax.ShapeDtypeStruct((B, S * SUBLANES, LANES), F32),
                   jax.ShapeDtypeStruct((B, S, LANES), F32),
                   jax.ShapeDtypeStruct((B, S // tm, SUBLANES, LANES), F32)),
        grid=(B, S // tm),
        in_specs=[tok(D), tok(ATTN_WIDTH), tok(REC_WIDTH), tok(D), tok(D), vec(), vec(), vec(), full(g2),
                  full(wba), full(wbr), full(wo), full(wrh), full(wrl), full(br)],
        out_specs=(tok(D), pl.BlockSpec((1, tm * SUBLANES, LANES), lambda b, i: (b, i, 0)), tok(LANES),
                   pl.BlockSpec((1, 1, SUBLANES, LANES), lambda b, i: (b, i, 0, 0))),
        compiler_params=_cparams(("arbitrary", "arbitrary")),
        name="merge",
    )(x, ya, yr, sga, sgr, gt1, sh2, sc2, g2, wba, wbr, wo, wrh, wrl, br)


def _to_token_tiles(ref, x):
    n = x.shape[0]
    for s in range(SUBLANES):
        ref[pl.ds(s, n, stride=SUBLANES), :] = x[:, s * LANES:(s + 1) * LANES]


def _from_token_tiles(ref, first_tile, n):
    return jnp.concatenate([ref[pl.ds(first_tile * SUBLANES + s, n, stride=SUBLANES), :]
                            for s in range(SUBLANES)], axis=1)


def _tile_gather(src_hbm, idx_ref, dst, sem, n):
    def body(r, carry):
        src = src_hbm.at[pl.ds(pl.multiple_of(idx_ref[0, 0, r] * SUBLANES, SUBLANES), SUBLANES)]
        pltpu.make_async_copy(src, dst.at[pl.ds(pl.multiple_of(r * SUBLANES, SUBLANES), SUBLANES)], sem).start()
        return carry
    lax.fori_loop(0, n, body, 0, unroll=8)


def _tile_gather_wait(src_hbm, dst, sem, n):
    pltpu.make_async_copy(src_hbm.at[pl.ds(0, n * SUBLANES)], dst, sem).wait()


def _scatter_kernel(TM, d_ref, h2_ref, xs_in, xs_out, sem):
    del xs_in

    for j in range(TOP_K_IN_GROUP):
        def body(t, carry):
            src = h2_ref.at[pl.ds(pl.multiple_of(t * SUBLANES, SUBLANES), SUBLANES)]
            dst = xs_out.at[pl.ds(pl.multiple_of(d_ref[0, 0, j * TM + t] * SUBLANES, SUBLANES), SUBLANES)]
            pltpu.make_async_copy(src, dst, sem).start()
            return carry
        lax.fori_loop(0, TM, body, 0, unroll=8)
    for _ in range(TOP_K_IN_GROUP):
        pltpu.make_async_copy(h2_ref, xs_out.at[pl.ds(0, TM * SUBLANES)], sem).wait()


def _scatter_rows(dest3, h2t, n_rows, tm):
    n = dest3.shape[0]
    xs0 = jnp.zeros((n_rows * SUBLANES, LANES), F32)
    return pl.pallas_call(
        functools.partial(_scatter_kernel, tm),
        out_shape=jax.ShapeDtypeStruct(xs0.shape, F32),
        grid=(n,),
        in_specs=[
            pl.BlockSpec((1, 1, TOP_K_IN_GROUP * tm), lambda i: (i, 0, 0), memory_space=pltpu.SMEM),
            pl.BlockSpec((tm * SUBLANES, LANES), lambda i: (i, 0)),
            pl.BlockSpec(memory_space=pl.ANY),
        ],
        out_specs=pl.BlockSpec(memory_space=pl.ANY),
        scratch_shapes=[pltpu.SemaphoreType.DMA(())],
        input_output_aliases={2: 0},
        compiler_params=_cparams(("arbitrary",)),
        name="scatter_rows",
    )(dest3, h2t, xs0)


def _moe_kernel(be_ref, x_ref, wg_ref, wu_ref, wd_ref, y_ref, wgb, wub, wdb):
    i = pl.program_id(0)

    @pl.when((i == 0) | (be_ref[i] != be_ref[jnp.maximum(i - 1, 0)]))
    def _():
        wgb[...] = wg_ref[0, 0].astype(BF16)
        wub[...] = wu_ref[0, 0].astype(BF16)
        wdb[...] = wd_ref[0, 0].astype(BF16)

    xb = _from_token_tiles(x_ref, 0, x_ref.shape[0] // SUBLANES).astype(BF16)
    g = _dot(xb, wgb[...])
    hid = (g * jax.nn.sigmoid(g)) * _dot(xb, wub[...])
    _to_token_tiles(y_ref, _dot(hid.astype(BF16), wdb[...]))


def _moe(layer, blk_expert, xs, wgate, wup, wdown, blk):
    n_blk = blk_expert.shape[0]
    D = wgate.shape[2]
    rows = pl.BlockSpec((blk * SUBLANES, LANES), lambda i, be: (i, 0))
    grid_spec = pltpu.PrefetchScalarGridSpec(
        num_scalar_prefetch=1,
        grid=(n_blk,),
        in_specs=[
            rows,
            pl.BlockSpec((1, 1, D, D_EXPERT), lambda i, be: (layer, be[i], 0, 0)),
            pl.BlockSpec((1, 1, D, D_EXPERT), lambda i, be: (layer, be[i], 0, 0)),
            pl.BlockSpec((1, 1, D_EXPERT, D), lambda i, be: (layer, be[i], 0, 0)),
        ],
        out_specs=rows,
        scratch_shapes=[pltpu.VMEM((D, D_EXPERT), BF16), pltpu.VMEM((D, D_EXPERT), BF16),
                        pltpu.VMEM((D_EXPERT, D), BF16)],
    )
    return pl.pallas_call(
        _moe_kernel,
        out_shape=jax.ShapeDtypeStruct(xs.shape, F32),
        grid_spec=grid_spec,
        compiler_params=_cparams(("arbitrary",)),
        name="moe",
    )(blk_expert, xs, wgate, wup, wdown)


def _comb_kernel(TM, final, d_ref, dn_ref, x1_ref, rt_ref, gt2_ref, gf_ref, ys_hbm, o_ref, buf, sem):
    i = pl.program_id(0)
    n = pl.num_programs(0)
    slot = lax.rem(i, 2)

    @pl.when(i == 0)
    def _():
        _tile_gather(ys_hbm, d_ref, buf.at[0], sem.at[0], 2 * TM)

    @pl.when(i + 1 < n)
    def _():
        _tile_gather(ys_hbm, dn_ref, buf.at[1 - slot], sem.at[1 - slot], 2 * TM)

    _tile_gather_wait(ys_hbm, buf.at[slot], sem.at[slot], 2 * TM)
    rt = rt_ref[...]
    y0 = _from_token_tiles(buf.at[slot], 0, TM)
    y1 = _from_token_tiles(buf.at[slot], TM, TM)
    mix = rt[:, TOP_K_IN_GROUP:TOP_K_IN_GROUP + 1] * y0 + rt[:, TOP_K_IN_GROUP + 1:TOP_K_IN_GROUP + 2] * y1
    x2 = x1_ref[...] + gt2_ref[0] * mix
    if final:
        x2 = _rms(x2) * gf_ref[...]
    o_ref[...] = x2


def _combine(dest3, x1, route, gt2, gf, ys, S, tm, final):
    T, D = x1.shape
    n = T // tm
    per_b = S // tm
    return pl.pallas_call(
        functools.partial(_comb_kernel, tm, final),
        out_shape=jax.ShapeDtypeStruct((T, D), F32),
        grid=(n,),
        in_specs=[
            pl.BlockSpec((1, 1, 2 * tm), lambda i: (i, 0, 0), memory_space=pltpu.SMEM),
            pl.BlockSpec((1, 1, 2 * tm), lambda i: (jnp.minimum(i + 1, n - 1), 0, 0), memory_space=pltpu.SMEM),
            pl.BlockSpec((tm, D), lambda i: (i, 0)),
            pl.BlockSpec((tm, LANES), lambda i: (i, 0)),
            pl.BlockSpec((1, 1, D), lambda i: (i // per_b, 0, 0)),
            pl.BlockSpec((1, D), lambda i: (0, 0)),
            pl.BlockSpec(memory_space=pl.ANY),
        ],
        out_specs=pl.BlockSpec((tm, D), lambda i: (i, 0)),
        scratch_shapes=[pltpu.VMEM((2, 2 * tm * SUBLANES, LANES), F32), pltpu.SemaphoreType.DMA((2,))],
        compiler_params=_cparams(("arbitrary",)),
        name="combine",
    )(dest3, dest3, x1, route, gt2, gf, ys)


def _dispatch(route, tile_cnt, blk):
    T = route.shape[0]
    N = T * TOP_K_IN_GROUP
    n_tiles = tile_cnt.shape[0]
    eid = route[:, 0:TOP_K_IN_GROUP].astype(I32).reshape(N)
    rank_in_tile = route[:, 2 * TOP_K_IN_GROUP:3 * TOP_K_IN_GROUP].astype(I32).reshape(N)
    tile_start = jnp.cumsum(tile_cnt, axis=0) - tile_cnt
    counts = jnp.sum(tile_cnt, axis=0)
    padded = (counts + blk - 1) // blk * blk
    pad_end = jnp.cumsum(padded)
    pad_start = pad_end - padded
    base = jnp.repeat(pad_start[None, :] + tile_start, N // n_tiles, axis=0)
    hit = eid[:, None] == jnp.arange(N_EXPERTS, dtype=I32)[None, :]
    dest = jnp.sum(jnp.where(hit, base, 0), axis=1) + rank_in_tile
    n_blk = (N + N_EXPERTS * blk) // blk
    blk_start = jnp.arange(n_blk, dtype=I32)[:, None] * blk
    blk_expert = jnp.minimum(jnp.sum((pad_end[None, :] <= blk_start).astype(I32), axis=1), N_EXPERTS - 1)
    return blk_expert, dest.reshape(T, TOP_K_IN_GROUP)


def _tile_plan(dest, tm):
    T = dest.shape[0]
    return dest.reshape(T // tm, tm, TOP_K_IN_GROUP).transpose(0, 2, 1).reshape(T // tm, 1, TOP_K_IN_GROUP * tm)


def kernel(x, c, w_mod, b_mod, g_norm1, g_norm2, w_in, g_cq, g_ckv, g_kidx, w_q_up, w_idx_q, w_v_up,
           lb_logits, g_rec, w_branch_a, w_branch_r, w_out, w_grp, b_grp, w_exp_router, b_exp_router,
           w_gate, w_up, w_down, g_final):
    B, S, D = x.shape
    L = w_mod.shape[0]
    T = B * S
    tiles = _tiles(S)
    tm, ts, tmc, blk = tiles["tm"], tiles["ts"], tiles["tmc"], tiles["blk"]
    kw = N_REC_HEADS * REC_K_DIM
    na = Q_RANK + KV_RANK + IDX_DIM + IDX_HEADS
    n_rec = 2 * kw + 2 * REC_WIDTH

    mod = _modulation(c, w_mod, b_mod)
    for l in range(L):
        m6 = mod[l].reshape(B, 6, 1, D)
        sh1, sc1, gt1, sh2, sc2, gt2 = (m6[:, j] for j in range(6))
        wa = jnp.pad(w_in[l, :, :na], ((0, 0), (0, 512 - na))).astype(BF16)
        wrec = w_in[l, :, na:na + n_rec].astype(BF16)
        wg = w_in[l, :, na + n_rec:].astype(BF16)
        gk = jnp.pad(g_kidx[l], (0, LANES - IDX_DIM)).reshape(1, LANES)
        (q4, qi4, kv, kvt, kidx2, wt, qs, kk, vv, vt, lf, og, sga, sgr) = _proj(
            l, x, sh1, sc1, g_norm1[l].reshape(1, D), wa, wrec, wg, g_cq[l].reshape(1, -1),
            g_ckv[l].reshape(1, -1), gk, w_q_up[l].astype(BF16), w_idx_q[l].astype(BF16), lb_logits, tm)
        wvt = jnp.swapaxes(w_v_up[l], 1, 2).astype(BF16)
        ya = _dsa(q4, qi4, wt, kv, kvt, kidx2, wvt)
        yr = _hgrn(qs, kk, vv, vt, lf, og, jnp.tile(g_rec[l], N_REC_HEADS).reshape(1, REC_WIDTH), ts)
        wr = jnp.pad(jnp.concatenate([w_grp[l], w_exp_router[l]], axis=1),
                     ((0, 0), (0, LANES - N_GROUPS - N_EXPERTS)))
        wrh, wrl = _hi_lo(wr)
        br = jnp.pad(jnp.concatenate([b_grp[l], b_exp_router[l]]), (0, LANES - N_GROUPS - N_EXPERTS)).reshape(1, LANES)
        x1, h2t, route, tile_cnt = _merge(x, ya, yr, sga, sgr, gt1, sh2, sc2, g_norm2[l].reshape(1, D),
                               w_branch_a[l].astype(BF16), w_branch_r[l].astype(BF16), w_out[l].astype(BF16),
                               wrh, wrl, br, tm)
        route = route.reshape(T, LANES)
        tile_cnt = tile_cnt[:, :, 0, :N_EXPERTS].reshape(-1, N_EXPERTS).astype(I32)
        blk_expert, dest = _dispatch(route, tile_cnt, blk)
        xs = _scatter_rows(_tile_plan(dest, tm), h2t.reshape(T * SUBLANES, LANES),
                           TOP_K_IN_GROUP * T + N_EXPERTS * blk, tm)
        ys = _moe(l, blk_expert, xs, w_gate, w_up, w_down, blk)
        x = _combine(_tile_plan(dest, tmc), x1.reshape(T, D), route, gt2, g_final.reshape(1, D), ys, S, tmc,
                     l == L - 1).reshape(B, S, D)
    return x
```

```python
import functools

import jax
import jax.numpy as jnp
from jax import lax
from jax.experimental import pallas as pl
from jax.experimental.pallas import tpu as pltpu

N_ATTN_HEADS = 8
Q_RANK = 256
KV_RANK = 128
ATTN_V_DIM = 64
ATTN_WIDTH = N_ATTN_HEADS * ATTN_V_DIM
ATTN_SCALE = KV_RANK ** -0.5
IDX_HEADS = 8
IDX_DIM = 64
IDX_W_SCALE = (IDX_HEADS * IDX_DIM) ** -0.5
TOPK_MAX = 256
N_REC_HEADS = 8
REC_K_DIM = 64
REC_V_DIM = 64
REC_WIDTH = N_REC_HEADS * REC_V_DIM
N_GROUPS = 4
EXPERTS_PER_GROUP = 8
N_EXPERTS = N_GROUPS * EXPERTS_PER_GROUP
TOP_K_IN_GROUP = 2
D_EXPERT = 512
EPS = 1e-6
NEG = -1e30
TINY = 1e-30

LANES = 128
SUBLANES = 8
Q_BLOCK = 128
COUNT_ACCS = 8
ONES_ROWS = 16
LOG2E = 1.4426950408889634
SUB = 16
DIAG_GROUP = 4
MERGE_SPLIT = 2
VMEM_LIMIT = 56 * 1024 * 1024


def _tiles(S):
    return dict(
        tm=min(256, S),
        ts=min(256, S),
        tmc=min(128, S),
        kc=min(512, S),
        blk=256,
    )

F32 = jnp.float32
BF16 = jnp.bfloat16
I32 = jnp.int32


def _cparams(sem):
    return pltpu.CompilerParams(dimension_semantics=sem, vmem_limit_bytes=VMEM_LIMIT)


def _nt_dot(a, b):
    return lax.dot_general(a, b, (((1,), (1,)), ((), ())), preferred_element_type=F32)


def _dot(a, b):
    return jnp.dot(a, b, preferred_element_type=F32)


def _split_dot(a_f32, b_hi, b_lo):
    a_hi = a_f32.astype(BF16)
    a_lo = (a_f32 - a_hi.astype(F32)).astype(BF16)
    return _dot(a_hi, b_hi) + (_dot(a_hi, b_lo) + _dot(a_lo, b_hi))


def _hi_lo(w):
    hi = w.astype(BF16)
    lo = (w - hi.astype(F32)).astype(BF16)
    return hi, lo


def _mod_kernel(c_ref, w_ref, b_ref, o_ref):
    c = c_ref[...]
    ca = c * jax.nn.sigmoid(c)
    w = w_ref[0]
    w_hi = w.astype(BF16)
    w_lo = (w - w_hi.astype(F32)).astype(BF16)
    o_ref[0] = _split_dot(ca, w_hi, w_lo) + b_ref[0]


def _modulation(c, w_mod, b_mod):
    L, D, D6 = w_mod.shape
    B = c.shape[0]
    tn = 1024
    return pl.pallas_call(
        _mod_kernel,
        out_shape=jax.ShapeDtypeStruct((L, B, D6), F32),
        grid=(L, D6 // tn),
        in_specs=[
            pl.BlockSpec((B, D), lambda l, n: (0, 0)),
            pl.BlockSpec((1, D, tn), lambda l, n: (l, 0, n)),
            pl.BlockSpec((1, 1, tn), lambda l, n: (l, 0, n)),
        ],
        out_specs=pl.BlockSpec((1, B, tn), lambda l, n: (l, 0, n)),
        compiler_params=_cparams(("arbitrary", "arbitrary")),
        name="modulation",
    )(c, w_mod, b_mod.reshape(L, 1, D6))


def _rms(x, eps=EPS):
    return x * lax.rsqrt(jnp.mean(x * x, axis=-1, keepdims=True) + eps)


def _proj_kernel(layer, x_ref, sh_ref, sc_ref, g1_ref, wa_ref, wrec_ref, wg_ref, gcq_ref, gckv_ref,
                 gk_ref, wq_ref, wiq_ref, lb_ref,
                 q_out, qi_out, kv_out, kvt_out, kidx_out, wt_out, qs_out, k_out, v_out, vt_out, lf_out, og_out,
                 sga_out, sgr_out):
    x = x_ref[0]
    tm = x.shape[0]
    h = (_rms(x) * g1_ref[...]) * (1.0 + sc_ref[0]) + sh_ref[0]
    hb = h.astype(BF16)

    ua = _dot(hb, wa_ref[...])
    cq = (_rms(ua[:, :Q_RANK]) * gcq_ref[...]).astype(BF16)
    q = (_dot(cq, wq_ref[...]) * (ATTN_SCALE * LOG2E)).astype(BF16)
    qi = _dot(cq, wiq_ref[...]).astype(BF16)
    for qb in range(tm // Q_BLOCK):
        rows = slice(qb * Q_BLOCK, (qb + 1) * Q_BLOCK)
        for hd in range(N_ATTN_HEADS):
            q_out[0, qb, hd * Q_BLOCK:(hd + 1) * Q_BLOCK, :] = q[rows, hd * LANES:(hd + 1) * LANES]
        for t in range(IDX_HEADS * IDX_DIM // LANES):
            qi_out[0, qb, t * Q_BLOCK:(t + 1) * Q_BLOCK, :] = qi[rows, t * LANES:(t + 1) * LANES]
    kvn = _rms(ua[:, Q_RANK:Q_RANK + KV_RANK]) * gckv_ref[...]
    kv_out[0] = kvn.astype(BF16)
    kvt_out[0] = kvn.T.astype(BF16)

    t3 = ua[:, Q_RANK + KV_RANK:]
    lane = lax.broadcasted_iota(I32, t3.shape, 1)
    ms = jnp.sum(jnp.where(lane < IDX_DIM, t3 * t3, 0.0), axis=-1, keepdims=True) * (1.0 / IDX_DIM)
    ka = t3 * lax.rsqrt(ms + EPS) * gk_ref[...]
    kb = pltpu.roll(ka, IDX_DIM, 1)
    kidx_out[0] = jnp.concatenate([ka, kb], axis=1).astype(BF16)
    wt_out[0] = (t3 * IDX_W_SCALE).T[IDX_DIM:IDX_DIM + IDX_HEADS, :]

    ur = _dot(hb, wrec_ref[...])
    kw = N_REC_HEADS * REC_K_DIM
    lbl = lb_ref[...]
    e = jnp.exp(lbl - jnp.max(lbl, axis=0, keepdims=True))
    p = e / jnp.sum(e, axis=0, keepdims=True)
    lb = jnp.zeros((1, kw), F32)
    for j in range(1, layer + 1):
        lb = lb + p[j:j + 1, :]
    lb = jnp.clip(lb, 0.0, 1.0)
    sig = jax.nn.sigmoid(ur[:, kw:2 * kw])
    f = lb + (1.0 - lb) * sig
    lf_out[0] = jnp.log(jnp.maximum(f, TINY))
    k_out[0] = ((1.0 - lb) * (1.0 - sig)).astype(BF16)
    qr = ur[:, :kw]
    qs_out[0] = (qr * jax.nn.sigmoid(qr)).astype(BF16)
    vr = ur[:, 2 * kw:2 * kw + REC_WIDTH]
    v_out[0] = vr.astype(BF16)
    vt_out[0] = vr.T.astype(BF16)
    og = ur[:, 2 * kw + REC_WIDTH:]
    og_out[0] = (og * jax.nn.sigmoid(og)).astype(BF16)

    ug = _dot(hb, wg_ref[...])
    D = x.shape[1]
    sga_out[0] = jax.nn.sigmoid(ug[:, :D]).astype(BF16)
    sgr_out[0] = jax.nn.sigmoid(ug[:, D:]).astype(BF16)


def _proj(layer, x, sh1, sc1, g1, wa, wrec, wg, gcq, gckv, gk, wq, wiq, lb_logits, tm):
    B, S, D = x.shape
    nqb = tm // Q_BLOCK
    nq = S // Q_BLOCK
    kw = N_REC_HEADS * REC_K_DIM
    c2 = lambda b, i: (0, 0)
    tok = lambda w: pl.BlockSpec((1, tm, w), lambda b, i: (b, i, 0))
    full = lambda a: pl.BlockSpec(a.shape, c2)
    out_shapes = (
        jax.ShapeDtypeStruct((B, nq, N_ATTN_HEADS * Q_BLOCK, KV_RANK), BF16),
        jax.ShapeDtypeStruct((B, nq, IDX_HEADS * IDX_DIM // LANES * Q_BLOCK, LANES), BF16),
        jax.ShapeDtypeStruct((B, S, KV_RANK), BF16),
        jax.ShapeDtypeStruct((B, KV_RANK, S), BF16),
        jax.ShapeDtypeStruct((B, S, 2 * LANES), BF16),
        jax.ShapeDtypeStruct((B, IDX_HEADS, S), F32),
        jax.ShapeDtypeStruct((B, S, kw), BF16),
        jax.ShapeDtypeStruct((B, S, kw), BF16),
        jax.ShapeDtypeStruct((B, S, REC_WIDTH), BF16),
        jax.ShapeDtypeStruct((B, REC_WIDTH, S), BF16),
        jax.ShapeDtypeStruct((B, S, kw), F32),
        jax.ShapeDtypeStruct((B, S, REC_WIDTH), BF16),
        jax.ShapeDtypeStruct((B, S, D), BF16),
        jax.ShapeDtypeStruct((B, S, D), BF16),
    )
    out_specs = (
        pl.BlockSpec((1, nqb, N_ATTN_HEADS * Q_BLOCK, KV_RANK), lambda b, i: (b, i, 0, 0)),
        pl.BlockSpec((1, nqb, IDX_HEADS * IDX_DIM // LANES * Q_BLOCK, LANES), lambda b, i: (b, i, 0, 0)),
        tok(KV_RANK),
        pl.BlockSpec((1, KV_RANK, tm), lambda b, i: (b, 0, i)),
        tok(2 * LANES),
        pl.BlockSpec((1, IDX_HEADS, tm), lambda b, i: (b, 0, i)),
        tok(kw), tok(kw), tok(REC_WIDTH),
        pl.BlockSpec((1, REC_WIDTH, tm), lambda b, i: (b, 0, i)),
        tok(kw), tok(REC_WIDTH),
        tok(D), tok(D),
    )
    vec = lambda: pl.BlockSpec((1, 1, D), lambda b, i: (b, 0, 0))
    return pl.pallas_call(
        functools.partial(_proj_kernel, layer),
        out_shape=out_shapes,
        grid=(B, S // tm),
        in_specs=[tok(D), vec(), vec(), full(g1), full(wa), full(wrec), full(wg), full(gcq), full(gckv),
                  full(gk), full(wq), full(wiq), full(lb_logits)],
        out_specs=out_specs,
        compiler_params=_cparams(("arbitrary", "arbitrary")),
        name="proj",
    )(x, sh1, sc1, g1, wa, wrec, wg, gcq, gckv, gk, wq, wiq, lb_logits)


def _loop_by_four(n, body):
    def quad(p, carry):
        for j in range(4):
            body(4 * p + j, carry)
        return carry
    lax.fori_loop(0, n // 4, quad, 0)
    done = n // 4 * 4

    @pl.when(n % 4 >= 2)
    def _():
        body(done, 0)
        body(done + 1, 0)

    @pl.when(n % 2 == 1)
    def _():
        body(n - 1, 0)


def _sort_key(score):
    bits = lax.bitcast_convert_type(score, I32)
    return bits ^ (lax.shift_right_arithmetic(bits, 31) & jnp.int32(0x7FFFFFFF))


def _dsa_kernel(S, KC, KA, TOPK, q_ref, qi_ref, wt_ref, kv_ref, kvt_ref, kidx_ref, wvt_ref, ya_ref,
                keys_ref, acc_ref, m_ref, a_ref, p_ref):
    i = pl.program_id(1)
    n_ch = ((i + 1) * Q_BLOCK + KC - 1) // KC
    qpos = i * Q_BLOCK + lax.broadcasted_iota(I32, (1, Q_BLOCK), 1)
    wt = wt_ref[0]
    wrow = [wt[hd:hd + 1, :] for hd in range(IDX_HEADS)]
    qi = qi_ref[0, 0]
    n_t = IDX_HEADS * IDX_DIM // LANES

    def kpos_of(c):
        return c * KC + lax.broadcasted_iota(I32, (KC, 1), 0)

    def score_body(c, carry):
        koff = pl.multiple_of(c * KC, KC)
        la = _nt_dot(kidx_ref[0, pl.ds(koff, KC), 0:LANES], qi)
        lb = _nt_dot(kidx_ref[0, pl.ds(koff, KC), LANES:2 * LANES], qi)
        score = jnp.zeros((KC, Q_BLOCK), F32)
        for t in range(n_t):
            lanes = slice(t * Q_BLOCK, (t + 1) * Q_BLOCK)
            score = score + wrow[2 * t] * jnp.maximum(la[:, lanes], 0.0)
            score = score + wrow[2 * t + 1] * jnp.maximum(lb[:, lanes], 0.0)
        score = jnp.where(kpos_of(c) <= qpos, score, NEG)
        keys_ref[pl.ds(koff, KC), :] = _sort_key(score)
        return carry

    _loop_by_four(n_ch, score_body)

    def count(pred):
        def body(c, accs):
            kk = keys_ref[pl.ds(pl.multiple_of(c * KC, KC), KC), :]
            accs = list(accs)
            for r in range(KC // SUBLANES):
                a = accs[r % COUNT_ACCS]
                accs[r % COUNT_ACCS] = jnp.where(pred(kk[r * SUBLANES:(r + 1) * SUBLANES]), a + 1, a)
            return tuple(accs)
        accs = lax.fori_loop(0, n_ch, body, tuple(jnp.zeros((SUBLANES, Q_BLOCK), I32) for _ in range(COUNT_ACCS)))
        return jnp.sum(functools.reduce(lambda x, y: x + y, accs), axis=0, keepdims=True)

    def bis_body(it, v):
        cand = v + lax.shift_left(jnp.int32(1), 31 - it)
        cnt = count(lambda kk: kk >= cand)
        return jnp.where(cnt >= TOPK, cand, v)

    v = lax.fori_loop(0, 32, bis_body, jnp.full((1, Q_BLOCK), -2 ** 31, I32))
    n_gt = count(lambda kk: kk > v)
    n_ge = count(lambda kk: kk >= v)
    need = (TOPK - n_gt).astype(F32)
    any_split = jnp.max(jnp.where(n_ge > TOPK, 1, 0)) > 0

    @pl.when(any_split)
    def _():
        tri = jnp.where(lax.broadcasted_iota(I32, (KC, KC), 1) <= lax.broadcasted_iota(I32, (KC, KC), 0),
                        1.0, 0.0).astype(BF16)

        def tie_body(c, seen):
            koff = pl.multiple_of(c * KC, KC)
            kk = keys_ref[pl.ds(koff, KC), :]
            tie = kk == v
            tie_f = jnp.where(tie, 1.0, 0.0)
            rank = _dot(tri, tie_f.astype(BF16)) + seen
            keys_ref[pl.ds(koff, KC), :] = jnp.where(tie & (rank > need), v - 1, kk)
            return seen + jnp.sum(tie_f, axis=0, keepdims=True)

        lax.fori_loop(0, n_ch, tie_body, jnp.zeros((1, Q_BLOCK), F32))

    m_ref[...] = jnp.full(m_ref.shape, NEG, F32)
    acc_ref[...] = jnp.zeros(acc_ref.shape, F32)
    ones_rows = jnp.ones((ONES_ROWS, KA), BF16)

    def att_body(c, carry):
        koff = pl.multiple_of(c * KA, KA)
        kvc = kv_ref[0, pl.ds(koff, KA), :]
        kpos = c * KA + lax.broadcasted_iota(I32, (KA, 1), 0)
        bias = jnp.where((keys_ref[pl.ds(koff, KA), :] >= v) & (kpos <= qpos), 0.0, -jnp.inf)
        for hd in range(N_ATTN_HEADS):
            lanes = slice(hd * Q_BLOCK, (hd + 1) * Q_BLOCK)
            s = _nt_dot(kvc, q_ref[0, 0, hd * Q_BLOCK:(hd + 1) * Q_BLOCK, :]) + bias
            m_prev = m_ref[:, lanes]
            m_next = jnp.maximum(m_prev, jnp.max(s, axis=0, keepdims=True))
            a_ref[:, lanes] = jnp.exp2(m_prev - m_next)
            m_ref[:, lanes] = m_next
            p_ref[:, lanes] = jnp.exp2(s - m_next).astype(BF16)
        kvt1 = jnp.concatenate([kvt_ref[0, :, pl.ds(koff, KA)], ones_rows], axis=0)
        acc_ref[...] = acc_ref[...] * a_ref[...] + _dot(kvt1, p_ref[...])
        return carry

    _loop_by_four(((i + 1) * Q_BLOCK + KA - 1) // KA, att_body)
    o = (acc_ref[0:KV_RANK, :] / acc_ref[KV_RANK:KV_RANK + 1, :]).astype(BF16)
    ya_t = jnp.concatenate([_dot(wvt_ref[hd], o[:, hd * Q_BLOCK:(hd + 1) * Q_BLOCK])
                            for hd in range(N_ATTN_HEADS)], axis=0)
    ya_ref[0] = ya_t.T.astype(BF16)


def _dsa(q4, qi4, wt, kv, kvt, kidx2, wvt):
    B, nq = q4.shape[0], q4.shape[1]
    S = kv.shape[1]
    KC = KA = _tiles(S)["kc"]
    topk = min(TOPK_MAX, S // 4)
    assert S % KC == 0 and KC % KA == 0 and KA % Q_BLOCK == 0
    hq = N_ATTN_HEADS * Q_BLOCK
    return pl.pallas_call(
        functools.partial(_dsa_kernel, S, KC, KA, topk),
        out_shape=jax.ShapeDtypeStruct((B, S, ATTN_WIDTH), BF16),
        grid=(B, nq),
        in_specs=[
            pl.BlockSpec((1, 1) + q4.shape[2:], lambda b, i: (b, i, 0, 0)),
            pl.BlockSpec((1, 1) + qi4.shape[2:], lambda b, i: (b, i, 0, 0)),
            pl.BlockSpec((1, IDX_HEADS, Q_BLOCK), lambda b, i: (b, 0, i)),
            pl.BlockSpec((1, S, KV_RANK), lambda b, i: (b, 0, 0)),
            pl.BlockSpec((1, KV_RANK, S), lambda b, i: (b, 0, 0)),
            pl.BlockSpec((1, S, 2 * LANES), lambda b, i: (b, 0, 0)),
            pl.BlockSpec(wvt.shape, lambda b, i: (0, 0, 0)),
        ],
        out_specs=pl.BlockSpec((1, Q_BLOCK, ATTN_WIDTH), lambda b, i: (b, i, 0)),
        scratch_shapes=[
            pltpu.VMEM((S, Q_BLOCK), I32),
            pltpu.VMEM((KV_RANK + ONES_ROWS, hq), F32),
            pltpu.VMEM((1, hq), F32),
            pltpu.VMEM((1, hq), F32),
            pltpu.VMEM((KA, hq), BF16),
        ],
        compiler_params=_cparams(("arbitrary", "arbitrary")),
        name="dsa",
    )(q4, qi4, wt, kv, kvt, kidx2, wvt)


def _hgrn_kernel(TS, qs_ref, k_ref, v_ref, vt_ref, lf_ref, og_ref, grec_ref, y_ref,
                 st_ref, aloc_ref, e_ref, qh_ref, kh_ref, o_ref, g_ref, sb_ref, kblk_ref):
    n_tiles = REC_WIDTH // LANES
    nsb = TS // SUB

    @pl.when(pl.program_id(1) == 0)
    def _():
        st_ref[...] = jnp.zeros(st_ref.shape, F32)

    a = lf_ref[0]
    r = lax.broadcasted_iota(I32, a.shape, 0) % SUB
    for sh in (1, 2, 4, 8):
        a = a + jnp.where(r >= sh, pltpu.roll(a, sh, 0), 0.0)
    aend = jnp.where(r == SUB - 1, a, 0.0)
    for sh in (1, 2, 4, 8):
        aend = aend + jnp.where(r + sh <= SUB - 1, pltpu.roll(aend, TS - sh, 0), 0.0)
    e = jnp.exp(a)
    aloc_ref[...] = a * LOG2E
    e_ref[...] = e
    qh_ref[...] = (qs_ref[0].astype(F32) * e).astype(BF16)
    kh_ref[...] = (k_ref[0].astype(F32) * jnp.exp(aend - a)).astype(BF16)

    half = REC_K_DIM
    li = lax.broadcasted_iota(I32, (LANES, LANES), 0) // half
    lj = lax.broadcasted_iota(I32, (LANES, LANES), 1) // half
    bd_f = jnp.where(li == lj, 1.0, 0.0)
    bd_b = bd_f.astype(BF16)
    gsel = jnp.where(lax.broadcasted_iota(I32, (SUB, SUB * SUB), 0)
                     == lax.broadcasted_iota(I32, (SUB, SUB * SUB), 1) // SUB, 1.0, 0.0).astype(BF16)
    srow = lax.broadcasted_iota(I32, (SUB, LANES), 0)

    def diag_body(g, carry):
        units = []
        for jj in range(DIAG_GROUP):
            rows = pl.ds(pl.multiple_of((g * DIAG_GROUP + jj) * SUB, SUB), SUB)
            for p in range(n_tiles):
                units.append((rows, slice(p * LANES, (p + 1) * LANES)))
        wsts = []
        for rows, lanes in units:
            a_blk = aloc_ref[rows, lanes]
            q_blk = qs_ref[0, rows, lanes].astype(F32)
            k_blk = k_ref[0, rows, lanes].astype(F32)
            ws = []
            for t in range(SUB):
                d = a_blk[t:t + 1, :] - a_blk
                w = jnp.exp2(jnp.where(srow <= t, d, -jnp.inf)) * (k_blk * q_blk[t:t + 1, :])
                ws.append(w.astype(BF16))
            wsts.append(jnp.concatenate(ws, axis=0))
        rexps = [_dot(wst, bd_b) for wst in wsts]
        m2s = []
        for (rows, lanes), rexp in zip(units, rexps):
            v_f = v_ref[0, rows, lanes].astype(F32)
            m2s.append((rexp * jnp.concatenate([v_f] * SUB, axis=0)).astype(BF16))
        outs = [_dot(gsel, m2) for m2 in m2s]
        for (rows, lanes), out in zip(units, outs):
            o_ref[rows, lanes] = out
        return carry

    lax.fori_loop(0, nsb // DIAG_GROUP, diag_body, 0)

    @pl.when((pl.program_id(0) == 0) & (pl.program_id(1) == 0))
    def _():
        kblk_ref[...] = jnp.zeros(kblk_ref.shape, BF16)

    for p in range(n_tiles):
        lanes = slice(p * LANES, (p + 1) * LANES)
        for j in range(nsb):
            kblk_ref[j * SUB:(j + 1) * SUB, j * LANES:(j + 1) * LANES] = kh_ref[j * SUB:(j + 1) * SUB, lanes]
        g_ref[p] = _dot(vt_ref[0, lanes, :], kblk_ref[...])

    for p in range(n_tiles):
        lanes = slice(p * LANES, (p + 1) * LANES)
        st = st_ref[p]
        for j in range(nsb):
            sb_ref[j, p] = st.astype(BF16)
            dec = e_ref[(j + 1) * SUB - 1:(j + 1) * SUB, lanes]
            st = st * dec + g_ref[p, :, j * LANES:(j + 1) * LANES] * bd_f
        st_ref[p] = st

    for j in range(nsb):
        rows = slice(j * SUB, (j + 1) * SUB)
        for p in range(n_tiles):
            lanes = slice(p * LANES, (p + 1) * LANES)
            o_ref[rows, lanes] = o_ref[rows, lanes] + _nt_dot(qh_ref[rows, lanes], sb_ref[j, p])

    o = o_ref[...]
    o2 = o * o
    hi = o2.astype(BF16)
    lo = (o2 - hi.astype(F32)).astype(BF16)
    ms = jnp.concatenate(
        [_dot(hi[:, p * LANES:(p + 1) * LANES], bd_b) + _dot(lo[:, p * LANES:(p + 1) * LANES], bd_b)
         for p in range(n_tiles)], axis=1) * (1.0 / REC_V_DIM)
    y = o * lax.rsqrt(ms + EPS) * grec_ref[...]
    y_ref[0] = (y * og_ref[0].astype(F32)).astype(BF16)


def _hgrn(qs, k, v, vt, lf, og, grec_t, ts):
    B, S, W = qs.shape
    nsb = ts // SUB
    n_tiles = W // LANES
    tok = lambda: pl.BlockSpec((1, ts, W), lambda b, i: (b, i, 0))
    return pl.pallas_call(
        functools.partial(_hgrn_kernel, ts),
        out_shape=jax.ShapeDtypeStruct((B, S, W), BF16),
        grid=(B, S // ts),
        in_specs=[tok(), tok(), tok(), pl.BlockSpec((1, W, ts), lambda b, i: (b, 0, i)), tok(), tok(),
                  pl.BlockSpec((1, W), lambda b, i: (0, 0))],
        out_specs=tok(),
        scratch_shapes=[
            pltpu.VMEM((n_tiles, LANES, LANES), F32),
            pltpu.VMEM((ts, W), F32),
            pltpu.VMEM((ts, W), F32),
            pltpu.VMEM((ts, W), BF16),
            pltpu.VMEM((ts, W), BF16),
            pltpu.VMEM((ts, W), F32),
            pltpu.VMEM((n_tiles, LANES, nsb * LANES), F32),
            pltpu.VMEM((nsb, n_tiles, LANES, LANES), BF16),
            pltpu.VMEM((ts, nsb * LANES), BF16),
        ],
        compiler_params=_cparams(("arbitrary", "arbitrary")),
        name="hgrn",
    )(qs, k, v, vt, lf, og, grec_t)


def _merge_kernel(x_ref, ya_ref, yr_ref, sga_ref, sgr_ref, gt1_ref, sh2_ref, sc2_ref, g2_ref,
                  wba_ref, wbr_ref, wo_ref, wrh_ref, wrl_ref, br_ref, x1_out, h2_out, rt_out, cnt_out):
    tm = x_ref.shape[1]
    groups = [slice(h * tm // MERGE_SPLIT, (h + 1) * tm // MERGE_SPLIT) for h in range(MERGE_SPLIT)]
    da = [_dot(ya_ref[0, r, :], wba_ref[...]) for r in groups]
    dr = [_dot(yr_ref[0, r, :], wbr_ref[...]) for r in groups]
    merged = [(sga_ref[0, r, :].astype(F32) * a + sgr_ref[0, r, :].astype(F32) * b).astype(BF16)
              for r, a, b in zip(groups, da, dr)]
    dout = [_dot(m, wo_ref[...]) for m in merged]
    x1s = [x_ref[0, r, :] + gt1_ref[0] * d for r, d in zip(groups, dout)]
    h2s = [(_rms(x1) * g2_ref[...]) * (1.0 + sc2_ref[0]) + sh2_ref[0] for x1 in x1s]
    lgs = [_split_dot(h2, wrh_ref[...], wrl_ref[...]) for h2 in h2s]
    x1_out[0] = jnp.concatenate(x1s, axis=0)
    _to_token_tiles(h2_out.at[0], jnp.concatenate(h2s, axis=0))

    lg = jnp.concatenate(lgs, axis=0) + br_ref[...]
    lane = lax.broadcasted_iota(I32, lg.shape, 1)
    big = jnp.int32(1 << 20)
    gl = jnp.where(lane < N_GROUPS, lg, -jnp.inf)
    gmax = jnp.max(gl, axis=1, keepdims=True)
    gsel = jnp.min(jnp.where(gl == gmax, lane, big), axis=1, keepdims=True)
    ggate = 1.0 / jnp.sum(jnp.exp(gl - gmax), axis=1, keepdims=True)
    lo = N_GROUPS + EXPERTS_PER_GROUP * gsel
    el = jnp.where((lane >= lo) & (lane < lo + EXPERTS_PER_GROUP), lg, -jnp.inf)
    v1 = jnp.max(el, axis=1, keepdims=True)
    i1 = jnp.min(jnp.where(el == v1, lane, big), axis=1, keepdims=True)
    el2 = jnp.where(lane == i1, -jnp.inf, el)
    v2 = jnp.max(el2, axis=1, keepdims=True)
    i2 = jnp.min(jnp.where(el2 == v2, lane, big), axis=1, keepdims=True)
    e2 = jnp.exp(v2 - v1)
    den = 1.0 + e2
    g1 = (1.0 / den) * ggate
    g2 = (e2 / den) * ggate
    oh1 = jnp.where(lane == i1 - N_GROUPS, 1.0, 0.0)
    oh2 = jnp.where(lane == i2 - N_GROUPS, 1.0, 0.0)
    ohs = (oh1 + oh2).astype(BF16)
    ltri = jnp.where(lax.broadcasted_iota(I32, (tm, tm), 1) < lax.broadcasted_iota(I32, (tm, tm), 0),
                     1.0, 0.0).astype(BF16)
    before = _dot(ltri, ohs)
    r1 = jnp.sum(before * oh1, axis=1, keepdims=True)
    r2 = jnp.sum(before * oh2, axis=1, keepdims=True)
    cnt_out[0, 0] = jnp.broadcast_to(jnp.sum(oh1 + oh2, axis=0, keepdims=True), (SUBLANES, LANES))
    vals = ((i1 - N_GROUPS).astype(F32), (i2 - N_GROUPS).astype(F32), g1, g2, r1, r2)
    rt = jnp.zeros(lg.shape, F32)
    for j, val in enumerate(vals):
        rt = jnp.where(lane == j, val, rt)
    rt_out[0] = rt


def _merge(x, ya, yr, sga, sgr, gt1, sh2, sc2, g2, wba, wbr, wo, wrh, wrl, br, tm):
    B, S, D = x.shape
    tok = lambda w: pl.BlockSpec((1, tm, w), lambda b, i: (b, i, 0))
    vec = lambda: pl.BlockSpec((1, 1, D), lambda b, i: (b, 0, 0))
    full = lambda a: pl.BlockSpec(a.shape, lambda b, i: (0,) * a.ndim)
    return pl.pallas_call(
        _merge_kernel,
        out_shape=(jax.ShapeDtypeStruct((B, S, D), F32), jax.ShapeDtypeStruct((B, S * SUBLANES, LANES), F32),
                   jax.ShapeDtypeStruct((B, S, LANES), F32),
                   jax.ShapeDtypeStruct((B, S // tm, SUBLANES, LANES), F32)),
        grid=(B, S // tm),
        in_specs=[tok(D), tok(ATTN_WIDTH), tok(REC_WIDTH), tok(D), tok(D), vec(), vec(), vec(), full(g2),
                  full(wba), full(wbr), full(wo), full(wrh), full(wrl), full(br)],
        out_specs=(tok(D), pl.BlockSpec((1, tm * SUBLANES, LANES), lambda b, i: (b, i, 0)), tok(LANES),
                   pl.BlockSpec((1, 1, SUBLANES, LANES), lambda b, i: (b, i, 0, 0))),
        compiler_params=_cparams(("arbitrary", "arbitrary")),
        name="merge",
    )(x, ya, yr, sga, sgr, gt1, sh2, sc2, g2, wba, wbr, wo, wrh, wrl, br)


def _to_token_tiles(ref, x):
    n = x.shape[0]
    for s in range(SUBLANES):
        ref[pl.ds(s, n, stride=SUBLANES), :] = x[:, s * LANES:(s + 1) * LANES]


def _from_token_tiles(ref, first_tile, n):
    return jnp.concatenate([ref[pl.ds(first_tile * SUBLANES + s, n, stride=SUBLANES), :]
                            for s in range(SUBLANES)], axis=1)


def _tile_gather(src_hbm, idx_ref, dst, sem, n):
    def body(r, carry):
        src = src_hbm.at[pl.ds(pl.multiple_of(idx_ref[0, 0, r] * SUBLANES, SUBLANES), SUBLANES)]
        pltpu.make_async_copy(src, dst.at[pl.ds(pl.multiple_of(r * SUBLANES, SUBLANES), SUBLANES)], sem).start()
        return carry
    lax.fori_loop(0, n, body, 0, unroll=8)


def _tile_gather_wait(src_hbm, dst, sem, n):
    pltpu.make_async_copy(src_hbm.at[pl.ds(0, n * SUBLANES)], dst, sem).wait()


def _scatter_kernel(TM, d_ref, h2_ref, xs_in, xs_out, sem):
    del xs_in

    for j in range(TOP_K_IN_GROUP):
        def body(t, carry):
            src = h2_ref.at[pl.ds(pl.multiple_of(t * SUBLANES, SUBLANES), SUBLANES)]
            dst = xs_out.at[pl.ds(pl.multiple_of(d_ref[0, 0, j * TM + t] * SUBLANES, SUBLANES), SUBLANES)]
            pltpu.make_async_copy(src, dst, sem).start()
            return carry
        lax.fori_loop(0, TM, body, 0, unroll=8)
    for _ in range(TOP_K_IN_GROUP):
        pltpu.make_async_copy(h2_ref, xs_out.at[pl.ds(0, TM * SUBLANES)], sem).wait()


def _scatter_rows(dest3, h2t, n_rows, tm):
    n = dest3.shape[0]
    xs0 = jnp.zeros((n_rows * SUBLANES, LANES), F32)
    return pl.pallas_call(
        functools.partial(_scatter_kernel, tm),
        out_shape=jax.ShapeDtypeStruct(xs0.shape, F32),
        grid=(n,),
        in_specs=[
            pl.BlockSpec((1, 1, TOP_K_IN_GROUP * tm), lambda i: (i, 0, 0), memory_space=pltpu.SMEM),
            pl.BlockSpec((tm * SUBLANES, LANES), lambda i: (i, 0)),
            pl.BlockSpec(memory_space=pl.ANY),
        ],
        out_specs=pl.BlockSpec(memory_space=pl.ANY),
        scratch_shapes=[pltpu.SemaphoreType.DMA(())],
        input_output_aliases={2: 0},
        compiler_params=_cparams(("arbitrary",)),
        name="scatter_rows",
    )(dest3, h2t, xs0)


def _moe_kernel(be_ref, x_ref, wg_ref, wu_ref, wd_ref, y_ref, wgb, wub, wdb):
    i = pl.program_id(0)

    @pl.when((i == 0) | (be_ref[i] != be_ref[jnp.maximum(i - 1, 0)]))
    def _():
        wgb[...] = wg_ref[0, 0].astype(BF16)
        wub[...] = wu_ref[0, 0].astype(BF16)
        wdb[...] = wd_ref[0, 0].astype(BF16)

    xb = _from_token_tiles(x_ref, 0, x_ref.shape[0] // SUBLANES).astype(BF16)
    g = _dot(xb, wgb[...])
    hid = (g * jax.nn.sigmoid(g)) * _dot(xb, wub[...])
    _to_token_tiles(y_ref, _dot(hid.astype(BF16), wdb[...]))


def _moe(layer, blk_expert, xs, wgate, wup, wdown, blk):
    n_blk = blk_expert.shape[0]
    D = wgate.shape[2]
    rows = pl.BlockSpec((blk * SUBLANES, LANES), lambda i, be: (i, 0))
    grid_spec = pltpu.PrefetchScalarGridSpec(
        num_scalar_prefetch=1,
        grid=(n_blk,),
        in_specs=[
            rows,
            pl.BlockSpec((1, 1, D, D_EXPERT), lambda i, be: (layer, be[i], 0, 0)),
            pl.BlockSpec((1, 1, D, D_EXPERT), lambda i, be: (layer, be[i], 0, 0)),
            pl.BlockSpec((1, 1, D_EXPERT, D), lambda i, be: (layer, be[i], 0, 0)),
        ],
        out_specs=rows,
        scratch_shapes=[pltpu.VMEM((D, D_EXPERT), BF16), pltpu.VMEM((D, D_EXPERT), BF16),
                        pltpu.VMEM((D_EXPERT, D), BF16)],
    )
    return pl.pallas_call(
        _moe_kernel,
        out_shape=jax.ShapeDtypeStruct(xs.shape, F32),
        grid_spec=grid_spec,
        compiler_params=_cparams(("arbitrary",)),
        name="moe",
    )(blk_expert, xs, wgate, wup, wdown)


def _comb_kernel(TM, final, d_ref, dn_ref, x1_ref, rt_ref, gt2_ref, gf_ref, ys_hbm, o_ref, buf, sem):
    i = pl.program_id(0)
    n = pl.num_programs(0)
    slot = lax.rem(i, 2)

    @pl.when(i == 0)
    def _():
        _tile_gather(ys_hbm, d_ref, buf.at[0], sem.at[0], 2 * TM)

    @pl.when(i + 1 < n)
    def _():
        _tile_gather(ys_hbm, dn_ref, buf.at[1 - slot], sem.at[1 - slot], 2 * TM)

    _tile_gather_wait(ys_hbm, buf.at[slot], sem.at[slot], 2 * TM)
    rt = rt_ref[...]
    y0 = _from_token_tiles(buf.at[slot], 0, TM)
    y1 = _from_token_tiles(buf.at[slot], TM, TM)
    mix = rt[:, TOP_K_IN_GROUP:TOP_K_IN_GROUP + 1] * y0 + rt[:, TOP_K_IN_GROUP + 1:TOP_K_IN_GROUP + 2] * y1
    x2 = x1_ref[...] + gt2_ref[0] * mix
    if final:
        x2 = _rms(x2) * gf_ref[...]
    o_ref[...] = x2


def _combine(dest3, x1, route, gt2, gf, ys, S, tm, final):
    T, D = x1.shape
    n = T // tm
    per_b = S // tm
    return pl.pallas_call(
        functools.partial(_comb_kernel, tm, final),
        out_shape=jax.ShapeDtypeStruct((T, D), F32),
        grid=(n,),
        in_specs=[
            pl.BlockSpec((1, 1, 2 * tm), lambda i: (i, 0, 0), memory_space=pltpu.SMEM),
            pl.BlockSpec((1, 1, 2 * tm), lambda i: (jnp.minimum(i + 1, n - 1), 0, 0), memory_space=pltpu.SMEM),
            pl.BlockSpec((tm, D), lambda i: (i, 0)),
            pl.BlockSpec((tm, LANES), lambda i: (i, 0)),
            pl.BlockSpec((1, 1, D), lambda i: (i // per_b, 0, 0)),
            pl.BlockSpec((1, D), lambda i: (0, 0)),
            pl.BlockSpec(memory_space=pl.ANY),
        ],
        out_specs=pl.BlockSpec((tm, D), lambda i: (i, 0)),
        scratch_shapes=[pltpu.VMEM((2, 2 * tm * SUBLANES, LANES), F32), pltpu.SemaphoreType.DMA((2,))],
        compiler_params=_cparams(("arbitrary",)),
        name="combine",
    )(dest3, dest3, x1, route, gt2, gf, ys)


def _dispatch(route, tile_cnt, blk):
    T = route.shape[0]
    N = T * TOP_K_IN_GROUP
    n_tiles = tile_cnt.shape[0]
    eid = route[:, 0:TOP_K_IN_GROUP].astype(I32).reshape(N)
    rank_in_tile = route[:, 2 * TOP_K_IN_GROUP:3 * TOP_K_IN_GROUP].astype(I32).reshape(N)
    tile_start = jnp.cumsum(tile_cnt, axis=0) - tile_cnt
    counts = jnp.sum(tile_cnt, axis=0)
    padded = (counts + blk - 1) // blk * blk
    pad_end = jnp.cumsum(padded)
    pad_start = pad_end - padded
    base = jnp.repeat(pad_start[None, :] + tile_start, N // n_tiles, axis=0)
    hit = eid[:, None] == jnp.arange(N_EXPERTS, dtype=I32)[None, :]
    dest = jnp.sum(jnp.where(hit, base, 0), axis=1) + rank_in_tile
    n_blk = (N + N_EXPERTS * blk) // blk
    blk_start = jnp.arange(n_blk, dtype=I32)[:, None] * blk
    blk_expert = jnp.minimum(jnp.sum((pad_end[None, :] <= blk_start).astype(I32), axis=1), N_EXPERTS - 1)
    return blk_expert, dest.reshape(T, TOP_K_IN_GROUP)


def _tile_plan(dest, tm):
    T = dest.shape[0]
    return dest.reshape(T // tm, tm, TOP_K_IN_GROUP).transpose(0, 2, 1).reshape(T // tm, 1, TOP_K_IN_GROUP * tm)


def kernel(x, c, w_mod, b_mod, g_norm1, g_norm2, w_in, g_cq, g_ckv, g_kidx, w_q_up, w_idx_q, w_v_up,
           lb_logits, g_rec, w_branch_a, w_branch_r, w_out, w_grp, b_grp, w_exp_router, b_exp_router,
           w_gate, w_up, w_down, g_final):
    B, S, D = x.shape
    L = w_mod.shape[0]
    T = B * S
    tiles = _tiles(S)
    tm, ts, tmc, blk = tiles["tm"], tiles["ts"], tiles["tmc"], tiles["blk"]
    kw = N_REC_HEADS * REC_K_DIM
    na = Q_RANK + KV_RANK + IDX_DIM + IDX_HEADS
    n_rec = 2 * kw + 2 * REC_WIDTH

    mod = _modulation(c, w_mod, b_mod)
    for l in range(L):
        m6 = mod[l].reshape(B, 6, 1, D)
        sh1, sc1, gt1, sh2, sc2, gt2 = (m6[:, j] for j in range(6))
        wa = jnp.pad(w_in[l, :, :na], ((0, 0), (0, 512 - na))).astype(BF16)
        wrec = w_in[l, :, na:na + n_rec].astype(BF16)
        wg = w_in[l, :, na + n_rec:].astype(BF16)
        gk = jnp.pad(g_kidx[l], (0, LANES - IDX_DIM)).reshape(1, LANES)
        (q4, qi4, kv, kvt, kidx2, wt, qs, kk, vv, vt, lf, og, sga, sgr) = _proj(
            l, x, sh1, sc1, g_norm1[l].reshape(1, D), wa, wrec, wg, g_cq[l].reshape(1, -1),
            g_ckv[l].reshape(1, -1), gk, w_q_up[l].astype(BF16), w_idx_q[l].astype(BF16), lb_logits, tm)
        wvt = jnp.swapaxes(w_v_up[l], 1, 2).astype(BF16)
        ya = _dsa(q4, qi4, wt, kv, kvt, kidx2, wvt)
        yr = _hgrn(qs, kk, vv, vt, lf, og, jnp.tile(g_rec[l], N_REC_HEADS).reshape(1, REC_WIDTH), ts)
        wr = jnp.pad(jnp.concatenate([w_grp[l], w_exp_router[l]], axis=1),
                     ((0, 0), (0, LANES - N_GROUPS - N_EXPERTS)))
        wrh, wrl = _hi_lo(wr)
        br = jnp.pad(jnp.concatenate([b_grp[l], b_exp_router[l]]), (0, LANES - N_GROUPS - N_EXPERTS)).reshape(1, LANES)
        x1, h2t, route, tile_cnt = _merge(x, ya, yr, sga, sgr, gt1, sh2, sc2, g_norm2[l].reshape(1, D),
                               w_branch_a[l].astype(BF16), w_branch_r[l].astype(BF16), w_out[l].astype(BF16),
                               wrh, wrl, br, tm)
        route = route.reshape(T, LANES)
        tile_cnt = tile_cnt[:, :, 0, :N_EXPERTS].reshape(-1, N_EXPERTS).astype(I32)
        blk_expert, dest = _dispatch(route, tile_cnt, blk)
        xs = _scatter_rows(_tile_plan(dest, tm), h2t.reshape(T * SUBLANES, LANES),
                           TOP_K_IN_GROUP * T + N_EXPERTS * blk, tm)
        ys = _moe(l, blk_expert, xs, w_gate, w_up, w_down, blk)
        x = _combine(_tile_plan(dest, tmc), x1.reshape(T, D), route, gt2, g_final.reshape(1, D), ys, S, tmc,
                     l == L - 1).reshape(B, S, D)
    return x
```

```python
import functools

import jax
import jax.numpy as jnp
from jax import lax
from jax.experimental import pallas as pl
from jax.experimental.pallas import tpu as pltpu

N_ATTN_HEADS = 8
Q_RANK = 256
KV_RANK = 128
ATTN_V_DIM = 64
ATTN_WIDTH = N_ATTN_HEADS * ATTN_V_DIM
ATTN_SCALE = KV_RANK ** -0.5
IDX_HEADS = 8
IDX_DIM = 64
IDX_W_SCALE = (IDX_HEADS * IDX_DIM) ** -0.5
TOPK_MAX = 256
N_REC_HEADS = 8
REC_K_DIM = 64
REC_V_DIM = 64
REC_WIDTH = N_REC_HEADS * REC_V_DIM
N_GROUPS = 4
EXPERTS_PER_GROUP = 8
N_EXPERTS = N_GROUPS * EXPERTS_PER_GROUP
TOP_K_IN_GROUP = 2
D_EXPERT = 512
EPS = 1e-6
NEG = -1e30
TINY = 1e-30

LANES = 128
SUBLANES = 8
Q_BLOCK = 128
COUNT_ACCS = 8
ONES_ROWS = 16
LOG2E = 1.4426950408889634
SUB = 16
DIAG_GROUP = 4
MERGE_SPLIT = 2
VMEM_LIMIT = 56 * 1024 * 1024


def _tiles(S):
    return dict(
        tm=min(512, S),
        ts=min(256, S),
        tmc=min(128, S),
        kc=min(512, S),
        blk=512,
    )

F32 = jnp.float32
BF16 = jnp.bfloat16
I32 = jnp.int32


def _cparams(sem):
    return pltpu.CompilerParams(dimension_semantics=sem, vmem_limit_bytes=VMEM_LIMIT)


def _nt_dot(a, b):
    return lax.dot_general(a, b, (((1,), (1,)), ((), ())), preferred_element_type=F32)


def _dot(a, b):
    return jnp.dot(a, b, preferred_element_type=F32)


def _split_dot(a_f32, b_hi, b_lo):
    a_hi = a_f32.astype(BF16)
    a_lo = (a_f32 - a_hi.astype(F32)).astype(BF16)
    return _dot(a_hi, b_hi) + (_dot(a_hi, b_lo) + _dot(a_lo, b_hi))


def _hi_lo(w):
    hi = w.astype(BF16)
    lo = (w - hi.astype(F32)).astype(BF16)
    return hi, lo


def _mod_kernel(c_ref, w_ref, b_ref, o_ref):
    c = c_ref[...]
    ca = c * jax.nn.sigmoid(c)
    w = w_ref[0]
    w_hi = w.astype(BF16)
    w_lo = (w - w_hi.astype(F32)).astype(BF16)
    o_ref[0] = _split_dot(ca, w_hi, w_lo) + b_ref[0]


def _modulation(c, w_mod, b_mod):
    L, D, D6 = w_mod.shape
    B = c.shape[0]
    tn = 1024
    return pl.pallas_call(
        _mod_kernel,
        out_shape=jax.ShapeDtypeStruct((L, B, D6), F32),
        grid=(L, D6 // tn),
        in_specs=[
            pl.BlockSpec((B, D), lambda l, n: (0, 0)),
            pl.BlockSpec((1, D, tn), lambda l, n: (l, 0, n)),
            pl.BlockSpec((1, 1, tn), lambda l, n: (l, 0, n)),
        ],
        out_specs=pl.BlockSpec((1, B, tn), lambda l, n: (l, 0, n)),
        compiler_params=_cparams(("arbitrary", "arbitrary")),
        name="modulation",
    )(c, w_mod, b_mod.reshape(L, 1, D6))


def _rms(x, eps=EPS):
    return x * lax.rsqrt(jnp.mean(x * x, axis=-1, keepdims=True) + eps)


def _proj_kernel(layer, x_ref, sh_ref, sc_ref, g1_ref, wa_ref, wrec_ref, wg_ref, gcq_ref, gckv_ref,
                 gk_ref, wq_ref, wiq_ref, lb_ref,
                 q_out, qi_out, kv_out, kvt_out, kidx_out, wt_out, qs_out, k_out, v_out, vt_out, lf_out, og_out,
                 sga_out, sgr_out):
    x = x_ref[0]
    tm = x.shape[0]
    h = (_rms(x) * g1_ref[...]) * (1.0 + sc_ref[0]) + sh_ref[0]
    hb = h.astype(BF16)

    ua = _dot(hb, wa_ref[...])
    cq = (_rms(ua[:, :Q_RANK]) * gcq_ref[...]).astype(BF16)
    q = (_dot(cq, wq_ref[...]) * (ATTN_SCALE * LOG2E)).astype(BF16)
    qi = _dot(cq, wiq_ref[...]).astype(BF16)
    for qb in range(tm // Q_BLOCK):
        rows = slice(qb * Q_BLOCK, (qb + 1) * Q_BLOCK)
        for hd in range(N_ATTN_HEADS):
            q_out[0, qb, hd * Q_BLOCK:(hd + 1) * Q_BLOCK, :] = q[rows, hd * LANES:(hd + 1) * LANES]
        for t in range(IDX_HEADS * IDX_DIM // LANES):
            qi_out[0, qb, t * Q_BLOCK:(t + 1) * Q_BLOCK, :] = qi[rows, t * LANES:(t + 1) * LANES]
    kvn = _rms(ua[:, Q_RANK:Q_RANK + KV_RANK]) * gckv_ref[...]
    kv_out[0] = kvn.astype(BF16)
    kvt_out[0] = kvn.T.astype(BF16)

    t3 = ua[:, Q_RANK + KV_RANK:]
    lane = lax.broadcasted_iota(I32, t3.shape, 1)
    ms = jnp.sum(jnp.where(lane < IDX_DIM, t3 * t3, 0.0), axis=-1, keepdims=True) * (1.0 / IDX_DIM)
    ka = t3 * lax.rsqrt(ms + EPS) * gk_ref[...]
    kb = pltpu.roll(ka, IDX_DIM, 1)
    kidx_out[0] = jnp.concatenate([ka, kb], axis=1).astype(BF16)
    wt_out[0] = (t3 * IDX_W_SCALE).T[IDX_DIM:IDX_DIM + IDX_HEADS, :]

    ur = _dot(hb, wrec_ref[...])
    kw = N_REC_HEADS * REC_K_DIM
    lbl = lb_ref[...]
    e = jnp.exp(lbl - jnp.max(lbl, axis=0, keepdims=True))
    p = e / jnp.sum(e, axis=0, keepdims=True)
    lb = jnp.zeros((1, kw), F32)
    for j in range(1, layer + 1):
        lb = lb + p[j:j + 1, :]
    lb = jnp.clip(lb, 0.0, 1.0)
    sig = jax.nn.sigmoid(ur[:, kw:2 * kw])
    f = lb + (1.0 - lb) * sig
    lf_out[0] = jnp.log(jnp.maximum(f, TINY))
    k_out[0] = ((1.0 - lb) * (1.0 - sig)).astype(BF16)
    qr = ur[:, :kw]
    qs_out[0] = (qr * jax.nn.sigmoid(qr)).astype(BF16)
    vr = ur[:, 2 * kw:2 * kw + REC_WIDTH]
    v_out[0] = vr.astype(BF16)
    vt_out[0] = vr.T.astype(BF16)
    og = ur[:, 2 * kw + REC_WIDTH:]
    og_out[0] = (og * jax.nn.sigmoid(og)).astype(BF16)

    ug = _dot(hb, wg_ref[...])
    D = x.shape[1]
    sga_out[0] = jax.nn.sigmoid(ug[:, :D]).astype(BF16)
    sgr_out[0] = jax.nn.sigmoid(ug[:, D:]).astype(BF16)


def _proj(layer, x, sh1, sc1, g1, wa, wrec, wg, gcq, gckv, gk, wq, wiq, lb_logits, tm):
    B, S, D = x.shape
    nqb = tm // Q_BLOCK
    nq = S // Q_BLOCK
    kw = N_REC_HEADS * REC_K_DIM
    c2 = lambda b, i: (0, 0)
    tok = lambda w: pl.BlockSpec((1, tm, w), lambda b, i: (b, i, 0))
    full = lambda a: pl.BlockSpec(a.shape, c2)
    out_shapes = (
        jax.ShapeDtypeStruct((B, nq, N_ATTN_HEADS * Q_BLOCK, KV_RANK), BF16),
        jax.ShapeDtypeStruct((B, nq, IDX_HEADS * IDX_DIM // LANES * Q_BLOCK, LANES), BF16),
        jax.ShapeDtypeStruct((B, S, KV_RANK), BF16),
        jax.ShapeDtypeStruct((B, KV_RANK, S), BF16),
        jax.ShapeDtypeStruct((B, S, 2 * LANES), BF16),
        jax.ShapeDtypeStruct((B, IDX_HEADS, S), F32),
        jax.ShapeDtypeStruct((B, S, kw), BF16),
        jax.ShapeDtypeStruct((B, S, kw), BF16),
        jax.ShapeDtypeStruct((B, S, REC_WIDTH), BF16),
        jax.ShapeDtypeStruct((B, REC_WIDTH, S), BF16),
        jax.ShapeDtypeStruct((B, S, kw), F32),
        jax.ShapeDtypeStruct((B, S, REC_WIDTH), BF16),
        jax.ShapeDtypeStruct((B, S, D), BF16),
        jax.ShapeDtypeStruct((B, S, D), BF16),
    )
    out_specs = (
        pl.BlockSpec((1, nqb, N_ATTN_HEADS * Q_BLOCK, KV_RANK), lambda b, i: (b, i, 0, 0)),
        pl.BlockSpec((1, nqb, IDX_HEADS * IDX_DIM // LANES * Q_BLOCK, LANES), lambda b, i: (b, i, 0, 0)),
        tok(KV_RANK),
        pl.BlockSpec((1, KV_RANK, tm), lambda b, i: (b, 0, i)),
        tok(2 * LANES),
        pl.BlockSpec((1, IDX_HEADS, tm), lambda b, i: (b, 0, i)),
        tok(kw), tok(kw), tok(REC_WIDTH),
        pl.BlockSpec((1, REC_WIDTH, tm), lambda b, i: (b, 0, i)),
        tok(kw), tok(REC_WIDTH),
        tok(D), tok(D),
    )
    vec = lambda: pl.BlockSpec((1, 1, D), lambda b, i: (b, 0, 0))
    return pl.pallas_call(
        functools.partial(_proj_kernel, layer),
        out_shape=out_shapes,
        grid=(B, S // tm),
        in_specs=[tok(D), vec(), vec(), full(g1), full(wa), full(wrec), full(wg), full(gcq), full(gckv),
                  full(gk), full(wq), full(wiq), full(lb_logits)],
        out_specs=out_specs,
        compiler_params=_cparams(("arbitrary", "arbitrary")),
        name="proj",
    )(x, sh1, sc1, g1, wa, wrec, wg, gcq, gckv, gk, wq, wiq, lb_logits)


def _loop_by_four(n, body):
    def quad(p, carry):
        for j in range(4):
            body(4 * p + j, carry)
        return carry
    lax.fori_loop(0, n // 4, quad, 0)
    done = n // 4 * 4

    @pl.when(n % 4 >= 2)
    def _():
        body(done, 0)
        body(done + 1, 0)

    @pl.when(n % 2 == 1)
    def _():
        body(n - 1, 0)


def _sort_key(score):
    bits = lax.bitcast_convert_type(score, I32)
    return bits ^ (lax.shift_right_arithmetic(bits, 31) & jnp.int32(0x7FFFFFFF))


def _dsa_kernel(S, KC, KA, TOPK, q_ref, qi_ref, wt_ref, kv_ref, kvt_ref, kidx_ref, wvt_ref, ya_ref,
                keys_ref, acc_ref, m_ref, a_ref, p_ref):
    i = pl.program_id(1)
    n_ch = ((i + 1) * Q_BLOCK + KC - 1) // KC
    qpos = i * Q_BLOCK + lax.broadcasted_iota(I32, (1, Q_BLOCK), 1)
    wt = wt_ref[0]
    wrow = [wt[hd:hd + 1, :] for hd in range(IDX_HEADS)]
    qi = qi_ref[0, 0]
    n_t = IDX_HEADS * IDX_DIM // LANES

    def kpos_of(c):
        return c * KC + lax.broadcasted_iota(I32, (KC, 1), 0)

    def score_body(c, carry):
        koff = pl.multiple_of(c * KC, KC)
        la = _nt_dot(kidx_ref[0, pl.ds(koff, KC), 0:LANES], qi)
        lb = _nt_dot(kidx_ref[0, pl.ds(koff, KC), LANES:2 * LANES], qi)
        score = jnp.zeros((KC, Q_BLOCK), F32)
        for t in range(n_t):
            lanes = slice(t * Q_BLOCK, (t + 1) * Q_BLOCK)
            score = score + wrow[2 * t] * jnp.maximum(la[:, lanes], 0.0)
            score = score + wrow[2 * t + 1] * jnp.maximum(lb[:, lanes], 0.0)
        score = jnp.where(kpos_of(c) <= qpos, score, NEG)
        keys_ref[pl.ds(koff, KC), :] = _sort_key(score)
        return carry

    _loop_by_four(n_ch, score_body)

    def count(pred):
        def body(c, accs):
            kk = keys_ref[pl.ds(pl.multiple_of(c * KC, KC), KC), :]
            accs = list(accs)
            for r in range(KC // SUBLANES):
                a = accs[r % COUNT_ACCS]
                accs[r % COUNT_ACCS] = jnp.where(pred(kk[r * SUBLANES:(r + 1) * SUBLANES]), a + 1, a)
            return tuple(accs)
        accs = lax.fori_loop(0, n_ch, body, tuple(jnp.zeros((SUBLANES, Q_BLOCK), I32) for _ in range(COUNT_ACCS)))
        return jnp.sum(functools.reduce(lambda x, y: x + y, accs), axis=0, keepdims=True)

    def bis_body(it, v):
        cand = v + lax.shift_left(jnp.int32(1), 31 - it)
        cnt = count(lambda kk: kk >= cand)
        return jnp.where(cnt >= TOPK, cand, v)

    v = lax.fori_loop(0, 32, bis_body, jnp.full((1, Q_BLOCK), -2 ** 31, I32))
    n_gt = count(lambda kk: kk > v)
    n_ge = count(lambda kk: kk >= v)
    need = (TOPK - n_gt).astype(F32)
    any_split = jnp.max(jnp.where(n_ge > TOPK, 1, 0)) > 0

    @pl.when(any_split)
    def _():
        tri = jnp.where(lax.broadcasted_iota(I32, (KC, KC), 1) <= lax.broadcasted_iota(I32, (KC, KC), 0),
                        1.0, 0.0).astype(BF16)

        def tie_body(c, seen):
            koff = pl.multiple_of(c * KC, KC)
            kk = keys_ref[pl.ds(koff, KC), :]
            tie = kk == v
            tie_f = jnp.where(tie, 1.0, 0.0)
            rank = _dot(tri, tie_f.astype(BF16)) + seen
            keys_ref[pl.ds(koff, KC), :] = jnp.where(tie & (rank > need), v - 1, kk)
            return seen + jnp.sum(tie_f, axis=0, keepdims=True)

        lax.fori_loop(0, n_ch, tie_body, jnp.zeros((1, Q_BLOCK), F32))

    m_ref[...] = jnp.full(m_ref.shape, NEG, F32)
    acc_ref[...] = jnp.zeros(acc_ref.shape, F32)
    ones_rows = jnp.ones((ONES_ROWS, KA), BF16)

    def att_body(c, carry):
        koff = pl.multiple_of(c * KA, KA)
        kvc = kv_ref[0, pl.ds(koff, KA), :]
        kpos = c * KA + lax.broadcasted_iota(I32, (KA, 1), 0)
        bias = jnp.where((keys_ref[pl.ds(koff, KA), :] >= v) & (kpos <= qpos), 0.0, -jnp.inf)
        for hd in range(N_ATTN_HEADS):
            lanes = slice(hd * Q_BLOCK, (hd + 1) * Q_BLOCK)
            s = _nt_dot(kvc, q_ref[0, 0, hd * Q_BLOCK:(hd + 1) * Q_BLOCK, :]) + bias
            m_prev = m_ref[:, lanes]
            m_next = jnp.maximum(m_prev, jnp.max(s, axis=0, keepdims=True))
            a_ref[:, lanes] = jnp.exp2(m_prev - m_next)
            m_ref[:, lanes] = m_next
            p_ref[:, lanes] = jnp.exp2(s - m_next).astype(BF16)
        kvt1 = jnp.concatenate([kvt_ref[0, :, pl.ds(koff, KA)], ones_rows], axis=0)
        acc_ref[...] = acc_ref[...] * a_ref[...] + _dot(kvt1, p_ref[...])
        return carry

    _loop_by_four(((i + 1) * Q_BLOCK + KA - 1) // KA, att_body)
    o = (acc_ref[0:KV_RANK, :] / acc_ref[KV_RANK:KV_RANK + 1, :]).astype(BF16)
    ya_t = jnp.concatenate([_dot(wvt_ref[hd], o[:, hd * Q_BLOCK:(hd + 1) * Q_BLOCK])
                            for hd in range(N_ATTN_HEADS)], axis=0)
    ya_ref[0] = ya_t.T.astype(BF16)


def _dsa(q4, qi4, wt, kv, kvt, kidx2, wvt):
    B, nq = q4.shape[0], q4.shape[1]
    S = kv.shape[1]
    KC = KA = _tiles(S)["kc"]
    topk = min(TOPK_MAX, S // 4)
    assert S % KC == 0 and KC % KA == 0 and KA % Q_BLOCK == 0
    hq = N_ATTN_HEADS * Q_BLOCK
    return pl.pallas_call(
        functools.partial(_dsa_kernel, S, KC, KA, topk),
        out_shape=jax.ShapeDtypeStruct((B, S, ATTN_WIDTH), BF16),
        grid=(B, nq),
        in_specs=[
            pl.BlockSpec((1, 1) + q4.shape[2:], lambda b, i: (b, i, 0, 0)),
            pl.BlockSpec((1, 1) + qi4.shape[2:], lambda b, i: (b, i, 0, 0)),
            pl.BlockSpec((1, IDX_HEADS, Q_BLOCK), lambda b, i: (b, 0, i)),
            pl.BlockSpec((1, S, KV_RANK), lambda b, i: (b, 0, 0)),
            pl.BlockSpec((1, KV_RANK, S), lambda b, i: (b, 0, 0)),
            pl.BlockSpec((1, S, 2 * LANES), lambda b, i: (b, 0, 0)),
            pl.BlockSpec(wvt.shape, lambda b, i: (0, 0, 0)),
        ],
        out_specs=pl.BlockSpec((1, Q_BLOCK, ATTN_WIDTH), lambda b, i: (b, i, 0)),
        scratch_shapes=[
            pltpu.VMEM((S, Q_BLOCK), I32),
            pltpu.VMEM((KV_RANK + ONES_ROWS, hq), F32),
            pltpu.VMEM((1, hq), F32),
            pltpu.VMEM((1, hq), F32),
            pltpu.VMEM((KA, hq), BF16),
        ],
        compiler_params=_cparams(("arbitrary", "arbitrary")),
        name="dsa",
    )(q4, qi4, wt, kv, kvt, kidx2, wvt)


def _hgrn_kernel(TS, qs_ref, k_ref, v_ref, vt_ref, lf_ref, og_ref, grec_ref, y_ref,
                 st_ref, aloc_ref, e_ref, qh_ref, kh_ref, o_ref, g_ref, sb_ref, kblk_ref):
    n_tiles = REC_WIDTH // LANES
    nsb = TS // SUB

    @pl.when(pl.program_id(1) == 0)
    def _():
        st_ref[...] = jnp.zeros(st_ref.shape, F32)

    a = lf_ref[0]
    r = lax.broadcasted_iota(I32, a.shape, 0) % SUB
    for sh in (1, 2, 4, 8):
        a = a + jnp.where(r >= sh, pltpu.roll(a, sh, 0), 0.0)
    aend = jnp.where(r == SUB - 1, a, 0.0)
    for sh in (1, 2, 4, 8):
        aend = aend + jnp.where(r + sh <= SUB - 1, pltpu.roll(aend, TS - sh, 0), 0.0)
    e = jnp.exp(a)
    aloc_ref[...] = a * LOG2E
    e_ref[...] = e
    qh_ref[...] = (qs_ref[0].astype(F32) * e).astype(BF16)
    kh_ref[...] = (k_ref[0].astype(F32) * jnp.exp(aend - a)).astype(BF16)

    half = REC_K_DIM
    li = lax.broadcasted_iota(I32, (LANES, LANES), 0) // half
    lj = lax.broadcasted_iota(I32, (LANES, LANES), 1) // half
    bd_f = jnp.where(li == lj, 1.0, 0.0)
    bd_b = bd_f.astype(BF16)
    gsel = jnp.where(lax.broadcasted_iota(I32, (SUB, SUB * SUB), 0)
                     == lax.broadcasted_iota(I32, (SUB, SUB * SUB), 1) // SUB, 1.0, 0.0).astype(BF16)
    srow = lax.broadcasted_iota(I32, (SUB, LANES), 0)

    def diag_body(g, carry):
        units = []
        for jj in range(DIAG_GROUP):
            rows = pl.ds(pl.multiple_of((g * DIAG_GROUP + jj) * SUB, SUB), SUB)
            for p in range(n_tiles):
                units.append((rows, slice(p * LANES, (p + 1) * LANES)))
        wsts = []
        for rows, lanes in units:
            a_blk = aloc_ref[rows, lanes]
            q_blk = qs_ref[0, rows, lanes].astype(F32)
            k_blk = k_ref[0, rows, lanes].astype(F32)
            ws = []
            for t in range(SUB):
                d = a_blk[t:t + 1, :] - a_blk
                w = jnp.exp2(jnp.where(srow <= t, d, -jnp.inf)) * (k_blk * q_blk[t:t + 1, :])
                ws.append(w.astype(BF16))
            wsts.append(jnp.concatenate(ws, axis=0))
        rexps = [_dot(wst, bd_b) for wst in wsts]
        m2s = []
        for (rows, lanes), rexp in zip(units, rexps):
            v_f = v_ref[0, rows, lanes].astype(F32)
            m2s.append((rexp * jnp.concatenate([v_f] * SUB, axis=0)).astype(BF16))
        outs = [_dot(gsel, m2) for m2 in m2s]
        for (rows, lanes), out in zip(units, outs):
            o_ref[rows, lanes] = out
        return carry

    lax.fori_loop(0, nsb // DIAG_GROUP, diag_body, 0)

    @pl.when((pl.program_id(0) == 0) & (pl.program_id(1) == 0))
    def _():
        kblk_ref[...] = jnp.zeros(kblk_ref.shape, BF16)

    for p in range(n_tiles):
        lanes = slice(p * LANES, (p + 1) * LANES)
        for j in range(nsb):
            kblk_ref[j * SUB:(j + 1) * SUB, j * LANES:(j + 1) * LANES] = kh_ref[j * SUB:(j + 1) * SUB, lanes]
        g_ref[p] = _dot(vt_ref[0, lanes, :], kblk_ref[...])

    for p in range(n_tiles):
        lanes = slice(p * LANES, (p + 1) * LANES)
        st = st_ref[p]
        for j in range(nsb):
            sb_ref[j, p] = st.astype(BF16)
            dec = e_ref[(j + 1) * SUB - 1:(j + 1) * SUB, lanes]
            st = st * dec + g_ref[p, :, j * LANES:(j + 1) * LANES] * bd_f
        st_ref[p] = st

    for j in range(nsb):
        rows = slice(j * SUB, (j + 1) * SUB)
        for p in range(n_tiles):
            lanes = slice(p * LANES, (p + 1) * LANES)
            o_ref[rows, lanes] = o_ref[rows, lanes] + _nt_dot(qh_ref[rows, lanes], sb_ref[j, p])

    o = o_ref[...]
    o2 = o * o
    hi = o2.astype(BF16)
    lo = (o2 - hi.astype(F32)).astype(BF16)
    ms = jnp.concatenate(
        [_dot(hi[:, p * LANES:(p + 1) * LANES], bd_b) + _dot(lo[:, p * LANES:(p + 1) * LANES], bd_b)
         for p in range(n_tiles)], axis=1) * (1.0 / REC_V_DIM)
    y = o * lax.rsqrt(ms + EPS) * grec_ref[...]
    y_ref[0] = (y * og_ref[0].astype(F32)).astype(BF16)


def _hgrn(qs, k, v, vt, lf, og, grec_t, ts):
    B, S, W = qs.shape
    nsb = ts // SUB
    n_tiles = W // LANES
    tok = lambda: pl.BlockSpec((1, ts, W), lambda b, i: (b, i, 0))
    return pl.pallas_call(
        functools.partial(_hgrn_kernel, ts),
        out_shape=jax.ShapeDtypeStruct((B, S, W), BF16),
        grid=(B, S // ts),
        in_specs=[tok(), tok(), tok(), pl.BlockSpec((1, W, ts), lambda b, i: (b, 0, i)), tok(), tok(),
                  pl.BlockSpec((1, W), lambda b, i: (0, 0))],
        out_specs=tok(),
        scratch_shapes=[
            pltpu.VMEM((n_tiles, LANES, LANES), F32),
            pltpu.VMEM((ts, W), F32),
            pltpu.VMEM((ts, W), F32),
            pltpu.VMEM((ts, W), BF16),
            pltpu.VMEM((ts, W), BF16),
            pltpu.VMEM((ts, W), F32),
            pltpu.VMEM((n_tiles, LANES, nsb * LANES), F32),
            pltpu.VMEM((nsb, n_tiles, LANES, LANES), BF16),
            pltpu.VMEM((ts, nsb * LANES), BF16),
        ],
        compiler_params=_cparams(("arbitrary", "arbitrary")),
        name="hgrn",
    )(qs, k, v, vt, lf, og, grec_t)


def _merge_kernel(x_ref, ya_ref, yr_ref, sga_ref, sgr_ref, gt1_ref, sh2_ref, sc2_ref, g2_ref,
                  wba_ref, wbr_ref, wo_ref, wrh_ref, wrl_ref, br_ref, x1_out, h2_out, rt_out, cnt_out):
    tm = x_ref.shape[1]
    groups = [slice(h * tm // MERGE_SPLIT, (h + 1) * tm // MERGE_SPLIT) for h in range(MERGE_SPLIT)]
    da = [_dot(ya_ref[0, r, :], wba_ref[...]) for r in groups]
    dr = [_dot(yr_ref[0, r, :], wbr_ref[...]) for r in groups]
    merged = [(sga_ref[0, r, :].astype(F32) * a + sgr_ref[0, r, :].astype(F32) * b).astype(BF16)
              for r, a, b in zip(groups, da, dr)]
    dout = [_dot(m, wo_ref[...]) for m in merged]
    x1s = [x_ref[0, r, :] + gt1_ref[0] * d for r, d in zip(groups, dout)]
    h2s = [(_rms(x1) * g2_ref[...]) * (1.0 + sc2_ref[0]) + sh2_ref[0] for x1 in x1s]
    lgs = [_split_dot(h2, wrh_ref[...], wrl_ref[...]) for h2 in h2s]
    x1_out[0] = jnp.concatenate(x1s, axis=0)
    _to_token_tiles(h2_out.at[0], jnp.concatenate(h2s, axis=0))

    lg = jnp.concatenate(lgs, axis=0) + br_ref[...]
    lane = lax.broadcasted_iota(I32, lg.shape, 1)
    big = jnp.int32(1 << 20)
    gl = jnp.where(lane < N_GROUPS, lg, -jnp.inf)
    gmax = jnp.max(gl, axis=1, keepdims=True)
    gsel = jnp.min(jnp.where(gl == gmax, lane, big), axis=1, keepdims=True)
    ggate = 1.0 / jnp.sum(jnp.exp(gl - gmax), axis=1, keepdims=True)
    lo = N_GROUPS + EXPERTS_PER_GROUP * gsel
    el = jnp.where((lane >= lo) & (lane < lo + EXPERTS_PER_GROUP), lg, -jnp.inf)
    v1 = jnp.max(el, axis=1, keepdims=True)
    i1 = jnp.min(jnp.where(el == v1, lane, big), axis=1, keepdims=True)
    el2 = jnp.where(lane == i1, -jnp.inf, el)
    v2 = jnp.max(el2, axis=1, keepdims=True)
    i2 = jnp.min(jnp.where(el2 == v2, lane, big), axis=1, keepdims=True)
    e2 = jnp.exp(v2 - v1)
    den = 1.0 + e2
    g1 = (1.0 / den) * ggate
    g2 = (e2 / den) * ggate
    oh1 = jnp.where(lane == i1 - N_GROUPS, 1.0, 0.0)
    oh2 = jnp.where(lane == i2 - N_GROUPS, 1.0, 0.0)
    ohs = (oh1 + oh2).astype(BF16)
    ltri = jnp.where(lax.broadcasted_iota(I32, (tm, tm), 1) < lax.broadcasted_iota(I32, (tm, tm), 0),
                     1.0, 0.0).astype(BF16)
    before = _dot(ltri, ohs)
    r1 = jnp.sum(before * oh1, axis=1, keepdims=True)
    r2 = jnp.sum(before * oh2, axis=1, keepdims=True)
    cnt_out[0, 0] = jnp.broadcast_to(jnp.sum(oh1 + oh2, axis=0, keepdims=True), (SUBLANES, LANES))
    vals = ((i1 - N_GROUPS).astype(F32), (i2 - N_GROUPS).astype(F32), g1, g2, r1, r2)
    rt = jnp.zeros(lg.shape, F32)
    for j, val in enumerate(vals):
        rt = jnp.where(lane == j, val, rt)
    rt_out[0] = rt


def _merge(x, ya, yr, sga, sgr, gt1, sh2, sc2, g2, wba, wbr, wo, wrh, wrl, br, tm):
    B, S, D = x.shape
    tok = lambda w: pl.BlockSpec((1, tm, w), lambda b, i: (b, i, 0))
    vec = lambda: pl.BlockSpec((1, 1, D), lambda b, i: (b, 0, 0))
    full = lambda a: pl.BlockSpec(a.shape, lambda b, i: (0,) * a.ndim)
    return pl.pallas_call(
        _merge_kernel,
        out_shape=(jax.ShapeDtypeStruct((B, S, D), F32), jax.ShapeDtypeStruct((B, S * SUBLANES, LANES), F32),
                   jax.ShapeDtypeStruct((B, S, LANES), F32),
                   jax.ShapeDtypeStruct((B, S // tm, SUBLANES, LANES), F32)),
        grid=(B, S // tm),
        in_specs=[tok(D), tok(ATTN_WIDTH), tok(REC_WIDTH), tok(D), tok(D), vec(), vec(), vec(), full(g2),
                  full(wba), full(wbr), full(wo), full(wrh), full(wrl), full(br)],
        out_specs=(tok(D), pl.BlockSpec((1, tm * SUBLANES, LANES), lambda b, i: (b, i, 0)), tok(LANES),
                   pl.BlockSpec((1, 1, SUBLANES, LANES), lambda b, i: (b, i, 0, 0))),
        compiler_params=_cparams(("arbitrary", "arbitrary")),
        name="merge",
    )(x, ya, yr, sga, sgr, gt1, sh2, sc2, g2, wba, wbr, wo, wrh, wrl, br)


def _to_token_tiles(ref, x):
    n = x.shape[0]
    for s in range(SUBLANES):
        ref[pl.ds(s, n, stride=SUBLANES), :] = x[:, s * LANES:(s + 1) * LANES]


def _from_token_tiles(ref, first_tile, n):
    return jnp.concatenate([ref[pl.ds(first_tile * SUBLANES + s, n, stride=SUBLANES), :]
                            for s in range(SUBLANES)], axis=1)


def _tile_gather(src_hbm, idx_ref, dst, sem, n):
    def body(r, carry):
        src = src_hbm.at[pl.ds(pl.multiple_of(idx_ref[0, 0, r] * SUBLANES, SUBLANES), SUBLANES)]
        pltpu.make_async_copy(src, dst.at[pl.ds(pl.multiple_of(r * SUBLANES, SUBLANES), SUBLANES)], sem).start()
        return carry
    lax.fori_loop(0, n, body, 0, unroll=8)


def _tile_gather_wait(src_hbm, dst, sem, n):
    pltpu.make_async_copy(src_hbm.at[pl.ds(0, n * SUBLANES)], dst, sem).wait()


def _scatter_kernel(TM, d_ref, h2_ref, xs_in, xs_out, sem):
    del xs_in

    for j in range(TOP_K_IN_GROUP):
        def body(t, carry):
            src = h2_ref.at[pl.ds(pl.multiple_of(t * SUBLANES, SUBLANES), SUBLANES)]
            dst = xs_out.at[pl.ds(pl.multiple_of(d_ref[0, 0, j * TM + t] * SUBLANES, SUBLANES), SUBLANES)]
            pltpu.make_async_copy(src, dst, sem).start()
            return carry
        lax.fori_loop(0, TM, body, 0, unroll=8)
    for _ in range(TOP_K_IN_GROUP):
        pltpu.make_async_copy(h2_ref, xs_out.at[pl.ds(0, TM * SUBLANES)], sem).wait()


def _scatter_rows(dest3, h2t, n_rows, tm):
    n = dest3.shape[0]
    xs0 = jnp.zeros((n_rows * SUBLANES, LANES), F32)
    return pl.pallas_call(
        functools.partial(_scatter_kernel, tm),
        out_shape=jax.ShapeDtypeStruct(xs0.shape, F32),
        grid=(n,),
        in_specs=[
            pl.BlockSpec((1, 1, TOP_K_IN_GROUP * tm), lambda i: (i, 0, 0), memory_space=pltpu.SMEM),
            pl.BlockSpec((tm * SUBLANES, LANES), lambda i: (i, 0)),
            pl.BlockSpec(memory_space=pl.ANY),
        ],
        out_specs=pl.BlockSpec(memory_space=pl.ANY),
        scratch_shapes=[pltpu.SemaphoreType.DMA(())],
        input_output_aliases={2: 0},
        compiler_params=_cparams(("arbitrary",)),
        name="scatter_rows",
    )(dest3, h2t, xs0)


def _moe_kernel(be_ref, x_ref, wg_ref, wu_ref, wd_ref, y_ref, wgb, wub, wdb):
    i = pl.program_id(0)

    @pl.when((i == 0) | (be_ref[i] != be_ref[jnp.maximum(i - 1, 0)]))
    def _():
        wgb[...] = wg_ref[0, 0].astype(BF16)
        wub[...] = wu_ref[0, 0].astype(BF16)
        wdb[...] = wd_ref[0, 0].astype(BF16)

    xb = _from_token_tiles(x_ref, 0, x_ref.shape[0] // SUBLANES).astype(BF16)
    g = _dot(xb, wgb[...])
    hid = (g * jax.nn.sigmoid(g)) * _dot(xb, wub[...])
    _to_token_tiles(y_ref, _dot(hid.astype(BF16), wdb[...]))


def _moe(layer, blk_expert, xs, wgate, wup, wdown, blk):
    n_blk = blk_expert.shape[0]
    D = wgate.shape[2]
    rows = pl.BlockSpec((blk * SUBLANES, LANES), lambda i, be: (i, 0))
    grid_spec = pltpu.PrefetchScalarGridSpec(
        num_scalar_prefetch=1,
        grid=(n_blk,),
        in_specs=[
            rows,
            pl.BlockSpec((1, 1, D, D_EXPERT), lambda i, be: (layer, be[i], 0, 0)),
            pl.BlockSpec((1, 1, D, D_EXPERT), lambda i, be: (layer, be[i], 0, 0)),
            pl.BlockSpec((1, 1, D_EXPERT, D), lambda i, be: (layer, be[i], 0, 0)),
        ],
        out_specs=rows,
        scratch_shapes=[pltpu.VMEM((D, D_EXPERT), BF16), pltpu.VMEM((D, D_EXPERT), BF16),
                        pltpu.VMEM((D_EXPERT, D), BF16)],
    )
    return pl.pallas_call(
        _moe_kernel,
        out_shape=jax.ShapeDtypeStruct(xs.shape, F32),
        grid_spec=grid_spec,
        compiler_params=_cparams(("arbitrary",)),
        name="moe",
    )(blk_expert, xs, wgate, wup, wdown)


def _comb_kernel(TM, final, d_ref, dn_ref, x1_ref, rt_ref, gt2_ref, gf_ref, ys_hbm, o_ref, buf, sem):
    i = pl.program_id(0)
    n = pl.num_programs(0)
    slot = lax.rem(i, 2)

    @pl.when(i == 0)
    def _():
        _tile_gather(ys_hbm, d_ref, buf.at[0], sem.at[0], 2 * TM)

    @pl.when(i + 1 < n)
    def _():
        _tile_gather(ys_hbm, dn_ref, buf.at[1 - slot], sem.at[1 - slot], 2 * TM)

    _tile_gather_wait(ys_hbm, buf.at[slot], sem.at[slot], 2 * TM)
    rt = rt_ref[...]
    y0 = _from_token_tiles(buf.at[slot], 0, TM)
    y1 = _from_token_tiles(buf.at[slot], TM, TM)
    mix = rt[:, TOP_K_IN_GROUP:TOP_K_IN_GROUP + 1] * y0 + rt[:, TOP_K_IN_GROUP + 1:TOP_K_IN_GROUP + 2] * y1
    x2 = x1_ref[...] + gt2_ref[0] * mix
    if final:
        x2 = _rms(x2) * gf_ref[...]
    o_ref[...] = x2


def _combine(dest3, x1, route, gt2, gf, ys, S, tm, final):
    T, D = x1.shape
    n = T // tm
    per_b = S // tm
    return pl.pallas_call(
        functools.partial(_comb_kernel, tm, final),
        out_shape=jax.ShapeDtypeStruct((T, D), F32),
        grid=(n,),
        in_specs=[
            pl.BlockSpec((1, 1, 2 * tm), lambda i: (i, 0, 0), memory_space=pltpu.SMEM),
            pl.BlockSpec((1, 1, 2 * tm), lambda i: (jnp.minimum(i + 1, n - 1), 0, 0), memory_space=pltpu.SMEM),
            pl.BlockSpec((tm, D), lambda i: (i, 0)),
            pl.BlockSpec((tm, LANES), lambda i: (i, 0)),
            pl.BlockSpec((1, 1, D), lambda i: (i // per_b, 0, 0)),
            pl.BlockSpec((1, D), lambda i: (0, 0)),
            pl.BlockSpec(memory_space=pl.ANY),
        ],
        out_specs=pl.BlockSpec((tm, D), lambda i: (i, 0)),
        scratch_shapes=[pltpu.VMEM((2, 2 * tm * SUBLANES, LANES), F32), pltpu.SemaphoreType.DMA((2,))],
        compiler_params=_cparams(("arbitrary",)),
        name="combine",
    )(dest3, dest3, x1, route, gt2, gf, ys)


def _dispatch(route, tile_cnt, blk):
    T = route.shape[0]
    N = T * TOP_K_IN_GROUP
    n_tiles = tile_cnt.shape[0]
    eid = route[:, 0:TOP_K_IN_GROUP].astype(I32).reshape(N)
    rank_in_tile = route[:, 2 * TOP_K_IN_GROUP:3 * TOP_K_IN_GROUP].astype(I32).reshape(N)
    tile_start = jnp.cumsum(tile_cnt, axis=0) - tile_cnt
    counts = jnp.sum(tile_cnt, axis=0)
    padded = (counts + blk - 1) // blk * blk
    pad_end = jnp.cumsum(padded)
    pad_start = pad_end - padded
    base = jnp.repeat(pad_start[None, :] + tile_start, N // n_tiles, axis=0)
    hit = eid[:, None] == jnp.arange(N_EXPERTS, dtype=I32)[None, :]
    dest = jnp.sum(jnp.where(hit, base, 0), axis=1) + rank_in_tile
    n_blk = (N + N_EXPERTS * blk) // blk
    blk_start = jnp.arange(n_blk, dtype=I32)[:, None] * blk
    blk_expert = jnp.minimum(jnp.sum((pad_end[None, :] <= blk_start).astype(I32), axis=1), N_EXPERTS - 1)
    return blk_expert, dest.reshape(T, TOP_K_IN_GROUP)


def _tile_plan(dest, tm):
    T = dest.shape[0]
    return dest.reshape(T // tm, tm, TOP_K_IN_GROUP).transpose(0, 2, 1).reshape(T // tm, 1, TOP_K_IN_GROUP * tm)


def kernel(x, c, w_mod, b_mod, g_norm1, g_norm2, w_in, g_cq, g_ckv, g_kidx, w_q_up, w_idx_q, w_v_up,
           lb_logits, g_rec, w_branch_a, w_branch_r, w_out, w_grp, b_grp, w_exp_router, b_exp_router,
           w_gate, w_up, w_down, g_final):
    B, S, D = x.shape
    L = w_mod.shape[0]
    T = B * S
    tiles = _tiles(S)
    tm, ts, tmc, blk = tiles["tm"], tiles["ts"], tiles["tmc"], tiles["blk"]
    kw = N_REC_HEADS * REC_K_DIM
    na = Q_RANK + KV_RANK + IDX_DIM + IDX_HEADS
    n_rec = 2 * kw + 2 * REC_WIDTH

    mod = _modulation(c, w_mod, b_mod)
    for l in range(L):
        m6 = mod[l].reshape(B, 6, 1, D)
        sh1, sc1, gt1, sh2, sc2, gt2 = (m6[:, j] for j in range(6))
        wa = jnp.pad(w_in[l, :, :na], ((0, 0), (0, 512 - na))).astype(BF16)
        wrec = w_in[l, :, na:na + n_rec].astype(BF16)
        wg = w_in[l, :, na + n_rec:].astype(BF16)
        gk = jnp.pad(g_kidx[l], (0, LANES - IDX_DIM)).reshape(1, LANES)
        (q4, qi4, kv, kvt, kidx2, wt, qs, kk, vv, vt, lf, og, sga, sgr) = _proj(
            l, x, sh1, sc1, g_norm1[l].reshape(1, D), wa, wrec, wg, g_cq[l].reshape(1, -1),
            g_ckv[l].reshape(1, -1), gk, w_q_up[l].astype(BF16), w_idx_q[l].astype(BF16), lb_logits, tm)
        wvt = jnp.swapaxes(w_v_up[l], 1, 2).astype(BF16)
        ya = _dsa(q4, qi4, wt, kv, kvt, kidx2, wvt)
        yr = _hgrn(qs, kk, vv, vt, lf, og, jnp.tile(g_rec[l], N_REC_HEADS).reshape(1, REC_WIDTH), ts)
        wr = jnp.pad(jnp.concatenate([w_grp[l], w_exp_router[l]], axis=1),
                     ((0, 0), (0, LANES - N_GROUPS - N_EXPERTS)))
        wrh, wrl = _hi_lo(wr)
        br = jnp.pad(jnp.concatenate([b_grp[l], b_exp_router[l]]), (0, LANES - N_GROUPS - N_EXPERTS)).reshape(1, LANES)
        x1, h2t, route, tile_cnt = _merge(x, ya, yr, sga, sgr, gt1, sh2, sc2, g_norm2[l].reshape(1, D),
                               w_branch_a[l].astype(BF16), w_branch_r[l].astype(BF16), w_out[l].astype(BF16),
                               wrh, wrl, br, tm)
        route = route.reshape(T, LANES)
        tile_cnt = tile_cnt[:, :, 0, :N_EXPERTS].reshape(-1, N_EXPERTS).astype(I32)
        blk_expert, dest = _dispatch(route, tile_cnt, blk)
        xs = _scatter_rows(_tile_plan(dest, tm), h2t.reshape(T * SUBLANES, LANES),
                           TOP_K_IN_GROUP * T + N_EXPERTS * blk, tm)
        ys = _moe(l, blk_expert, xs, w_gate, w_up, w_down, blk)
        x = _combine(_tile_plan(dest, tmc), x1.reshape(T, D), route, gt2, g_final.reshape(1, D), ys, S, tmc,
                     l == L - 1).reshape(B, S, D)
    return x
```

```python
import functools

import jax
import jax.numpy as jnp
from jax import lax
from jax.experimental import pallas as pl
from jax.experimental.pallas import tpu as pltpu

N_ATTN_HEADS = 8
Q_RANK = 256
KV_RANK = 128
ATTN_V_DIM = 64
ATTN_WIDTH = N_ATTN_HEADS * ATTN_V_DIM
ATTN_SCALE = KV_RANK ** -0.5
IDX_HEADS = 8
IDX_DIM = 64
IDX_W_SCALE = (IDX_HEADS * IDX_DIM) ** -0.5
TOPK_MAX = 256
N_REC_HEADS = 8
REC_K_DIM = 64
REC_V_DIM = 64
REC_WIDTH = N_REC_HEADS * REC_V_DIM
N_GROUPS = 4
EXPERTS_PER_GROUP = 8
N_EXPERTS = N_GROUPS * EXPERTS_PER_GROUP
TOP_K_IN_GROUP = 2
D_EXPERT = 512
EPS = 1e-6
NEG = -1e30
TINY = 1e-30

LANES = 128
SUBLANES = 8
Q_BLOCK = 128
COUNT_ACCS = 8
ONES_ROWS = 16
LOG2E = 1.4426950408889634
SUB = 16
DIAG_GROUP = 4
MERGE_SPLIT = 2
VMEM_LIMIT = 56 * 1024 * 1024


def _tiles(S):
    return dict(
        tm=min(512, S),
        ts=min(256, S),
        tmc=min(256, S),
        kc=min(512, S),
        blk=512,
    )

F32 = jnp.float32
BF16 = jnp.bfloat16
I32 = jnp.int32


def _cparams(sem):
    return pltpu.CompilerParams(dimension_semantics=sem, vmem_limit_bytes=VMEM_LIMIT)


def _nt_dot(a, b):
    return lax.dot_general(a, b, (((1,), (1,)), ((), ())), preferred_element_type=F32)


def _dot(a, b):
    return jnp.dot(a, b, preferred_element_type=F32)


def _split_dot(a_f32, b_hi, b_lo):
    a_hi = a_f32.astype(BF16)
    a_lo = (a_f32 - a_hi.astype(F32)).astype(BF16)
    return _dot(a_hi, b_hi) + (_dot(a_hi, b_lo) + _dot(a_lo, b_hi))


def _hi_lo(w):
    hi = w.astype(BF16)
    lo = (w - hi.astype(F32)).astype(BF16)
    return hi, lo


def _mod_kernel(c_ref, w_ref, b_ref, o_ref):
    c = c_ref[...]
    ca = c * jax.nn.sigmoid(c)
    w = w_ref[0]
    w_hi = w.astype(BF16)
    w_lo = (w - w_hi.astype(F32)).astype(BF16)
    o_ref[0] = _split_dot(ca, w_hi, w_lo) + b_ref[0]


def _modulation(c, w_mod, b_mod):
    L, D, D6 = w_mod.shape
    B = c.shape[0]
    tn = 1024
    return pl.pallas_call(
        _mod_kernel,
        out_shape=jax.ShapeDtypeStruct((L, B, D6), F32),
        grid=(L, D6 // tn),
        in_specs=[
            pl.BlockSpec((B, D), lambda l, n: (0, 0)),
            pl.BlockSpec((1, D, tn), lambda l, n: (l, 0, n)),
            pl.BlockSpec((1, 1, tn), lambda l, n: (l, 0, n)),
        ],
        out_specs=pl.BlockSpec((1, B, tn), lambda l, n: (l, 0, n)),
        compiler_params=_cparams(("arbitrary", "arbitrary")),
        name="modulation",
    )(c, w_mod, b_mod.reshape(L, 1, D6))


def _rms(x, eps=EPS):
    return x * lax.rsqrt(jnp.mean(x * x, axis=-1, keepdims=True) + eps)


def _proj_kernel(layer, x_ref, sh_ref, sc_ref, g1_ref, wa_ref, wrec_ref, wg_ref, gcq_ref, gckv_ref,
                 gk_ref, wq_ref, wiq_ref, lb_ref,
                 q_out, qi_out, kv_out, kvt_out, kidx_out, wt_out, qs_out, k_out, v_out, vt_out, lf_out, og_out,
                 sga_out, sgr_out):
    x = x_ref[0]
    tm = x.shape[0]
    h = (_rms(x) * g1_ref[...]) * (1.0 + sc_ref[0]) + sh_ref[0]
    hb = h.astype(BF16)

    ua = _dot(hb, wa_ref[...])
    cq = (_rms(ua[:, :Q_RANK]) * gcq_ref[...]).astype(BF16)
    q = (_dot(cq, wq_ref[...]) * (ATTN_SCALE * LOG2E)).astype(BF16)
    qi = _dot(cq, wiq_ref[...]).astype(BF16)
    for qb in range(tm // Q_BLOCK):
        rows = slice(qb * Q_BLOCK, (qb + 1) * Q_BLOCK)
        for hd in range(N_ATTN_HEADS):
            q_out[0, qb, hd * Q_BLOCK:(hd + 1) * Q_BLOCK, :] = q[rows, hd * LANES:(hd + 1) * LANES]
        for t in range(IDX_HEADS * IDX_DIM // LANES):
            qi_out[0, qb, t * Q_BLOCK:(t + 1) * Q_BLOCK, :] = qi[rows, t * LANES:(t + 1) * LANES]
    kvn = _rms(ua[:, Q_RANK:Q_RANK + KV_RANK]) * gckv_ref[...]
    kv_out[0] = kvn.astype(BF16)
    kvt_out[0] = kvn.T.astype(BF16)

    t3 = ua[:, Q_RANK + KV_RANK:]
    lane = lax.broadcasted_iota(I32, t3.shape, 1)
    ms = jnp.sum(jnp.where(lane < IDX_DIM, t3 * t3, 0.0), axis=-1, keepdims=True) * (1.0 / IDX_DIM)
    ka = t3 * lax.rsqrt(ms + EPS) * gk_ref[...]
    kb = pltpu.roll(ka, IDX_DIM, 1)
    kidx_out[0] = jnp.concatenate([ka, kb], axis=1).astype(BF16)
    wt_out[0] = (t3 * IDX_W_SCALE).T[IDX_DIM:IDX_DIM + IDX_HEADS, :]

    ur = _dot(hb, wrec_ref[...])
    kw = N_REC_HEADS * REC_K_DIM
    lbl = lb_ref[...]
    e = jnp.exp(lbl - jnp.max(lbl, axis=0, keepdims=True))
    p = e / jnp.sum(e, axis=0, keepdims=True)
    lb = jnp.zeros((1, kw), F32)
    for j in range(1, layer + 1):
        lb = lb + p[j:j + 1, :]
    lb = jnp.clip(lb, 0.0, 1.0)
    sig = jax.nn.sigmoid(ur[:, kw:2 * kw])
    f = lb + (1.0 - lb) * sig
    lf_out[0] = jnp.log(jnp.maximum(f, TINY))
    k_out[0] = ((1.0 - lb) * (1.0 - sig)).astype(BF16)
    qr = ur[:, :kw]
    qs_out[0] = (qr * jax.nn.sigmoid(qr)).astype(BF16)
    vr = ur[:, 2 * kw:2 * kw + REC_WIDTH]
    v_out[0] = vr.astype(BF16)
    vt_out[0] = vr.T.astype(BF16)
    og = ur[:, 2 * kw + REC_WIDTH:]
    og_out[0] = (og * jax.nn.sigmoid(og)).astype(BF16)

    ug = _dot(hb, wg_ref[...])
    D = x.shape[1]
    sga_out[0] = jax.nn.sigmoid(ug[:, :D]).astype(BF16)
    sgr_out[0] = jax.nn.sigmoid(ug[:, D:]).astype(BF16)


def _proj(layer, x, sh1, sc1, g1, wa, wrec, wg, gcq, gckv, gk, wq, wiq, lb_logits, tm):
    B, S, D = x.shape
    nqb = tm // Q_BLOCK
    nq = S // Q_BLOCK
    kw = N_REC_HEADS * REC_K_DIM
    c2 = lambda b, i: (0, 0)
    tok = lambda w: pl.BlockSpec((1, tm, w), lambda b, i: (b, i, 0))
    full = lambda a: pl.BlockSpec(a.shape, c2)
    out_shapes = (
        jax.ShapeDtypeStruct((B, nq, N_ATTN_HEADS * Q_BLOCK, KV_RANK), BF16),
        jax.ShapeDtypeStruct((B, nq, IDX_HEADS * IDX_DIM // LANES * Q_BLOCK, LANES), BF16),
        jax.ShapeDtypeStruct((B, S, KV_RANK), BF16),
        jax.ShapeDtypeStruct((B, KV_RANK, S), BF16),
        jax.ShapeDtypeStruct((B, S, 2 * LANES), BF16),
        jax.ShapeDtypeStruct((B, IDX_HEADS, S), F32),
        jax.ShapeDtypeStruct((B, S, kw), BF16),
        jax.ShapeDtypeStruct((B, S, kw), BF16),
        jax.ShapeDtypeStruct((B, S, REC_WIDTH), BF16),
        jax.ShapeDtypeStruct((B, REC_WIDTH, S), BF16),
        jax.ShapeDtypeStruct((B, S, kw), F32),
        jax.ShapeDtypeStruct((B, S, REC_WIDTH), BF16),
        jax.ShapeDtypeStruct((B, S, D), BF16),
        jax.ShapeDtypeStruct((B, S, D), BF16),
    )
    out_specs = (
        pl.BlockSpec((1, nqb, N_ATTN_HEADS * Q_BLOCK, KV_RANK), lambda b, i: (b, i, 0, 0)),
        pl.BlockSpec((1, nqb, IDX_HEADS * IDX_DIM // LANES * Q_BLOCK, LANES), lambda b, i: (b, i, 0, 0)),
        tok(KV_RANK),
        pl.BlockSpec((1, KV_RANK, tm), lambda b, i: (b, 0, i)),
        tok(2 * LANES),
        pl.BlockSpec((1, IDX_HEADS, tm), lambda b, i: (b, 0, i)),
        tok(kw), tok(kw), tok(REC_WIDTH),
        pl.BlockSpec((1, REC_WIDTH, tm), lambda b, i: (b, 0, i)),
        tok(kw), tok(REC_WIDTH),
        tok(D), tok(D),
    )
    vec = lambda: pl.BlockSpec((1, 1, D), lambda b, i: (b, 0, 0))
    return pl.pallas_call(
        functools.partial(_proj_kernel, layer),
        out_shape=out_shapes,
        grid=(B, S // tm),
        in_specs=[tok(D), vec(), vec(), full(g1), full(wa), full(wrec), full(wg), full(gcq), full(gckv),
                  full(gk), full(wq), full(wiq), full(lb_logits)],
        out_specs=out_specs,
        compiler_params=_cparams(("arbitrary", "arbitrary")),
        name="proj",
    )(x, sh1, sc1, g1, wa, wrec, wg, gcq, gckv, gk, wq, wiq, lb_logits)


def _loop_by_four(n, body):
    def quad(p, carry):
        for j in range(4):
            body(4 * p + j, carry)
        return carry
    lax.fori_loop(0, n // 4, quad, 0)
    done = n // 4 * 4

    @pl.when(n % 4 >= 2)
    def _():
        body(done, 0)
        body(done + 1, 0)

    @pl.when(n % 2 == 1)
    def _():
        body(n - 1, 0)


def _sort_key(score):
    bits = lax.bitcast_convert_type(score, I32)
    return bits ^ (lax.shift_right_arithmetic(bits, 31) & jnp.int32(0x7FFFFFFF))


def _dsa_kernel(S, KC, KA, TOPK, q_ref, qi_ref, wt_ref, kv_ref, kvt_ref, kidx_ref, wvt_ref, ya_ref,
                keys_ref, acc_ref, m_ref, a_ref, p_ref):
    i = pl.program_id(1)
    n_ch = ((i + 1) * Q_BLOCK + KC - 1) // KC
    qpos = i * Q_BLOCK + lax.broadcasted_iota(I32, (1, Q_BLOCK), 1)
    wt = wt_ref[0]
    wrow = [wt[hd:hd + 1, :] for hd in range(IDX_HEADS)]
    qi = qi_ref[0, 0]
    n_t = IDX_HEADS * IDX_DIM // LANES

    def kpos_of(c):
        return c * KC + lax.broadcasted_iota(I32, (KC, 1), 0)

    def score_body(c, carry):
        koff = pl.multiple_of(c * KC, KC)
        la = _nt_dot(kidx_ref[0, pl.ds(koff, KC), 0:LANES], qi)
        lb = _nt_dot(kidx_ref[0, pl.ds(koff, KC), LANES:2 * LANES], qi)
        score = jnp.zeros((KC, Q_BLOCK), F32)
        for t in range(n_t):
            lanes = slice(t * Q_BLOCK, (t + 1) * Q_BLOCK)
            score = score + wrow[2 * t] * jnp.maximum(la[:, lanes], 0.0)
            score = score + wrow[2 * t + 1] * jnp.maximum(lb[:, lanes], 0.0)
        score = jnp.where(kpos_of(c) <= qpos, score, NEG)
        keys_ref[pl.ds(koff, KC), :] = _sort_key(score)
        return carry

    _loop_by_four(n_ch, score_body)

    def count(pred):
        def body(c, accs):
            kk = keys_ref[pl.ds(pl.multiple_of(c * KC, KC), KC), :]
            accs = list(accs)
            for r in range(KC // SUBLANES):
                a = accs[r % COUNT_ACCS]
                accs[r % COUNT_ACCS] = jnp.where(pred(kk[r * SUBLANES:(r + 1) * SUBLANES]), a + 1, a)
            return tuple(accs)
        accs = lax.fori_loop(0, n_ch, body, tuple(jnp.zeros((SUBLANES, Q_BLOCK), I32) for _ in range(COUNT_ACCS)))
        return jnp.sum(functools.reduce(lambda x, y: x + y, accs), axis=0, keepdims=True)

    def bis_body(it, carry):
        v, n_ge = carry
        cand = v + lax.shift_left(jnp.int32(1), 31 - it)
        cnt = count(lambda kk: kk >= cand)
        ok = cnt >= TOPK
        return jnp.where(ok, cand, v), jnp.where(ok, cnt, n_ge)

    v, n_ge = lax.fori_loop(0, 32, bis_body, (jnp.full((1, Q_BLOCK), -2 ** 31, I32),
                                              jnp.full((1, Q_BLOCK), n_ch * KC, I32)))
    any_split = jnp.max(jnp.where(n_ge > TOPK, 1, 0)) > 0

    @pl.when(any_split)
    def _():
        need = (TOPK - count(lambda kk: kk > v)).astype(F32)
        tri = jnp.where(lax.broadcasted_iota(I32, (KC, KC), 1) <= lax.broadcasted_iota(I32, (KC, KC), 0),
                        1.0, 0.0).astype(BF16)

        def tie_body(c, seen):
            koff = pl.multiple_of(c * KC, KC)
            kk = keys_ref[pl.ds(koff, KC), :]
            tie = kk == v
            tie_f = jnp.where(tie, 1.0, 0.0)
            rank = _dot(tri, tie_f.astype(BF16)) + seen
            keys_ref[pl.ds(koff, KC), :] = jnp.where(tie & (rank > need), v - 1, kk)
            return seen + jnp.sum(tie_f, axis=0, keepdims=True)

        lax.fori_loop(0, n_ch, tie_body, jnp.zeros((1, Q_BLOCK), F32))

    m_ref[...] = jnp.full(m_ref.shape, NEG, F32)
    acc_ref[...] = jnp.zeros(acc_ref.shape, F32)
    ones_rows = jnp.ones((ONES_ROWS, KA), BF16)

    def att_body(c, carry):
        koff = pl.multiple_of(c * KA, KA)
        kvc = kv_ref[0, pl.ds(koff, KA), :]
        kpos = c * KA + lax.broadcasted_iota(I32, (KA, 1), 0)
        bias = jnp.where((keys_ref[pl.ds(koff, KA), :] >= v) & (kpos <= qpos), 0.0, -jnp.inf)
        for hd in range(N_ATTN_HEADS):
            lanes = slice(hd * Q_BLOCK, (hd + 1) * Q_BLOCK)
            s = _nt_dot(kvc, q_ref[0, 0, hd * Q_BLOCK:(hd + 1) * Q_BLOCK, :]) + bias
            m_prev = m_ref[:, lanes]
            m_next = jnp.maximum(m_prev, jnp.max(s, axis=0, keepdims=True))
            a_ref[:, lanes] = jnp.exp2(m_prev - m_next)
            m_ref[:, lanes] = m_next
            p_ref[:, lanes] = jnp.exp2(s - m_next).astype(BF16)
        kvt1 = jnp.concatenate([kvt_ref[0, :, pl.ds(koff, KA)], ones_rows], axis=0)
        acc_ref[...] = acc_ref[...] * a_ref[...] + _dot(kvt1, p_ref[...])
        return carry

    _loop_by_four(((i + 1) * Q_BLOCK + KA - 1) // KA, att_body)
    o = (acc_ref[0:KV_RANK, :] / acc_ref[KV_RANK:KV_RANK + 1, :]).astype(BF16)
    ya_t = jnp.concatenate([_dot(wvt_ref[hd], o[:, hd * Q_BLOCK:(hd + 1) * Q_BLOCK])
                            for hd in range(N_ATTN_HEADS)], axis=0)
    ya_ref[0] = ya_t.T.astype(BF16)


def _dsa(q4, qi4, wt, kv, kvt, kidx2, wvt):
    B, nq = q4.shape[0], q4.shape[1]
    S = kv.shape[1]
    KC = KA = _tiles(S)["kc"]
    topk = min(TOPK_MAX, S // 4)
    assert S % KC == 0 and KC % KA == 0 and KA % Q_BLOCK == 0
    hq = N_ATTN_HEADS * Q_BLOCK
    return pl.pallas_call(
        functools.partial(_dsa_kernel, S, KC, KA, topk),
        out_shape=jax.ShapeDtypeStruct((B, S, ATTN_WIDTH), BF16),
        grid=(B, nq),
        in_specs=[
            pl.BlockSpec((1, 1) + q4.shape[2:], lambda b, i: (b, i, 0, 0)),
            pl.BlockSpec((1, 1) + qi4.shape[2:], lambda b, i: (b, i, 0, 0)),
            pl.BlockSpec((1, IDX_HEADS, Q_BLOCK), lambda b, i: (b, 0, i)),
            pl.BlockSpec((1, S, KV_RANK), lambda b, i: (b, 0, 0)),
            pl.BlockSpec((1, KV_RANK, S), lambda b, i: (b, 0, 0)),
            pl.BlockSpec((1, S, 2 * LANES), lambda b, i: (b, 0, 0)),
            pl.BlockSpec(wvt.shape, lambda b, i: (0, 0, 0)),
        ],
        out_specs=pl.BlockSpec((1, Q_BLOCK, ATTN_WIDTH), lambda b, i: (b, i, 0)),
        scratch_shapes=[
            pltpu.VMEM((S, Q_BLOCK), I32),
            pltpu.VMEM((KV_RANK + ONES_ROWS, hq), F32),
            pltpu.VMEM((1, hq), F32),
            pltpu.VMEM((1, hq), F32),
            pltpu.VMEM((KA, hq), BF16),
        ],
        compiler_params=_cparams(("arbitrary", "arbitrary")),
        name="dsa",
    )(q4, qi4, wt, kv, kvt, kidx2, wvt)


def _hgrn_kernel(TS, qs_ref, k_ref, v_ref, vt_ref, lf_ref, og_ref, grec_ref, y_ref,
                 st_ref, aloc_ref, e_ref, qh_ref, kh_ref, o_ref, g_ref, sb_ref, kblk_ref):
    n_tiles = REC_WIDTH // LANES
    nsb = TS // SUB

    @pl.when(pl.program_id(1) == 0)
    def _():
        st_ref[...] = jnp.zeros(st_ref.shape, F32)

    a = lf_ref[0]
    r = lax.broadcasted_iota(I32, a.shape, 0) % SUB
    for sh in (1, 2, 4, 8):
        a = a + jnp.where(r >= sh, pltpu.roll(a, sh, 0), 0.0)
    aend = jnp.where(r == SUB - 1, a, 0.0)
    for sh in (1, 2, 4, 8):
        aend = aend + jnp.where(r + sh <= SUB - 1, pltpu.roll(aend, TS - sh, 0), 0.0)
    e = jnp.exp(a)
    aloc_ref[...] = a * LOG2E
    e_ref[...] = e
    qh_ref[...] = (qs_ref[0].astype(F32) * e).astype(BF16)
    kh_ref[...] = (k_ref[0].astype(F32) * jnp.exp(aend - a)).astype(BF16)

    half = REC_K_DIM
    li = lax.broadcasted_iota(I32, (LANES, LANES), 0) // half
    lj = lax.broadcasted_iota(I32, (LANES, LANES), 1) // half
    bd_f = jnp.where(li == lj, 1.0, 0.0)
    bd_b = bd_f.astype(BF16)
    gsel = jnp.where(lax.broadcasted_iota(I32, (SUB, SUB * SUB), 0)
                     == lax.broadcasted_iota(I32, (SUB, SUB * SUB), 1) // SUB, 1.0, 0.0).astype(BF16)
    srow = lax.broadcasted_iota(I32, (SUB, LANES), 0)

    def diag_body(g, carry):
        units = []
        for jj in range(DIAG_GROUP):
            rows = pl.ds(pl.multiple_of((g * DIAG_GROUP + jj) * SUB, SUB), SUB)
            for p in range(n_tiles):
                units.append((rows, slice(p * LANES, (p + 1) * LANES)))
        wsts = []
        for rows, lanes in units:
            a_blk = aloc_ref[rows, lanes]
            q_blk = qs_ref[0, rows, lanes].astype(F32)
            k_blk = k_ref[0, rows, lanes].astype(F32)
            ws = []
            for t in range(SUB):
                d = a_blk[t:t + 1, :] - a_blk
                w = jnp.exp2(jnp.where(srow <= t, d, -jnp.inf)) * (k_blk * q_blk[t:t + 1, :])
                ws.append(w.astype(BF16))
            wsts.append(jnp.concatenate(ws, axis=0))
        rexps = [_dot(wst, bd_b) for wst in wsts]
        m2s = []
        for (rows, lanes), rexp in zip(units, rexps):
            v_f = v_ref[0, rows, lanes].astype(F32)
            m2s.append((rexp * jnp.concatenate([v_f] * SUB, axis=0)).astype(BF16))
        outs = [_dot(gsel, m2) for m2 in m2s]
        for (rows, lanes), out in zip(units, outs):
            o_ref[rows, lanes] = out
        return carry

    lax.fori_loop(0, nsb // DIAG_GROUP, diag_body, 0)

    @pl.when((pl.program_id(0) == 0) & (pl.program_id(1) == 0))
    def _():
        kblk_ref[...] = jnp.zeros(kblk_ref.shape, BF16)

    for p in range(n_tiles):
        lanes = slice(p * LANES, (p + 1) * LANES)
        for j in range(nsb):
            kblk_ref[j * SUB:(j + 1) * SUB, j * LANES:(j + 1) * LANES] = kh_ref[j * SUB:(j + 1) * SUB, lanes]
        g_ref[p] = _dot(vt_ref[0, lanes, :], kblk_ref[...])

    for p in range(n_tiles):
        lanes = slice(p * LANES, (p + 1) * LANES)
        st = st_ref[p]
        for j in range(nsb):
            sb_ref[j, p] = st.astype(BF16)
            dec = e_ref[(j + 1) * SUB - 1:(j + 1) * SUB, lanes]
            st = st * dec + g_ref[p, :, j * LANES:(j + 1) * LANES] * bd_f
        st_ref[p] = st

    for j in range(nsb):
        rows = slice(j * SUB, (j + 1) * SUB)
        for p in range(n_tiles):
            lanes = slice(p * LANES, (p + 1) * LANES)
            o_ref[rows, lanes] = o_ref[rows, lanes] + _nt_dot(qh_ref[rows, lanes], sb_ref[j, p])

    o = o_ref[...]
    o2 = o * o
    hi = o2.astype(BF16)
    lo = (o2 - hi.astype(F32)).astype(BF16)
    ms = jnp.concatenate(
        [_dot(hi[:, p * LANES:(p + 1) * LANES], bd_b) + _dot(lo[:, p * LANES:(p + 1) * LANES], bd_b)
         for p in range(n_tiles)], axis=1) * (1.0 / REC_V_DIM)
    y = o * lax.rsqrt(ms + EPS) * grec_ref[...]
    y_ref[0] = (y * og_ref[0].astype(F32)).astype(BF16)


def _hgrn(qs, k, v, vt, lf, og, grec_t, ts):
    B, S, W = qs.shape
    nsb = ts // SUB
    n_tiles = W // LANES
    tok = lambda: pl.BlockSpec((1, ts, W), lambda b, i: (b, i, 0))
    return pl.pallas_call(
        functools.partial(_hgrn_kernel, ts),
        out_shape=jax.ShapeDtypeStruct((B, S, W), BF16),
        grid=(B, S // ts),
        in_specs=[tok(), tok(), tok(), pl.BlockSpec((1, W, ts), lambda b, i: (b, 0, i)), tok(), tok(),
                  pl.BlockSpec((1, W), lambda b, i: (0, 0))],
        out_specs=tok(),
        scratch_shapes=[
            pltpu.VMEM((n_tiles, LANES, LANES), F32),
            pltpu.VMEM((ts, W), F32),
            pltpu.VMEM((ts, W), F32),
            pltpu.VMEM((ts, W), BF16),
            pltpu.VMEM((ts, W), BF16),
            pltpu.VMEM((ts, W), F32),
            pltpu.VMEM((n_tiles, LANES, nsb * LANES), F32),
            pltpu.VMEM((nsb, n_tiles, LANES, LANES), BF16),
            pltpu.VMEM((ts, nsb * LANES), BF16),
        ],
        compiler_params=_cparams(("arbitrary", "arbitrary")),
        name="hgrn",
    )(qs, k, v, vt, lf, og, grec_t)


def _merge_kernel(x_ref, ya_ref, yr_ref, sga_ref, sgr_ref, gt1_ref, sh2_ref, sc2_ref, g2_ref,
                  wba_ref, wbr_ref, wo_ref, wrh_ref, wrl_ref, br_ref, x1_out, h2_out, rt_out, cnt_out):
    tm = x_ref.shape[1]
    groups = [slice(h * tm // MERGE_SPLIT, (h + 1) * tm // MERGE_SPLIT) for h in range(MERGE_SPLIT)]
    da = [_dot(ya_ref[0, r, :], wba_ref[...]) for r in groups]
    dr = [_dot(yr_ref[0, r, :], wbr_ref[...]) for r in groups]
    merged = [(sga_ref[0, r, :].astype(F32) * a + sgr_ref[0, r, :].astype(F32) * b).astype(BF16)
              for r, a, b in zip(groups, da, dr)]
    dout = [_dot(m, wo_ref[...]) for m in merged]
    x1s = [x_ref[0, r, :] + gt1_ref[0] * d for r, d in zip(groups, dout)]
    h2s = [(_rms(x1) * g2_ref[...]) * (1.0 + sc2_ref[0]) + sh2_ref[0] for x1 in x1s]
    lgs = [_split_dot(h2, wrh_ref[...], wrl_ref[...]) for h2 in h2s]
    x1_out[0] = jnp.concatenate(x1s, axis=0)
    _to_token_tiles(h2_out.at[0], jnp.concatenate(h2s, axis=0))

    lg = jnp.concatenate(lgs, axis=0) + br_ref[...]
    lane = lax.broadcasted_iota(I32, lg.shape, 1)
    big = jnp.int32(1 << 20)
    gl = jnp.where(lane < N_GROUPS, lg, -jnp.inf)
    gmax = jnp.max(gl, axis=1, keepdims=True)
    gsel = jnp.min(jnp.where(gl == gmax, lane, big), axis=1, keepdims=True)
    ggate = 1.0 / jnp.sum(jnp.exp(gl - gmax), axis=1, keepdims=True)
    lo = N_GROUPS + EXPERTS_PER_GROUP * gsel
    el = jnp.where((lane >= lo) & (lane < lo + EXPERTS_PER_GROUP), lg, -jnp.inf)
    v1 = jnp.max(el, axis=1, keepdims=True)
    i1 = jnp.min(jnp.where(el == v1, lane, big), axis=1, keepdims=True)
    el2 = jnp.where(lane == i1, -jnp.inf, el)
    v2 = jnp.max(el2, axis=1, keepdims=True)
    i2 = jnp.min(jnp.where(el2 == v2, lane, big), axis=1, keepdims=True)
    e2 = jnp.exp(v2 - v1)
    den = 1.0 + e2
    g1 = (1.0 / den) * ggate
    g2 = (e2 / den) * ggate
    oh1 = jnp.where(lane == i1 - N_GROUPS, 1.0, 0.0)
    oh2 = jnp.where(lane == i2 - N_GROUPS, 1.0, 0.0)
    ohs = (oh1 + oh2).astype(BF16)
    ltri = jnp.where(lax.broadcasted_iota(I32, (tm, tm), 1) < lax.broadcasted_iota(I32, (tm, tm), 0),
                     1.0, 0.0).astype(BF16)
    before = _dot(ltri, ohs)
    r1 = jnp.sum(before * oh1, axis=1, keepdims=True)
    r2 = jnp.sum(before * oh2, axis=1, keepdims=True)
    cnt_out[0, 0] = jnp.broadcast_to(jnp.sum(oh1 + oh2, axis=0, keepdims=True), (SUBLANES, LANES))
    vals = ((i1 - N_GROUPS).astype(F32), (i2 - N_GROUPS).astype(F32), g1, g2, r1, r2)
    rt = jnp.zeros(lg.shape, F32)
    for j, val in enumerate(vals):
        rt = jnp.where(lane == j, val, rt)
    rt_out[0] = rt


def _merge(x, ya, yr, sga, sgr, gt1, sh2, sc2, g2, wba, wbr, wo, wrh, wrl, br, tm):
    B, S, D = x.shape
    tok = lambda w: pl.BlockSpec((1, tm, w), lambda b, i: (b, i, 0))
    vec = lambda: pl.BlockSpec((1, 1, D), lambda b, i: (b, 0, 0))
    full = lambda a: pl.BlockSpec(a.shape, lambda b, i: (0,) * a.ndim)
    return pl.pallas_call(
        _merge_kernel,
        out_shape=(jax.ShapeDtypeStruct((B, S, D), F32), jax.ShapeDtypeStruct((B, S * SUBLANES, LANES), F32),
                   jax.ShapeDtypeStruct((B, S, LANES), F32),
                   jax.ShapeDtypeStruct((B, S // tm, SUBLANES, LANES), F32)),
        grid=(B, S // tm),
        in_specs=[tok(D), tok(ATTN_WIDTH), tok(REC_WIDTH), tok(D), tok(D), vec(), vec(), vec(), full(g2),
                  full(wba), full(wbr), full(wo), full(wrh), full(wrl), full(br)],
        out_specs=(tok(D), pl.BlockSpec((1, tm * SUBLANES, LANES), lambda b, i: (b, i, 0)), tok(LANES),
                   pl.BlockSpec((1, 1, SUBLANES, LANES), lambda b, i: (b, i, 0, 0))),
        compiler_params=_cparams(("arbitrary", "arbitrary")),
        name="merge",
    )(x, ya, yr, sga, sgr, gt1, sh2, sc2, g2, wba, wbr, wo, wrh, wrl, br)


def _to_token_tiles(ref, x):
    n = x.shape[0]
    for s in range(SUBLANES):
        ref[pl.ds(s, n, stride=SUBLANES), :] = x[:, s * LANES:(s + 1) * LANES]


def _from_token_tiles(ref, first_tile, n):
    return jnp.concatenate([ref[pl.ds(first_tile * SUBLANES + s, n, stride=SUBLANES), :]
                            for s in range(SUBLANES)], axis=1)


def _tile_gather(src_hbm, idx_ref, dst, sem, n):
    def body(r, carry):
        src = src_hbm.at[pl.ds(pl.multiple_of(idx_ref[0, 0, r] * SUBLANES, SUBLANES), SUBLANES)]
        pltpu.make_async_copy(src, dst.at[pl.ds(pl.multiple_of(r * SUBLANES, SUBLANES), SUBLANES)], sem).start()
        return carry
    lax.fori_loop(0, n, body, 0, unroll=8)


def _tile_gather_wait(src_hbm, dst, sem, n):
    pltpu.make_async_copy(src_hbm.at[pl.ds(0, n * SUBLANES)], dst, sem).wait()


def _scatter_kernel(TM, d_ref, h2_ref, xs_in, xs_out, sem):
    del xs_in

    for j in range(TOP_K_IN_GROUP):
        def body(t, carry):
            src = h2_ref.at[pl.ds(pl.multiple_of(t * SUBLANES, SUBLANES), SUBLANES)]
            dst = xs_out.at[pl.ds(pl.multiple_of(d_ref[0, 0, j * TM + t] * SUBLANES, SUBLANES), SUBLANES)]
            pltpu.make_async_copy(src, dst, sem).start()
            return carry
        lax.fori_loop(0, TM, body, 0, unroll=8)
    for _ in range(TOP_K_IN_GROUP):
        pltpu.make_async_copy(h2_ref, xs_out.at[pl.ds(0, TM * SUBLANES)], sem).wait()


def _scatter_rows(dest3, h2t, n_rows, tm):
    n = dest3.shape[0]
    xs0 = jnp.zeros((n_rows * SUBLANES, LANES), F32)
    return pl.pallas_call(
        functools.partial(_scatter_kernel, tm),
        out_shape=jax.ShapeDtypeStruct(xs0.shape, F32),
        grid=(n,),
        in_specs=[
            pl.BlockSpec((1, 1, TOP_K_IN_GROUP * tm), lambda i: (i, 0, 0), memory_space=pltpu.SMEM),
            pl.BlockSpec((tm * SUBLANES, LANES), lambda i: (i, 0)),
            pl.BlockSpec(memory_space=pl.ANY),
        ],
        out_specs=pl.BlockSpec(memory_space=pl.ANY),
        scratch_shapes=[pltpu.SemaphoreType.DMA(())],
        input_output_aliases={2: 0},
        compiler_params=_cparams(("arbitrary",)),
        name="scatter_rows",
    )(dest3, h2t, xs0)


def _moe_kernel(be_ref, x_ref, wg_ref, wu_ref, wd_ref, y_ref, wgb, wub, wdb):
    i = pl.program_id(0)

    @pl.when((i == 0) | (be_ref[i] != be_ref[jnp.maximum(i - 1, 0)]))
    def _():
        wgb[...] = wg_ref[0, 0].astype(BF16)
        wub[...] = wu_ref[0, 0].astype(BF16)
        wdb[...] = wd_ref[0, 0].astype(BF16)

    xb = _from_token_tiles(x_ref, 0, x_ref.shape[0] // SUBLANES).astype(BF16)
    g = _dot(xb, wgb[...])
    hid = (g * jax.nn.sigmoid(g)) * _dot(xb, wub[...])
    _to_token_tiles(y_ref, _dot(hid.astype(BF16), wdb[...]))


def _moe(layer, blk_expert, xs, wgate, wup, wdown, blk):
    n_blk = blk_expert.shape[0]
    D = wgate.shape[2]
    rows = pl.BlockSpec((blk * SUBLANES, LANES), lambda i, be: (i, 0))
    grid_spec = pltpu.PrefetchScalarGridSpec(
        num_scalar_prefetch=1,
        grid=(n_blk,),
        in_specs=[
            rows,
            pl.BlockSpec((1, 1, D, D_EXPERT), lambda i, be: (layer, be[i], 0, 0)),
            pl.BlockSpec((1, 1, D, D_EXPERT), lambda i, be: (layer, be[i], 0, 0)),
            pl.BlockSpec((1, 1, D_EXPERT, D), lambda i, be: (layer, be[i], 0, 0)),
        ],
        out_specs=rows,
        scratch_shapes=[pltpu.VMEM((D, D_EXPERT), BF16), pltpu.VMEM((D, D_EXPERT), BF16),
                        pltpu.VMEM((D_EXPERT, D), BF16)],
    )
    return pl.pallas_call(
        _moe_kernel,
        out_shape=jax.ShapeDtypeStruct(xs.shape, F32),
        grid_spec=grid_spec,
        compiler_params=_cparams(("arbitrary",)),
        name="moe",
    )(blk_expert, xs, wgate, wup, wdown)


def _comb_kernel(TM, final, d_ref, dn_ref, x1_ref, rt_ref, gt2_ref, gf_ref, ys_hbm, o_ref, buf, sem):
    i = pl.program_id(0)
    n = pl.num_programs(0)
    slot = lax.rem(i, 2)

    @pl.when(i == 0)
    def _():
        _tile_gather(ys_hbm, d_ref, buf.at[0], sem.at[0], 2 * TM)

    @pl.when(i + 1 < n)
    def _():
        _tile_gather(ys_hbm, dn_ref, buf.at[1 - slot], sem.at[1 - slot], 2 * TM)

    _tile_gather_wait(ys_hbm, buf.at[slot], sem.at[slot], 2 * TM)
    rt = rt_ref[...]
    y0 = _from_token_tiles(buf.at[slot], 0, TM)
    y1 = _from_token_tiles(buf.at[slot], TM, TM)
    mix = rt[:, TOP_K_IN_GROUP:TOP_K_IN_GROUP + 1] * y0 + rt[:, TOP_K_IN_GROUP + 1:TOP_K_IN_GROUP + 2] * y1
    x2 = x1_ref[...] + gt2_ref[0] * mix
    if final:
        x2 = _rms(x2) * gf_ref[...]
    o_ref[...] = x2


def _combine(dest3, x1, route, gt2, gf, ys, S, tm, final):
    T, D = x1.shape
    n = T // tm
    per_b = S // tm
    return pl.pallas_call(
        functools.partial(_comb_kernel, tm, final),
        out_shape=jax.ShapeDtypeStruct((T, D), F32),
        grid=(n,),
        in_specs=[
            pl.BlockSpec((1, 1, 2 * tm), lambda i: (i, 0, 0), memory_space=pltpu.SMEM),
            pl.BlockSpec((1, 1, 2 * tm), lambda i: (jnp.minimum(i + 1, n - 1), 0, 0), memory_space=pltpu.SMEM),
            pl.BlockSpec((tm, D), lambda i: (i, 0)),
            pl.BlockSpec((tm, LANES), lambda i: (i, 0)),
            pl.BlockSpec((1, 1, D), lambda i: (i // per_b, 0, 0)),
            pl.BlockSpec((1, D), lambda i: (0, 0)),
            pl.BlockSpec(memory_space=pl.ANY),
        ],
        out_specs=pl.BlockSpec((tm, D), lambda i: (i, 0)),
        scratch_shapes=[pltpu.VMEM((2, 2 * tm * SUBLANES, LANES), F32), pltpu.SemaphoreType.DMA((2,))],
        compiler_params=_cparams(("arbitrary",)),
        name="combine",
    )(dest3, dest3, x1, route, gt2, gf, ys)


def _dispatch(route, tile_cnt, blk):
    T = route.shape[0]
    N = T * TOP_K_IN_GROUP
    n_tiles = tile_cnt.shape[0]
    eid = route[:, 0:TOP_K_IN_GROUP].astype(I32).reshape(N)
    rank_in_tile = route[:, 2 * TOP_K_IN_GROUP:3 * TOP_K_IN_GROUP].astype(I32).reshape(N)
    tile_start = jnp.cumsum(tile_cnt, axis=0) - tile_cnt
    counts = jnp.sum(tile_cnt, axis=0)
    padded = (counts + blk - 1) // blk * blk
    pad_end = jnp.cumsum(padded)
    pad_start = pad_end - padded
    base = jnp.repeat(pad_start[None, :] + tile_start, N // n_tiles, axis=0)
    hit = eid[:, None] == jnp.arange(N_EXPERTS, dtype=I32)[None, :]
    dest = jnp.sum(jnp.where(hit, base, 0), axis=1) + rank_in_tile
    n_blk = (N + N_EXPERTS * blk) // blk
    blk_start = jnp.arange(n_blk, dtype=I32)[:, None] * blk
    blk_expert = jnp.minimum(jnp.sum((pad_end[None, :] <= blk_start).astype(I32), axis=1), N_EXPERTS - 1)
    return blk_expert, dest.reshape(T, TOP_K_IN_GROUP)


def _tile_plan(dest, tm):
    T = dest.shape[0]
    return dest.reshape(T // tm, tm, TOP_K_IN_GROUP).transpose(0, 2, 1).reshape(T // tm, 1, TOP_K_IN_GROUP * tm)


def kernel(x, c, w_mod, b_mod, g_norm1, g_norm2, w_in, g_cq, g_ckv, g_kidx, w_q_up, w_idx_q, w_v_up,
           lb_logits, g_rec, w_branch_a, w_branch_r, w_out, w_grp, b_grp, w_exp_router, b_exp_router,
           w_gate, w_up, w_down, g_final):
    B, S, D = x.shape
    L = w_mod.shape[0]
    T = B * S
    tiles = _tiles(S)
    tm, ts, tmc, blk = tiles["tm"], tiles["ts"], tiles["tmc"], tiles["blk"]
    kw = N_REC_HEADS * REC_K_DIM
    na = Q_RANK + KV_RANK + IDX_DIM + IDX_HEADS
    n_rec = 2 * kw + 2 * REC_WIDTH

    mod = _modulation(c, w_mod, b_mod)
    for l in range(L):
        m6 = mod[l].reshape(B, 6, 1, D)
        sh1, sc1, gt1, sh2, sc2, gt2 = (m6[:, j] for j in range(6))
        wa = jnp.pad(w_in[l, :, :na], ((0, 0), (0, 512 - na))).astype(BF16)
        wrec = w_in[l, :, na:na + n_rec].astype(BF16)
        wg = w_in[l, :, na + n_rec:].astype(BF16)
        gk = jnp.pad(g_kidx[l], (0, LANES - IDX_DIM)).reshape(1, LANES)
        (q4, qi4, kv, kvt, kidx2, wt, qs, kk, vv, vt, lf, og, sga, sgr) = _proj(
            l, x, sh1, sc1, g_norm1[l].reshape(1, D), wa, wrec, wg, g_cq[l].reshape(1, -1),
            g_ckv[l].reshape(1, -1), gk, w_q_up[l].astype(BF16), w_idx_q[l].astype(BF16), lb_logits, tm)
        wvt = jnp.swapaxes(w_v_up[l], 1, 2).astype(BF16)
        ya = _dsa(q4, qi4, wt, kv, kvt, kidx2, wvt)
        yr = _hgrn(qs, kk, vv, vt, lf, og, jnp.tile(g_rec[l], N_REC_HEADS).reshape(1, REC_WIDTH), ts)
        wr = jnp.pad(jnp.concatenate([w_grp[l], w_exp_router[l]], axis=1),
                     ((0, 0), (0, LANES - N_GROUPS - N_EXPERTS)))
        wrh, wrl = _hi_lo(wr)
        br = jnp.pad(jnp.concatenate([b_grp[l], b_exp_router[l]]), (0, LANES - N_GROUPS - N_EXPERTS)).reshape(1, LANES)
        x1, h2t, route, tile_cnt = _merge(x, ya, yr, sga, sgr, gt1, sh2, sc2, g_norm2[l].reshape(1, D),
                               w_branch_a[l].astype(BF16), w_branch_r[l].astype(BF16), w_out[l].astype(BF16),
                               wrh, wrl, br, tm)
        route = route.reshape(T, LANES)
        tile_cnt = tile_cnt[:, :, 0, :N_EXPERTS].reshape(-1, N_EXPERTS).astype(I32)
        blk_expert, dest = _dispatch(route, tile_cnt, blk)
        xs = _scatter_rows(_tile_plan(dest, tm), h2t.reshape(T * SUBLANES, LANES),
                           TOP_K_IN_GROUP * T + N_EXPERTS * blk, tm)
        ys = _moe(l, blk_expert, xs, w_gate, w_up, w_down, blk)
        x = _combine(_tile_plan(dest, tmc), x1.reshape(T, D), route, gt2, g_final.reshape(1, D), ys, S, tmc,
                     l == L - 1).reshape(B, S, D)
    return x
```

```python
import functools

import jax
import jax.numpy as jnp
from jax import lax
from jax.experimental import pallas as pl
from jax.experimental.pallas import tpu as pltpu

N_ATTN_HEADS = 8
Q_RANK = 256
KV_RANK = 128
ATTN_V_DIM = 64
ATTN_WIDTH = N_ATTN_HEADS * ATTN_V_DIM
ATTN_SCALE = KV_RANK ** -0.5
IDX_HEADS = 8
IDX_DIM = 64
IDX_W_SCALE = (IDX_HEADS * IDX_DIM) ** -0.5
TOPK_MAX = 256
N_REC_HEADS = 8
REC_K_DIM = 64
REC_V_DIM = 64
REC_WIDTH = N_REC_HEADS * REC_V_DIM
N_GROUPS = 4
EXPERTS_PER_GROUP = 8
N_EXPERTS = N_GROUPS * EXPERTS_PER_GROUP
TOP_K_IN_GROUP = 2
D_EXPERT = 512
EPS = 1e-6
NEG = -1e30
TINY = 1e-30

LANES = 128
SUBLANES = 8
Q_BLOCK = 128
COUNT_ACCS = 8
ONES_ROWS = 16
LOG2E = 1.4426950408889634
SUB = 16
DIAG_GROUP = 4
DMA_PRIORITIES = 2
MERGE_SPLIT = 2
VMEM_LIMIT = 56 * 1024 * 1024


def _tiles(S):
    return dict(
        tm=min(512, S),
        ts=min(256, S),
        tmc=min(256, S),
        kc=min(512, S),
        blk=512,
    )

F32 = jnp.float32
BF16 = jnp.bfloat16
I32 = jnp.int32


def _cparams(sem):
    return pltpu.CompilerParams(dimension_semantics=sem, vmem_limit_bytes=VMEM_LIMIT)


def _nt_dot(a, b):
    return lax.dot_general(a, b, (((1,), (1,)), ((), ())), preferred_element_type=F32)


def _dot(a, b):
    return jnp.dot(a, b, preferred_element_type=F32)


def _split_dot(a_f32, b_hi, b_lo):
    a_hi = a_f32.astype(BF16)
    a_lo = (a_f32 - a_hi.astype(F32)).astype(BF16)
    return _dot(a_hi, b_hi) + (_dot(a_hi, b_lo) + _dot(a_lo, b_hi))


def _hi_lo(w):
    hi = w.astype(BF16)
    lo = (w - hi.astype(F32)).astype(BF16)
    return hi, lo


def _mod_kernel(c_ref, w_ref, b_ref, o_ref):
    c = c_ref[...]
    ca = c * jax.nn.sigmoid(c)
    w = w_ref[0]
    w_hi = w.astype(BF16)
    w_lo = (w - w_hi.astype(F32)).astype(BF16)
    o_ref[0] = _split_dot(ca, w_hi, w_lo) + b_ref[0]


def _modulation(c, w_mod, b_mod):
    L, D, D6 = w_mod.shape
    B = c.shape[0]
    tn = 1024
    return pl.pallas_call(
        _mod_kernel,
        out_shape=jax.ShapeDtypeStruct((L, B, D6), F32),
        grid=(L, D6 // tn),
        in_specs=[
            pl.BlockSpec((B, D), lambda l, n: (0, 0)),
            pl.BlockSpec((1, D, tn), lambda l, n: (l, 0, n)),
            pl.BlockSpec((1, 1, tn), lambda l, n: (l, 0, n)),
        ],
        out_specs=pl.BlockSpec((1, B, tn), lambda l, n: (l, 0, n)),
        compiler_params=_cparams(("arbitrary", "arbitrary")),
        name="modulation",
    )(c, w_mod, b_mod.reshape(L, 1, D6))


def _rms(x, eps=EPS):
    return x * lax.rsqrt(jnp.mean(x * x, axis=-1, keepdims=True) + eps)


def _proj_kernel(layer, x_ref, sh_ref, sc_ref, g1_ref, wa_ref, wrec_ref, wg_ref, gcq_ref, gckv_ref,
                 gk_ref, wq_ref, wiq_ref, lb_ref,
                 q_out, qi_out, kv_out, kvt_out, kidx_out, wt_out, qs_out, k_out, v_out, vt_out, lf_out, og_out,
                 sga_out, sgr_out):
    x = x_ref[0]
    tm = x.shape[0]
    h = (_rms(x) * g1_ref[...]) * (1.0 + sc_ref[0]) + sh_ref[0]
    hb = h.astype(BF16)

    ua = _dot(hb, wa_ref[...])
    cq = (_rms(ua[:, :Q_RANK]) * gcq_ref[...]).astype(BF16)
    q = (_dot(cq, wq_ref[...]) * (ATTN_SCALE * LOG2E)).astype(BF16)
    qi = _dot(cq, wiq_ref[...]).astype(BF16)
    for qb in range(tm // Q_BLOCK):
        rows = slice(qb * Q_BLOCK, (qb + 1) * Q_BLOCK)
        for hd in range(N_ATTN_HEADS):
            q_out[0, qb, hd * Q_BLOCK:(hd + 1) * Q_BLOCK, :] = q[rows, hd * LANES:(hd + 1) * LANES]
        for t in range(IDX_HEADS * IDX_DIM // LANES):
            qi_out[0, qb, t * Q_BLOCK:(t + 1) * Q_BLOCK, :] = qi[rows, t * LANES:(t + 1) * LANES]
    kvn = _rms(ua[:, Q_RANK:Q_RANK + KV_RANK]) * gckv_ref[...]
    kv_out[0] = kvn.astype(BF16)
    kvt_out[0] = kvn.T.astype(BF16)

    t3 = ua[:, Q_RANK + KV_RANK:]
    lane = lax.broadcasted_iota(I32, t3.shape, 1)
    ms = jnp.sum(jnp.where(lane < IDX_DIM, t3 * t3, 0.0), axis=-1, keepdims=True) * (1.0 / IDX_DIM)
    ka = t3 * lax.rsqrt(ms + EPS) * gk_ref[...]
    kb = pltpu.roll(ka, IDX_DIM, 1)
    kidx_out[0] = jnp.concatenate([ka, kb], axis=1).astype(BF16)
    wt_out[0] = (t3 * IDX_W_SCALE).T[IDX_DIM:IDX_DIM + IDX_HEADS, :]

    ur = _dot(hb, wrec_ref[...])
    kw = N_REC_HEADS * REC_K_DIM
    lbl = lb_ref[...]
    e = jnp.exp(lbl - jnp.max(lbl, axis=0, keepdims=True))
    p = e / jnp.sum(e, axis=0, keepdims=True)
    lb = jnp.zeros((1, kw), F32)
    for j in range(1, layer + 1):
        lb = lb + p[j:j + 1, :]
    lb = jnp.clip(lb, 0.0, 1.0)
    sig = jax.nn.sigmoid(ur[:, kw:2 * kw])
    f = lb + (1.0 - lb) * sig
    lf_out[0] = jnp.log(jnp.maximum(f, TINY))
    k_out[0] = ((1.0 - lb) * (1.0 - sig)).astype(BF16)
    qr = ur[:, :kw]
    qs_out[0] = (qr * jax.nn.sigmoid(qr)).astype(BF16)
    vr = ur[:, 2 * kw:2 * kw + REC_WIDTH]
    v_out[0] = vr.astype(BF16)
    vt_out[0] = vr.T.astype(BF16)
    og = ur[:, 2 * kw + REC_WIDTH:]
    og_out[0] = (og * jax.nn.sigmoid(og)).astype(BF16)

    ug = _dot(hb, wg_ref[...])
    D = x.shape[1]
    sga_out[0] = jax.nn.sigmoid(ug[:, :D]).astype(BF16)
    sgr_out[0] = jax.nn.sigmoid(ug[:, D:]).astype(BF16)


def _proj(layer, x, sh1, sc1, g1, wa, wrec, wg, gcq, gckv, gk, wq, wiq, lb_logits, tm):
    B, S, D = x.shape
    nqb = tm // Q_BLOCK
    nq = S // Q_BLOCK
    kw = N_REC_HEADS * REC_K_DIM
    c2 = lambda b, i: (0, 0)
    tok = lambda w: pl.BlockSpec((1, tm, w), lambda b, i: (b, i, 0))
    full = lambda a: pl.BlockSpec(a.shape, c2)
    out_shapes = (
        jax.ShapeDtypeStruct((B, nq, N_ATTN_HEADS * Q_BLOCK, KV_RANK), BF16),
        jax.ShapeDtypeStruct((B, nq, IDX_HEADS * IDX_DIM // LANES * Q_BLOCK, LANES), BF16),
        jax.ShapeDtypeStruct((B, S, KV_RANK), BF16),
        jax.ShapeDtypeStruct((B, KV_RANK, S), BF16),
        jax.ShapeDtypeStruct((B, S, 2 * LANES), BF16),
        jax.ShapeDtypeStruct((B, IDX_HEADS, S), F32),
        jax.ShapeDtypeStruct((B, S, kw), BF16),
        jax.ShapeDtypeStruct((B, S, kw), BF16),
        jax.ShapeDtypeStruct((B, S, REC_WIDTH), BF16),
        jax.ShapeDtypeStruct((B, REC_WIDTH, S), BF16),
        jax.ShapeDtypeStruct((B, S, kw), F32),
        jax.ShapeDtypeStruct((B, S, REC_WIDTH), BF16),
        jax.ShapeDtypeStruct((B, S, D), BF16),
        jax.ShapeDtypeStruct((B, S, D), BF16),
    )
    out_specs = (
        pl.BlockSpec((1, nqb, N_ATTN_HEADS * Q_BLOCK, KV_RANK), lambda b, i: (b, i, 0, 0)),
        pl.BlockSpec((1, nqb, IDX_HEADS * IDX_DIM // LANES * Q_BLOCK, LANES), lambda b, i: (b, i, 0, 0)),
        tok(KV_RANK),
        pl.BlockSpec((1, KV_RANK, tm), lambda b, i: (b, 0, i)),
        tok(2 * LANES),
        pl.BlockSpec((1, IDX_HEADS, tm), lambda b, i: (b, 0, i)),
        tok(kw), tok(kw), tok(REC_WIDTH),
        pl.BlockSpec((1, REC_WIDTH, tm), lambda b, i: (b, 0, i)),
        tok(kw), tok(REC_WIDTH),
        tok(D), tok(D),
    )
    vec = lambda: pl.BlockSpec((1, 1, D), lambda b, i: (b, 0, 0))
    return pl.pallas_call(
        functools.partial(_proj_kernel, layer),
        out_shape=out_shapes,
        grid=(B, S // tm),
        in_specs=[tok(D), vec(), vec(), full(g1), full(wa), full(wrec), full(wg), full(gcq), full(gckv),
                  full(gk), full(wq), full(wiq), full(lb_logits)],
        out_specs=out_specs,
        compiler_params=_cparams(("arbitrary", "arbitrary")),
        name="proj",
    )(x, sh1, sc1, g1, wa, wrec, wg, gcq, gckv, gk, wq, wiq, lb_logits)


def _loop_by_four(n, body):
    def quad(p, carry):
        for j in range(4):
            body(4 * p + j, carry)
        return carry
    lax.fori_loop(0, n // 4, quad, 0)
    done = n // 4 * 4

    @pl.when(n % 4 >= 2)
    def _():
        body(done, 0)
        body(done + 1, 0)

    @pl.when(n % 2 == 1)
    def _():
        body(n - 1, 0)


def _sort_key(score):
    bits = lax.bitcast_convert_type(score, I32)
    return bits ^ (lax.shift_right_arithmetic(bits, 31) & jnp.int32(0x7FFFFFFF))


def _dsa_kernel(S, KC, KA, TOPK, q_ref, qi_ref, wt_ref, kv_ref, kvt_ref, kidx_ref, wvt_ref, ya_ref,
                keys_ref, acc_ref, m_ref, a_ref, p_ref):
    i = pl.program_id(1)
    n_ch = ((i + 1) * Q_BLOCK + KC - 1) // KC
    qpos = i * Q_BLOCK + lax.broadcasted_iota(I32, (1, Q_BLOCK), 1)
    wt = wt_ref[0]
    wrow = [wt[hd:hd + 1, :] for hd in range(IDX_HEADS)]
    qi = qi_ref[0, 0]
    n_t = IDX_HEADS * IDX_DIM // LANES

    def kpos_of(c):
        return c * KC + lax.broadcasted_iota(I32, (KC, 1), 0)

    def score_body(c, carry):
        koff = pl.multiple_of(c * KC, KC)
        la = _nt_dot(kidx_ref[0, pl.ds(koff, KC), 0:LANES], qi)
        lb = _nt_dot(kidx_ref[0, pl.ds(koff, KC), LANES:2 * LANES], qi)
        score = jnp.zeros((KC, Q_BLOCK), F32)
        for t in range(n_t):
            lanes = slice(t * Q_BLOCK, (t + 1) * Q_BLOCK)
            score = score + wrow[2 * t] * jnp.maximum(la[:, lanes], 0.0)
            score = score + wrow[2 * t + 1] * jnp.maximum(lb[:, lanes], 0.0)
        score = jnp.where(kpos_of(c) <= qpos, score, NEG)
        keys_ref[pl.ds(koff, KC), :] = _sort_key(score)
        return carry

    _loop_by_four(n_ch, score_body)

    def count(pred):
        def body(c, accs):
            kk = keys_ref[pl.ds(pl.multiple_of(c * KC, KC), KC), :]
            accs = list(accs)
            for r in range(KC // SUBLANES):
                a = accs[r % COUNT_ACCS]
                accs[r % COUNT_ACCS] = jnp.where(pred(kk[r * SUBLANES:(r + 1) * SUBLANES]), a + 1, a)
            return tuple(accs)
        accs = lax.fori_loop(0, n_ch, body, tuple(jnp.zeros((SUBLANES, Q_BLOCK), I32) for _ in range(COUNT_ACCS)))
        return jnp.sum(functools.reduce(lambda x, y: x + y, accs), axis=0, keepdims=True)

    def bis_body(it, carry):
        v, n_ge = carry
        cand = v + lax.shift_left(jnp.int32(1), 31 - it)
        cnt = count(lambda kk: kk >= cand)
        ok = cnt >= TOPK
        return jnp.where(ok, cand, v), jnp.where(ok, cnt, n_ge)

    v, n_ge = lax.fori_loop(0, 32, bis_body, (jnp.full((1, Q_BLOCK), -2 ** 31, I32),
                                              jnp.full((1, Q_BLOCK), n_ch * KC, I32)))
    any_split = jnp.max(jnp.where(n_ge > TOPK, 1, 0)) > 0

    @pl.when(any_split)
    def _():
        need = (TOPK - count(lambda kk: kk > v)).astype(F32)
        tri = jnp.where(lax.broadcasted_iota(I32, (KC, KC), 1) <= lax.broadcasted_iota(I32, (KC, KC), 0),
                        1.0, 0.0).astype(BF16)

        def tie_body(c, seen):
            koff = pl.multiple_of(c * KC, KC)
            kk = keys_ref[pl.ds(koff, KC), :]
            tie = kk == v
            tie_f = jnp.where(tie, 1.0, 0.0)
            rank = _dot(tri, tie_f.astype(BF16)) + seen
            keys_ref[pl.ds(koff, KC), :] = jnp.where(tie & (rank > need), v - 1, kk)
            return seen + jnp.sum(tie_f, axis=0, keepdims=True)

        lax.fori_loop(0, n_ch, tie_body, jnp.zeros((1, Q_BLOCK), F32))

    m_ref[...] = jnp.full(m_ref.shape, NEG, F32)
    acc_ref[...] = jnp.zeros(acc_ref.shape, F32)
    ones_rows = jnp.ones((ONES_ROWS, KA), BF16)

    def att_body(c, carry):
        koff = pl.multiple_of(c * KA, KA)
        kvc = kv_ref[0, pl.ds(koff, KA), :]
        kpos = c * KA + lax.broadcasted_iota(I32, (KA, 1), 0)
        bias = jnp.where((keys_ref[pl.ds(koff, KA), :] >= v) & (kpos <= qpos), 0.0, -jnp.inf)
        for hd in range(N_ATTN_HEADS):
            lanes = slice(hd * Q_BLOCK, (hd + 1) * Q_BLOCK)
            s = _nt_dot(kvc, q_ref[0, 0, hd * Q_BLOCK:(hd + 1) * Q_BLOCK, :]) + bias
            m_prev = m_ref[:, lanes]
            m_next = jnp.maximum(m_prev, jnp.max(s, axis=0, keepdims=True))
            a_ref[:, lanes] = jnp.exp2(m_prev - m_next)
            m_ref[:, lanes] = m_next
            p_ref[:, lanes] = jnp.exp2(s - m_next).astype(BF16)
        kvt1 = jnp.concatenate([kvt_ref[0, :, pl.ds(koff, KA)], ones_rows], axis=0)
        acc_ref[...] = acc_ref[...] * a_ref[...] + _dot(kvt1, p_ref[...])
        return carry

    _loop_by_four(((i + 1) * Q_BLOCK + KA - 1) // KA, att_body)
    o = (acc_ref[0:KV_RANK, :] / acc_ref[KV_RANK:KV_RANK + 1, :]).astype(BF16)
    ya_t = jnp.concatenate([_dot(wvt_ref[hd], o[:, hd * Q_BLOCK:(hd + 1) * Q_BLOCK])
                            for hd in range(N_ATTN_HEADS)], axis=0)
    ya_ref[0] = ya_t.T.astype(BF16)


def _dsa(q4, qi4, wt, kv, kvt, kidx2, wvt):
    B, nq = q4.shape[0], q4.shape[1]
    S = kv.shape[1]
    KC = KA = _tiles(S)["kc"]
    topk = min(TOPK_MAX, S // 4)
    assert S % KC == 0 and KC % KA == 0 and KA % Q_BLOCK == 0
    hq = N_ATTN_HEADS * Q_BLOCK
    return pl.pallas_call(
        functools.partial(_dsa_kernel, S, KC, KA, topk),
        out_shape=jax.ShapeDtypeStruct((B, S, ATTN_WIDTH), BF16),
        grid=(B, nq),
        in_specs=[
            pl.BlockSpec((1, 1) + q4.shape[2:], lambda b, i: (b, i, 0, 0)),
            pl.BlockSpec((1, 1) + qi4.shape[2:], lambda b, i: (b, i, 0, 0)),
            pl.BlockSpec((1, IDX_HEADS, Q_BLOCK), lambda b, i: (b, 0, i)),
            pl.BlockSpec((1, S, KV_RANK), lambda b, i: (b, 0, 0)),
            pl.BlockSpec((1, KV_RANK, S), lambda b, i: (b, 0, 0)),
            pl.BlockSpec((1, S, 2 * LANES), lambda b, i: (b, 0, 0)),
            pl.BlockSpec(wvt.shape, lambda b, i: (0, 0, 0)),
        ],
        out_specs=pl.BlockSpec((1, Q_BLOCK, ATTN_WIDTH), lambda b, i: (b, i, 0)),
        scratch_shapes=[
            pltpu.VMEM((S, Q_BLOCK), I32),
            pltpu.VMEM((KV_RANK + ONES_ROWS, hq), F32),
            pltpu.VMEM((1, hq), F32),
            pltpu.VMEM((1, hq), F32),
            pltpu.VMEM((KA, hq), BF16),
        ],
        compiler_params=_cparams(("arbitrary", "arbitrary")),
        name="dsa",
    )(q4, qi4, wt, kv, kvt, kidx2, wvt)


def _hgrn_kernel(TS, qs_ref, k_ref, v_ref, vt_ref, lf_ref, og_ref, grec_ref, y_ref,
                 st_ref, aloc_ref, e_ref, qh_ref, kh_ref, o_ref, g_ref, sb_ref, kblk_ref):
    n_tiles = REC_WIDTH // LANES
    nsb = TS // SUB

    @pl.when(pl.program_id(1) == 0)
    def _():
        st_ref[...] = jnp.zeros(st_ref.shape, F32)

    a = lf_ref[0]
    r = lax.broadcasted_iota(I32, a.shape, 0) % SUB
    for sh in (1, 2, 4, 8):
        a = a + jnp.where(r >= sh, pltpu.roll(a, sh, 0), 0.0)
    aend = jnp.where(r == SUB - 1, a, 0.0)
    for sh in (1, 2, 4, 8):
        aend = aend + jnp.where(r + sh <= SUB - 1, pltpu.roll(aend, TS - sh, 0), 0.0)
    e = jnp.exp(a)
    aloc_ref[...] = a * LOG2E
    e_ref[...] = e
    qh_ref[...] = (qs_ref[0].astype(F32) * e).astype(BF16)
    kh_ref[...] = (k_ref[0].astype(F32) * jnp.exp(aend - a)).astype(BF16)

    half = REC_K_DIM
    li = lax.broadcasted_iota(I32, (LANES, LANES), 0) // half
    lj = lax.broadcasted_iota(I32, (LANES, LANES), 1) // half
    bd_f = jnp.where(li == lj, 1.0, 0.0)
    bd_b = bd_f.astype(BF16)
    gsel = jnp.where(lax.broadcasted_iota(I32, (SUB, SUB * SUB), 0)
                     == lax.broadcasted_iota(I32, (SUB, SUB * SUB), 1) // SUB, 1.0, 0.0).astype(BF16)
    srow = lax.broadcasted_iota(I32, (SUB, LANES), 0)

    def diag_body(g, carry):
        units = []
        for jj in range(DIAG_GROUP):
            rows = pl.ds(pl.multiple_of((g * DIAG_GROUP + jj) * SUB, SUB), SUB)
            for p in range(n_tiles):
                units.append((rows, slice(p * LANES, (p + 1) * LANES)))
        wsts = []
        for rows, lanes in units:
            a_blk = aloc_ref[rows, lanes]
            q_blk = qs_ref[0, rows, lanes].astype(F32)
            k_blk = k_ref[0, rows, lanes].astype(F32)
            ws = []
            for t in range(SUB):
                d = a_blk[t:t + 1, :] - a_blk
                w = jnp.exp2(jnp.where(srow <= t, d, -jnp.inf)) * (k_blk * q_blk[t:t + 1, :])
                ws.append(w.astype(BF16))
            wsts.append(jnp.concatenate(ws, axis=0))
        rexps = [_dot(wst, bd_b) for wst in wsts]
        m2s = []
        for (rows, lanes), rexp in zip(units, rexps):
            v_f = v_ref[0, rows, lanes].astype(F32)
            m2s.append((rexp * jnp.concatenate([v_f] * SUB, axis=0)).astype(BF16))
        outs = [_dot(gsel, m2) for m2 in m2s]
        for (rows, lanes), out in zip(units, outs):
            o_ref[rows, lanes] = out
        return carry

    lax.fori_loop(0, nsb // DIAG_GROUP, diag_body, 0)

    @pl.when((pl.program_id(0) == 0) & (pl.program_id(1) == 0))
    def _():
        kblk_ref[...] = jnp.zeros(kblk_ref.shape, BF16)

    for p in range(n_tiles):
        lanes = slice(p * LANES, (p + 1) * LANES)
        for j in range(nsb):
            kblk_ref[j * SUB:(j + 1) * SUB, j * LANES:(j + 1) * LANES] = kh_ref[j * SUB:(j + 1) * SUB, lanes]
        g_ref[p] = _dot(vt_ref[0, lanes, :], kblk_ref[...])

    for p in range(n_tiles):
        lanes = slice(p * LANES, (p + 1) * LANES)
        st = st_ref[p]
        for j in range(nsb):
            sb_ref[j, p] = st.astype(BF16)
            dec = e_ref[(j + 1) * SUB - 1:(j + 1) * SUB, lanes]
            st = st * dec + g_ref[p, :, j * LANES:(j + 1) * LANES] * bd_f
        st_ref[p] = st

    for j in range(nsb):
        rows = slice(j * SUB, (j + 1) * SUB)
        for p in range(n_tiles):
            lanes = slice(p * LANES, (p + 1) * LANES)
            o_ref[rows, lanes] = o_ref[rows, lanes] + _nt_dot(qh_ref[rows, lanes], sb_ref[j, p])

    o = o_ref[...]
    o2 = o * o
    hi = o2.astype(BF16)
    lo = (o2 - hi.astype(F32)).astype(BF16)
    ms = jnp.concatenate(
        [_dot(hi[:, p * LANES:(p + 1) * LANES], bd_b) + _dot(lo[:, p * LANES:(p + 1) * LANES], bd_b)
         for p in range(n_tiles)], axis=1) * (1.0 / REC_V_DIM)
    y = o * lax.rsqrt(ms + EPS) * grec_ref[...]
    y_ref[0] = (y * og_ref[0].astype(F32)).astype(BF16)


def _hgrn(qs, k, v, vt, lf, og, grec_t, ts):
    B, S, W = qs.shape
    nsb = ts // SUB
    n_tiles = W // LANES
    tok = lambda: pl.BlockSpec((1, ts, W), lambda b, i: (b, i, 0))
    return pl.pallas_call(
        functools.partial(_hgrn_kernel, ts),
        out_shape=jax.ShapeDtypeStruct((B, S, W), BF16),
        grid=(B, S // ts),
        in_specs=[tok(), tok(), tok(), pl.BlockSpec((1, W, ts), lambda b, i: (b, 0, i)), tok(), tok(),
                  pl.BlockSpec((1, W), lambda b, i: (0, 0))],
        out_specs=tok(),
        scratch_shapes=[
            pltpu.VMEM((n_tiles, LANES, LANES), F32),
            pltpu.VMEM((ts, W), F32),
            pltpu.VMEM((ts, W), F32),
            pltpu.VMEM((ts, W), BF16),
            pltpu.VMEM((ts, W), BF16),
            pltpu.VMEM((ts, W), F32),
            pltpu.VMEM((n_tiles, LANES, nsb * LANES), F32),
            pltpu.VMEM((nsb, n_tiles, LANES, LANES), BF16),
            pltpu.VMEM((ts, nsb * LANES), BF16),
        ],
        compiler_params=_cparams(("arbitrary", "arbitrary")),
        name="hgrn",
    )(qs, k, v, vt, lf, og, grec_t)


def _merge_kernel(x_ref, ya_ref, yr_ref, sga_ref, sgr_ref, gt1_ref, sh2_ref, sc2_ref, g2_ref,
                  wba_ref, wbr_ref, wo_ref, wrh_ref, wrl_ref, br_ref, x1_out, h2_out, rt_out, cnt_out):
    tm = x_ref.shape[1]
    groups = [slice(h * tm // MERGE_SPLIT, (h + 1) * tm // MERGE_SPLIT) for h in range(MERGE_SPLIT)]
    da = [_dot(ya_ref[0, r, :], wba_ref[...]) for r in groups]
    dr = [_dot(yr_ref[0, r, :], wbr_ref[...]) for r in groups]
    merged = [(sga_ref[0, r, :].astype(F32) * a + sgr_ref[0, r, :].astype(F32) * b).astype(BF16)
              for r, a, b in zip(groups, da, dr)]
    dout = [_dot(m, wo_ref[...]) for m in merged]
    x1s = [x_ref[0, r, :] + gt1_ref[0] * d for r, d in zip(groups, dout)]
    h2s = [(_rms(x1) * g2_ref[...]) * (1.0 + sc2_ref[0]) + sh2_ref[0] for x1 in x1s]
    lgs = [_split_dot(h2, wrh_ref[...], wrl_ref[...]) for h2 in h2s]
    x1_out[0] = jnp.concatenate(x1s, axis=0)
    _to_token_tiles(h2_out.at[0], jnp.concatenate(h2s, axis=0))

    lg = jnp.concatenate(lgs, axis=0) + br_ref[...]
    lane = lax.broadcasted_iota(I32, lg.shape, 1)
    big = jnp.int32(1 << 20)
    gl = jnp.where(lane < N_GROUPS, lg, -jnp.inf)
    gmax = jnp.max(gl, axis=1, keepdims=True)
    gsel = jnp.min(jnp.where(gl == gmax, lane, big), axis=1, keepdims=True)
    ggate = 1.0 / jnp.sum(jnp.exp(gl - gmax), axis=1, keepdims=True)
    lo = N_GROUPS + EXPERTS_PER_GROUP * gsel
    el = jnp.where((lane >= lo) & (lane < lo + EXPERTS_PER_GROUP), lg, -jnp.inf)
    v1 = jnp.max(el, axis=1, keepdims=True)
    i1 = jnp.min(jnp.where(el == v1, lane, big), axis=1, keepdims=True)
    el2 = jnp.where(lane == i1, -jnp.inf, el)
    v2 = jnp.max(el2, axis=1, keepdims=True)
    i2 = jnp.min(jnp.where(el2 == v2, lane, big), axis=1, keepdims=True)
    e2 = jnp.exp(v2 - v1)
    den = 1.0 + e2
    g1 = (1.0 / den) * ggate
    g2 = (e2 / den) * ggate
    oh1 = jnp.where(lane == i1 - N_GROUPS, 1.0, 0.0)
    oh2 = jnp.where(lane == i2 - N_GROUPS, 1.0, 0.0)
    ohs = (oh1 + oh2).astype(BF16)
    ltri = jnp.where(lax.broadcasted_iota(I32, (tm, tm), 1) < lax.broadcasted_iota(I32, (tm, tm), 0),
                     1.0, 0.0).astype(BF16)
    before = _dot(ltri, ohs)
    r1 = jnp.sum(before * oh1, axis=1, keepdims=True)
    r2 = jnp.sum(before * oh2, axis=1, keepdims=True)
    cnt_out[0, 0] = jnp.broadcast_to(jnp.sum(oh1 + oh2, axis=0, keepdims=True), (SUBLANES, LANES))
    vals = ((i1 - N_GROUPS).astype(F32), (i2 - N_GROUPS).astype(F32), g1, g2, r1, r2)
    rt = jnp.zeros(lg.shape, F32)
    for j, val in enumerate(vals):
        rt = jnp.where(lane == j, val, rt)
    rt_out[0] = rt


def _merge(x, ya, yr, sga, sgr, gt1, sh2, sc2, g2, wba, wbr, wo, wrh, wrl, br, tm):
    B, S, D = x.shape
    tok = lambda w: pl.BlockSpec((1, tm, w), lambda b, i: (b, i, 0))
    vec = lambda: pl.BlockSpec((1, 1, D), lambda b, i: (b, 0, 0))
    full = lambda a: pl.BlockSpec(a.shape, lambda b, i: (0,) * a.ndim)
    return pl.pallas_call(
        _merge_kernel,
        out_shape=(jax.ShapeDtypeStruct((B, S, D), F32), jax.ShapeDtypeStruct((B, S * SUBLANES, LANES), F32),
                   jax.ShapeDtypeStruct((B, S, LANES), F32),
                   jax.ShapeDtypeStruct((B, S // tm, SUBLANES, LANES), F32)),
        grid=(B, S // tm),
        in_specs=[tok(D), tok(ATTN_WIDTH), tok(REC_WIDTH), tok(D), tok(D), vec(), vec(), vec(), full(g2),
                  full(wba), full(wbr), full(wo), full(wrh), full(wrl), full(br)],
        out_specs=(tok(D), pl.BlockSpec((1, tm * SUBLANES, LANES), lambda b, i: (b, i, 0)), tok(LANES),
                   pl.BlockSpec((1, 1, SUBLANES, LANES), lambda b, i: (b, i, 0, 0))),
        compiler_params=_cparams(("arbitrary", "arbitrary")),
        name="merge",
    )(x, ya, yr, sga, sgr, gt1, sh2, sc2, g2, wba, wbr, wo, wrh, wrl, br)


def _to_token_tiles(ref, x):
    n = x.shape[0]
    for s in range(SUBLANES):
        ref[pl.ds(s, n, stride=SUBLANES), :] = x[:, s * LANES:(s + 1) * LANES]


def _from_token_tiles(ref, first_tile, n):
    return jnp.concatenate([ref[pl.ds(first_tile * SUBLANES + s, n, stride=SUBLANES), :]
                            for s in range(SUBLANES)], axis=1)


def _tile_gather(src_hbm, idx_ref, dst, sem, n):
    def body(p, carry):
        for prio in range(DMA_PRIORITIES):
            r = p * DMA_PRIORITIES + prio
            src = src_hbm.at[pl.ds(pl.multiple_of(idx_ref[0, 0, r] * SUBLANES, SUBLANES), SUBLANES)]
            pltpu.make_async_copy(src, dst.at[pl.ds(pl.multiple_of(r * SUBLANES, SUBLANES), SUBLANES)],
                                  sem).start(priority=prio)
        return carry
    lax.fori_loop(0, n // DMA_PRIORITIES, body, 0, unroll=4)


def _tile_gather_wait(src_hbm, dst, sem, n):
    pltpu.make_async_copy(src_hbm.at[pl.ds(0, n * SUBLANES)], dst, sem).wait()


def _scatter_kernel(TM, d_ref, h2_ref, xs_in, xs_out, sem):
    del xs_in

    for j in range(TOP_K_IN_GROUP):
        def body(p, carry):
            for prio in range(DMA_PRIORITIES):
                t = p * DMA_PRIORITIES + prio
                src = h2_ref.at[pl.ds(pl.multiple_of(t * SUBLANES, SUBLANES), SUBLANES)]
                dst = xs_out.at[pl.ds(pl.multiple_of(d_ref[0, 0, j * TM + t] * SUBLANES, SUBLANES), SUBLANES)]
                pltpu.make_async_copy(src, dst, sem).start(priority=prio)
            return carry
        lax.fori_loop(0, TM // DMA_PRIORITIES, body, 0, unroll=4)
    for _ in range(TOP_K_IN_GROUP):
        pltpu.make_async_copy(h2_ref, xs_out.at[pl.ds(0, TM * SUBLANES)], sem).wait()


def _scatter_rows(dest3, h2t, n_rows, tm):
    n = dest3.shape[0]
    xs0 = jnp.zeros((n_rows * SUBLANES, LANES), F32)
    return pl.pallas_call(
        functools.partial(_scatter_kernel, tm),
        out_shape=jax.ShapeDtypeStruct(xs0.shape, F32),
        grid=(n,),
        in_specs=[
            pl.BlockSpec((1, 1, TOP_K_IN_GROUP * tm), lambda i: (i, 0, 0), memory_space=pltpu.SMEM),
            pl.BlockSpec((tm * SUBLANES, LANES), lambda i: (i, 0)),
            pl.BlockSpec(memory_space=pl.ANY),
        ],
        out_specs=pl.BlockSpec(memory_space=pl.ANY),
        scratch_shapes=[pltpu.SemaphoreType.DMA(())],
        input_output_aliases={2: 0},
        compiler_params=_cparams(("arbitrary",)),
        name="scatter_rows",
    )(dest3, h2t, xs0)


def _moe_kernel(be_ref, x_ref, wg_ref, wu_ref, wd_ref, y_ref, wgb, wub, wdb):
    i = pl.program_id(0)

    @pl.when((i == 0) | (be_ref[i] != be_ref[jnp.maximum(i - 1, 0)]))
    def _():
        wgb[...] = wg_ref[0, 0].astype(BF16)
        wub[...] = wu_ref[0, 0].astype(BF16)
        wdb[...] = wd_ref[0, 0].astype(BF16)

    xb = _from_token_tiles(x_ref, 0, x_ref.shape[0] // SUBLANES).astype(BF16)
    g = _dot(xb, wgb[...])
    hid = (g * jax.nn.sigmoid(g)) * _dot(xb, wub[...])
    _to_token_tiles(y_ref, _dot(hid.astype(BF16), wdb[...]))


def _moe(layer, blk_expert, xs, wgate, wup, wdown, blk):
    n_blk = blk_expert.shape[0]
    D = wgate.shape[2]
    rows = pl.BlockSpec((blk * SUBLANES, LANES), lambda i, be: (i, 0))
    grid_spec = pltpu.PrefetchScalarGridSpec(
        num_scalar_prefetch=1,
        grid=(n_blk,),
        in_specs=[
            rows,
            pl.BlockSpec((1, 1, D, D_EXPERT), lambda i, be: (layer, be[i], 0, 0)),
            pl.BlockSpec((1, 1, D, D_EXPERT), lambda i, be: (layer, be[i], 0, 0)),
            pl.BlockSpec((1, 1, D_EXPERT, D), lambda i, be: (layer, be[i], 0, 0)),
        ],
        out_specs=rows,
        scratch_shapes=[pltpu.VMEM((D, D_EXPERT), BF16), pltpu.VMEM((D, D_EXPERT), BF16),
                        pltpu.VMEM((D_EXPERT, D), BF16)],
    )
    return pl.pallas_call(
        _moe_kernel,
        out_shape=jax.ShapeDtypeStruct(xs.shape, F32),
        grid_spec=grid_spec,
        compiler_params=_cparams(("arbitrary",)),
        name="moe",
    )(blk_expert, xs, wgate, wup, wdown)


def _comb_kernel(TM, final, d_ref, dn_ref, x1_ref, rt_ref, gt2_ref, gf_ref, ys_hbm, o_ref, buf, sem):
    i = pl.program_id(0)
    n = pl.num_programs(0)
    slot = lax.rem(i, 2)

    @pl.when(i == 0)
    def _():
        _tile_gather(ys_hbm, d_ref, buf.at[0], sem.at[0], 2 * TM)

    @pl.when(i + 1 < n)
    def _():
        _tile_gather(ys_hbm, dn_ref, buf.at[1 - slot], sem.at[1 - slot], 2 * TM)

    _tile_gather_wait(ys_hbm, buf.at[slot], sem.at[slot], 2 * TM)
    rt = rt_ref[...]
    y0 = _from_token_tiles(buf.at[slot], 0, TM)
    y1 = _from_token_tiles(buf.at[slot], TM, TM)
    mix = rt[:, TOP_K_IN_GROUP:TOP_K_IN_GROUP + 1] * y0 + rt[:, TOP_K_IN_GROUP + 1:TOP_K_IN_GROUP + 2] * y1
    x2 = x1_ref[...] + gt2_ref[0] * mix
    if final:
        x2 = _rms(x2) * gf_ref[...]
    o_ref[...] = x2


def _combine(dest3, x1, route, gt2, gf, ys, S, tm, final):
    T, D = x1.shape
    n = T // tm
    per_b = S // tm
    return pl.pallas_call(
        functools.partial(_comb_kernel, tm, final),
        out_shape=jax.ShapeDtypeStruct((T, D), F32),
        grid=(n,),
        in_specs=[
            pl.BlockSpec((1, 1, 2 * tm), lambda i: (i, 0, 0), memory_space=pltpu.SMEM),
            pl.BlockSpec((1, 1, 2 * tm), lambda i: (jnp.minimum(i + 1, n - 1), 0, 0), memory_space=pltpu.SMEM),
            pl.BlockSpec((tm, D), lambda i: (i, 0)),
            pl.BlockSpec((tm, LANES), lambda i: (i, 0)),
            pl.BlockSpec((1, 1, D), lambda i: (i // per_b, 0, 0)),
            pl.BlockSpec((1, D), lambda i: (0, 0)),
            pl.BlockSpec(memory_space=pl.ANY),
        ],
        out_specs=pl.BlockSpec((tm, D), lambda i: (i, 0)),
        scratch_shapes=[pltpu.VMEM((2, 2 * tm * SUBLANES, LANES), F32), pltpu.SemaphoreType.DMA((2,))],
        compiler_params=_cparams(("arbitrary",)),
        name="combine",
    )(dest3, dest3, x1, route, gt2, gf, ys)


def _dispatch(route, tile_cnt, blk):
    T = route.shape[0]
    N = T * TOP_K_IN_GROUP
    n_tiles = tile_cnt.shape[0]
    eid = route[:, 0:TOP_K_IN_GROUP].astype(I32).reshape(N)
    rank_in_tile = route[:, 2 * TOP_K_IN_GROUP:3 * TOP_K_IN_GROUP].astype(I32).reshape(N)
    tile_start = jnp.cumsum(tile_cnt, axis=0) - tile_cnt
    counts = jnp.sum(tile_cnt, axis=0)
    padded = (counts + blk - 1) // blk * blk
    pad_end = jnp.cumsum(padded)
    pad_start = pad_end - padded
    base = jnp.repeat(pad_start[None, :] + tile_start, N // n_tiles, axis=0)
    hit = eid[:, None] == jnp.arange(N_EXPERTS, dtype=I32)[None, :]
    dest = jnp.sum(jnp.where(hit, base, 0), axis=1) + rank_in_tile
    n_blk = (N + N_EXPERTS * blk) // blk
    blk_start = jnp.arange(n_blk, dtype=I32)[:, None] * blk
    blk_expert = jnp.minimum(jnp.sum((pad_end[None, :] <= blk_start).astype(I32), axis=1), N_EXPERTS - 1)
    return blk_expert, dest.reshape(T, TOP_K_IN_GROUP)


def _tile_plan(dest, tm):
    T = dest.shape[0]
    return dest.reshape(T // tm, tm, TOP_K_IN_GROUP).transpose(0, 2, 1).reshape(T // tm, 1, TOP_K_IN_GROUP * tm)


def kernel(x, c, w_mod, b_mod, g_norm1, g_norm2, w_in, g_cq, g_ckv, g_kidx, w_q_up, w_idx_q, w_v_up,
           lb_logits, g_rec, w_branch_a, w_branch_r, w_out, w_grp, b_grp, w_exp_router, b_exp_router,
           w_gate, w_up, w_down, g_final):
    B, S, D = x.shape
    L = w_mod.shape[0]
    T = B * S
    tiles = _tiles(S)
    tm, ts, tmc, blk = tiles["tm"], tiles["ts"], tiles["tmc"], tiles["blk"]
    kw = N_REC_HEADS * REC_K_DIM
    na = Q_RANK + KV_RANK + IDX_DIM + IDX_HEADS
    n_rec = 2 * kw + 2 * REC_WIDTH

    mod = _modulation(c, w_mod, b_mod)
    for l in range(L):
        m6 = mod[l].reshape(B, 6, 1, D)
        sh1, sc1, gt1, sh2, sc2, gt2 = (m6[:, j] for j in range(6))
        wa = jnp.pad(w_in[l, :, :na], ((0, 0), (0, 512 - na))).astype(BF16)
        wrec = w_in[l, :, na:na + n_rec].astype(BF16)
        wg = w_in[l, :, na + n_rec:].astype(BF16)
        gk = jnp.pad(g_kidx[l], (0, LANES - IDX_DIM)).reshape(1, LANES)
        (q4, qi4, kv, kvt, kidx2, wt, qs, kk, vv, vt, lf, og, sga, sgr) = _proj(
            l, x, sh1, sc1, g_norm1[l].reshape(1, D), wa, wrec, wg, g_cq[l].reshape(1, -1),
            g_ckv[l].reshape(1, -1), gk, w_q_up[l].astype(BF16), w_idx_q[l].astype(BF16), lb_logits, tm)
        wvt = jnp.swapaxes(w_v_up[l], 1, 2).astype(BF16)
        ya = _dsa(q4, qi4, wt, kv, kvt, kidx2, wvt)
        yr = _hgrn(qs, kk, vv, vt, lf, og, jnp.tile(g_rec[l], N_REC_HEADS).reshape(1, REC_WIDTH), ts)
        wr = jnp.pad(jnp.concatenate([w_grp[l], w_exp_router[l]], axis=1),
                     ((0, 0), (0, LANES - N_GROUPS - N_EXPERTS)))
        wrh, wrl = _hi_lo(wr)
        br = jnp.pad(jnp.concatenate([b_grp[l], b_exp_router[l]]), (0, LANES - N_GROUPS - N_EXPERTS)).reshape(1, LANES)
        x1, h2t, route, tile_cnt = _merge(x, ya, yr, sga, sgr, gt1, sh2, sc2, g_norm2[l].reshape(1, D),
                               w_branch_a[l].astype(BF16), w_branch_r[l].astype(BF16), w_out[l].astype(BF16),
                               wrh, wrl, br, tm)
        route = route.reshape(T, LANES)
        tile_cnt = tile_cnt[:, :, 0, :N_EXPERTS].reshape(-1, N_EXPERTS).astype(I32)
        blk_expert, dest = _dispatch(route, tile_cnt, blk)
        xs = _scatter_rows(_tile_plan(dest, tm), h2t.reshape(T * SUBLANES, LANES),
                           TOP_K_IN_GROUP * T + N_EXPERTS * blk, tm)
        ys = _moe(l, blk_expert, xs, w_gate, w_up, w_down, blk)
        x = _combine(_tile_plan(dest, tmc), x1.reshape(T, D), route, gt2, g_final.reshape(1, D), ys, S, tmc,
                     l == L - 1).reshape(B, S, D)
    return x
```
